```python
import math
import jax, jax.numpy as jnp
from jax import lax
import numpy as np

D_MODEL = 2048
BATCH = 1
SEQ = 16384
DEPTH = 4

GRID_W = 64
CTX_LEN = 256
N_GROUPS = 4
GROUP_WIDTH = D_MODEL // N_GROUPS
D_MIX = N_GROUPS * GROUP_WIDTH
HEAD_DIM = 64
QBLOCK = 128
A_DH = HEAD_DIM
A_HEADS = GROUP_WIDTH // (2 * A_DH)
B_GDIM = 128
B_GROUPS = GROUP_WIDTH // B_GDIM
CHUNK = 128
C_DH = HEAD_DIM
C_HEADS = GROUP_WIDTH // C_DH
C_KV_HEADS = C_HEADS // 4
WINDOW = 128
D_DH = HEAD_DIM
D_HEADS = GROUP_WIDTH // D_DH
NA_ROWS = 8
NA_COLS = 16
D_FF = 256 * math.ceil(8 * D_MODEL / 3 / 256)
N_MOD = 9
ROPE_THETA = 10000.0
LN_EPS = 1e-6
NEG_INF = -1e30
DEEPNORM_ALPHA = (2 * DEPTH) ** 0.25
DEEPNORM_BETA = (8 * DEPTH) ** -0.25
PROJ_SPLITS = (GROUP_WIDTH, GROUP_WIDTH, GROUP_WIDTH,
               GROUP_WIDTH, GROUP_WIDTH,
               GROUP_WIDTH, C_KV_HEADS * C_DH, C_KV_HEADS * C_DH,
               GROUP_WIDTH, GROUP_WIDTH, GROUP_WIDTH)
D_PROJ = sum(PROJ_SPLITS)

kernel_name = 'hybrid_parallel_group_flow_block'


def layer_norm(t, g, b):
    tf = t.astype(jnp.float32)
    tc = tf - jnp.mean(tf, axis=-1, keepdims=True)
    var = jnp.mean(tc * tc, axis=-1, keepdims=True)
    return (tc * lax.rsqrt(var + LN_EPS) * g.astype(jnp.float32) + b.astype(jnp.float32)).astype(t.dtype)


def rms_norm(t):
    tf = t.astype(jnp.float32)
    return (tf * lax.rsqrt(jnp.mean(tf * tf, axis=-1, keepdims=True) + LN_EPS)).astype(t.dtype)


def modulate(t, shift, scale):
    return t * (1.0 + scale) + shift


def swiglu(h, w_in, w_out):
    gate, up = jnp.split(h @ w_in, 2, axis=-1)
    return (jax.nn.silu(gate) * up) @ w_out


def axial_rope_tables(n_tokens, head_dim):
    t = jnp.arange(n_tokens, dtype=jnp.int32)
    row = (t // GRID_W).astype(jnp.float32)
    col = (t % GRID_W).astype(jnp.float32)
    n_freq = head_dim // 4
    inv = ROPE_THETA ** (-jnp.arange(n_freq, dtype=jnp.float32) / n_freq)
    ang = jnp.concatenate([row[:, None] * inv, col[:, None] * inv], axis=-1)
    return jnp.cos(ang), jnp.sin(ang)


def apply_rope(t, cos, sin):
    half = t.shape[-1] // 2
    shape = (1, cos.shape[0]) + (1,) * (t.ndim - 3) + (half,)
    cs, sn = cos.reshape(shape), sin.reshape(shape)
    t1 = t[..., :half].astype(jnp.float32)
    t2 = t[..., half:].astype(jnp.float32)
    return jnp.concatenate([t1 * cs - t2 * sn, t1 * sn + t2 * cs], axis=-1).astype(t.dtype)


def diff_attention(q, k, v, lam):
    s = jnp.einsum('bqhcd,bkhcd->bhcqk', q, k).astype(jnp.float32) * (A_DH ** -0.5)
    p = jax.nn.softmax(s, axis=-1)
    a = p[:, :, 0] - lam * p[:, :, 1]
    return jnp.einsum('bhqk,bkhe->bqhe', a.astype(v.dtype), v)


def diff_attention_latent(q, k, v, kc, vc, lam):
    Bn, N = q.shape[:2]
    nb = N // QBLOCK
    k_all = jnp.concatenate([kc, k], axis=1)
    v_all = jnp.concatenate([vc, v], axis=1)
    qb = jnp.moveaxis(q.reshape((Bn, nb, QBLOCK) + q.shape[2:]), 1, 0)
    ob = lax.map(lambda qi: diff_attention(qi, k_all, v_all, lam), qb)
    return jnp.moveaxis(ob, 0, 1).reshape((Bn, N) + v.shape[2:])


def spatial_gating(u, v, gn_g, gn_b, w_s, b_s):
    Bn, N = u.shape[:2]
    u = jax.nn.gelu(u)
    v = layer_norm(jax.nn.gelu(v), gn_g, gn_b)
    vch = v.reshape(Bn, N // CHUNK, CHUNK, B_GROUPS, B_GDIM)
    mixed = jnp.einsum('gpq,bnqgc->bnpgc', w_s, vch) + jnp.swapaxes(b_s, 0, 1)[None, None, :, :, None]
    return u * mixed.reshape(Bn, N, GROUP_WIDTH)


def sink_softmax(sink, s):
    grp = C_HEADS // C_KV_HEADS
    sb = jnp.broadcast_to(sink.astype(jnp.float32).reshape(C_KV_HEADS, grp, 1, 1), s.shape[:-1] + (1,))
    return jax.nn.softmax(jnp.concatenate([sb, s], axis=-1), axis=-1)[..., 1:]


def window_attention_latent(q, k, v, kc, vc, sink):
    Bn, N = q.shape[:2]
    L = kc.shape[1]
    nb = N // QBLOCK
    grp = C_HEADS // C_KV_HEADS
    qb = q.reshape(Bn, nb, QBLOCK, C_KV_HEADS, grp, C_DH)

    def band(t):
        tb = t.reshape(Bn, nb, QBLOCK, C_KV_HEADS, C_DH)
        tp = jnp.pad(tb, ((0, 0), (1, 1), (0, 0), (0, 0), (0, 0)))
        return jnp.concatenate([tp[:, :-2], tp[:, 1:-1], tp[:, 2:]], axis=2)

    kb, vb = band(k), band(v)
    scale = C_DH ** -0.5
    s_loc = jnp.einsum('bnqkgd,bnskd->bnkgqs', qb, kb).astype(jnp.float32) * scale
    s_ctx = jnp.einsum('bnqkgd,bskd->bnkgqs', qb, kc).astype(jnp.float32) * scale
    blk = jnp.arange(nb)[:, None, None]
    qpos = blk * QBLOCK + jnp.arange(QBLOCK)[None, :, None]
    kpos = (blk - 1) * QBLOCK + jnp.arange(3 * QBLOCK)[None, None, :]
    valid = (jnp.abs(kpos - qpos) <= WINDOW) & (kpos >= 0) & (kpos < N)
    s_loc = jnp.where(valid[None, :, None, None], s_loc, NEG_INF)
    p = sink_softmax(sink, jnp.concatenate([s_ctx, s_loc], axis=-1))
    p_ctx = p[..., :L].astype(v.dtype)
    p_loc = p[..., L:].astype(v.dtype)
    o = (jnp.einsum('bnkgqs,bskd->bnqkgd', p_ctx, vc)
         + jnp.einsum('bnkgqs,bnskd->bnqkgd', p_loc, vb))
    return o.reshape(Bn, N, C_HEADS * C_DH)


def sink_attention_ctx(qc, kc, vc, sink):
    Bn, L = qc.shape[:2]
    qg = qc.reshape(Bn, L, C_KV_HEADS, C_HEADS // C_KV_HEADS, C_DH)
    s = jnp.einsum('bqkgd,bskd->bkgqs', qg, kc).astype(jnp.float32) * (C_DH ** -0.5)
    p = sink_softmax(sink, s).astype(vc.dtype)
    return jnp.einsum('bkgqs,bskd->bqkgd', p, vc).reshape(Bn, L, C_HEADS * C_DH)


def neighbourhood_attention_latent(q, k, v, kc, vc, rpb):
    Bn, N = q.shape[:2]
    L = kc.shape[1]
    R = N // GRID_W
    kr = min(NA_ROWS, R)
    qg = q.reshape(Bn, R, GRID_W, D_HEADS, D_DH)
    kg = k.reshape(Bn, R, GRID_W, D_HEADS, D_DH)
    vg = v.reshape(Bn, R, GRID_W, D_HEADS, D_DH)
    r = jnp.arange(R)
    rs = jnp.clip(r - kr // 2, 0, R - kr)
    rows = rs[:, None] + jnp.arange(kr)[None, :]
    kn = kg[:, rows]
    vn = vg[:, rows]
    scale = D_DH ** -0.5
    s_nb = jnp.einsum('brqhd,brjkhd->brhqjk', qg, kn).astype(jnp.float32) * scale
    cq = jnp.arange(GRID_W)
    cs = jnp.clip(cq - NA_COLS // 2, 0, GRID_W - NA_COLS)
    col_ok = (cq[None, :] >= cs[:, None]) & (cq[None, :] < cs[:, None] + NA_COLS)
    dr = rows - r[:, None] + (NA_ROWS - 1)
    dc = jnp.clip(cq[None, :] - cq[:, None], -(NA_COLS - 1), NA_COLS - 1) + (NA_COLS - 1)
    bias = rpb[:, dr[:, None, :, None], dc[None, :, None, :]]
    bias = jnp.moveaxis(bias, 0, 1).astype(jnp.float32)
    s_nb = jnp.where(col_ok[:, None, :], s_nb + bias[None], NEG_INF)
    s_nb = s_nb.reshape(Bn, R, D_HEADS, GRID_W, kr * GRID_W)
    s_ctx = jnp.einsum('brqhd,bshd->brhqs', qg, kc).astype(jnp.float32) * scale
    p = jax.nn.softmax(jnp.concatenate([s_ctx, s_nb], axis=-1), axis=-1)
    p_ctx = p[..., :L].astype(v.dtype)
    p_nb = p[..., L:].reshape(Bn, R, D_HEADS, GRID_W, kr, GRID_W).astype(v.dtype)
    o = (jnp.einsum('brhqs,bshd->brqhd', p_ctx, vc)
         + jnp.einsum('brhqjk,brjkhd->brqhd', p_nb, vn))
    return o.reshape(Bn, N, D_HEADS * D_DH)


def ctx_attention(qc, kc, vc):
    Bn, L = qc.shape[:2]
    s = jnp.einsum('bqhd,bshd->bhqs', qc, kc).astype(jnp.float32) * (D_DH ** -0.5)
    p = jax.nn.softmax(s, axis=-1).astype(vc.dtype)
    return jnp.einsum('bhqs,bshd->bqhd', p, vc).reshape(Bn, L, D_HEADS * D_DH)


def token_mixer(h, hc, w_in, w_out, lam_vecs, lam_init, gn_g, gn_b, w_s, b_s, sink, rpb, cos, sin, need_ctx):
    Bn, N, _ = h.shape
    L = hc.shape[1]
    cuts = np.cumsum(PROJ_SPLITS)[:-1].tolist()
    aq, ak, av, bu, bv, cq, ck, cv, dq, dk, dv = jnp.split(h @ w_in, cuts, axis=-1)
    aqc, akc, avc, buc, bvc, cqc, ckc, cvc, dqc, dkc, dvc = jnp.split(hc @ w_in, cuts, axis=-1)

    lv = lam_vecs.astype(jnp.float32)
    lam = jnp.exp(jnp.sum(lv[0] * lv[1])) - jnp.exp(jnp.sum(lv[2] * lv[3])) + lam_init
    qa = apply_rope(aq.reshape(Bn, N, A_HEADS, 2, A_DH), cos, sin)
    ka = apply_rope(ak.reshape(Bn, N, A_HEADS, 2, A_DH), cos, sin)
    va = av.reshape(Bn, N, A_HEADS, 2 * A_DH)
    kac = akc.reshape(Bn, L, A_HEADS, 2, A_DH)
    vac = avc.reshape(Bn, L, A_HEADS, 2 * A_DH)
    ya = diff_attention_latent(qa, ka, va, kac, vac, lam)
    ya = (rms_norm(ya) * (1.0 - lam_init)).reshape(Bn, N, GROUP_WIDTH)

    yb = spatial_gating(bu, bv, gn_g, gn_b, w_s, b_s)

    qcq = apply_rope(cq.reshape(Bn, N, C_HEADS, C_DH), cos, sin)
    kcq = apply_rope(ck.reshape(Bn, N, C_KV_HEADS, C_DH), cos, sin)
    vcq = cv.reshape(Bn, N, C_KV_HEADS, C_DH)
    kcc = ckc.reshape(Bn, L, C_KV_HEADS, C_DH)
    vcc = cvc.reshape(Bn, L, C_KV_HEADS, C_DH)
    yc = window_attention_latent(qcq, kcq, vcq, kcc, vcc, sink)

    kdc = dkc.reshape(Bn, L, D_HEADS, D_DH)
    vdc = dvc.reshape(Bn, L, D_HEADS, D_DH)
    yd = neighbourhood_attention_latent(dq.reshape(Bn, N, D_HEADS, D_DH), dk.reshape(Bn, N, D_HEADS, D_DH),
                                        dv.reshape(Bn, N, D_HEADS, D_DH), kdc, vdc, rpb)

    y = jnp.concatenate([ya, yb, yc, yd], axis=-1) @ w_out
    if not need_ctx:
        return y, None

    ya_c = diff_attention(aqc.reshape(Bn, L, A_HEADS, 2, A_DH), kac, vac, lam)
    ya_c = (rms_norm(ya_c) * (1.0 - lam_init)).reshape(Bn, L, GROUP_WIDTH)
    yb_c = spatial_gating(buc, bvc, gn_g, gn_b, w_s, b_s)
    yc_c = sink_attention_ctx(cqc.reshape(Bn, L, C_HEADS, C_DH), kcc, vcc, sink)
    yd_c = ctx_attention(dqc.reshape(Bn, L, D_HEADS, D_DH), kdc, vdc)
    y_ctx = jnp.concatenate([ya_c, yb_c, yc_c, yd_c], axis=-1) @ w_out
    return y, y_ctx


def setup_inputs(seed: int = 0) -> dict:
    key = jax.random.key(seed)
    ks = jax.random.split(key, 24)

    def nrm(k, shape, scale):
        return jax.random.normal(k, shape, jnp.float32) * scale

    return {
        'x': nrm(ks[0], (BATCH, SEQ, D_MODEL), 1.0),
        'c': nrm(ks[1], (BATCH, D_MODEL), 1.0),
        'ctx': nrm(ks[2], (BATCH, CTX_LEN, D_MODEL), 1.0),
        'c_ctx': nrm(ks[3], (D_MODEL,), 1.0),
        'w_mod': nrm(ks[4], (DEPTH, D_MODEL, N_MOD * D_MODEL), D_MODEL ** -0.5),
        'b_mod': nrm(ks[5], (DEPTH, N_MOD * D_MODEL), 0.02),
        'ln_g': 1.0 + nrm(ks[6], (DEPTH, 3, D_MODEL), 0.02),
        'ln_b': nrm(ks[7], (DEPTH, 3, D_MODEL), 0.02),
        'ffn1_w_in': nrm(ks[8], (DEPTH, D_MODEL, 2 * D_FF), D_MODEL ** -0.5),
        'ffn1_w_out': nrm(ks[9], (DEPTH, D_FF, D_MODEL), D_FF ** -0.5 * DEEPNORM_BETA),
        'ffn2_w_in': nrm(ks[10], (DEPTH, D_MODEL, 2 * D_FF), D_MODEL ** -0.5),
        'ffn2_w_out': nrm(ks[11], (DEPTH, D_FF, D_MODEL), D_FF ** -0.5 * DEEPNORM_BETA),
        'mix_w_in': nrm(ks[12], (DEPTH, D_MODEL, D_PROJ), D_MODEL ** -0.5),
        'mix_w_out': nrm(ks[13], (DEPTH, D_MIX, D_MODEL), D_MIX ** -0.5 * DEEPNORM_BETA),
        'a_lambda': nrm(ks[14], (DEPTH, 4, A_DH), 0.1),
        'b_norm_g': 1.0 + nrm(ks[15], (DEPTH, GROUP_WIDTH), 0.02),
        'b_norm_b': nrm(ks[16], (DEPTH, GROUP_WIDTH), 0.02),
        'b_spatial_w': nrm(ks[17], (DEPTH, B_GROUPS, CHUNK, CHUNK), CHUNK ** -0.5),
        'b_spatial_b': 1.0 + nrm(ks[18], (DEPTH, B_GROUPS, CHUNK), 0.02),
        'c_sink': nrm(ks[19], (DEPTH, C_HEADS), 0.5),
        'd_rpb': nrm(ks[20], (DEPTH, D_HEADS, 2 * NA_ROWS - 1, 2 * NA_COLS - 1), 0.1),
    }


def reference(x, c, ctx, c_ctx, w_mod, b_mod, ln_g, ln_b, ffn1_w_in, ffn1_w_out, ffn2_w_in, ffn2_w_out,
              mix_w_in, mix_w_out, a_lambda, b_norm_g, b_norm_b, b_spatial_w, b_spatial_b, c_sink, d_rpb):
    N = x.shape[1]
    cos, sin = axial_rope_tables(N, HEAD_DIM)
    xc = ctx
    for l in range(DEPTH):
        last = l == DEPTH - 1
        lam_init = 0.8 - 0.6 * math.exp(-0.3 * l)
        mx = jnp.split((jax.nn.silu(c) @ w_mod[l] + b_mod[l])[:, None, :], N_MOD, axis=-1)
        mc = jnp.split((jax.nn.silu(c_ctx) @ w_mod[l] + b_mod[l])[None, None, :], N_MOD, axis=-1)

        x = layer_norm(DEEPNORM_ALPHA * x + 0.5 * mx[2] * swiglu(modulate(x, mx[0], mx[1]), ffn1_w_in[l], ffn1_w_out[l]),
                       ln_g[l, 0], ln_b[l, 0])
        xc = layer_norm(DEEPNORM_ALPHA * xc + 0.5 * mc[2] * swiglu(modulate(xc, mc[0], mc[1]), ffn1_w_in[l], ffn1_w_out[l]),
                        ln_g[l, 0], ln_b[l, 0])

        y, y_ctx = token_mixer(modulate(x, mx[3], mx[4]), modulate(xc, mc[3], mc[4]), mix_w_in[l], mix_w_out[l],
                               a_lambda[l], lam_init, b_norm_g[l], b_norm_b[l], b_spatial_w[l], b_spatial_b[l],
                               c_sink[l], d_rpb[l], cos, sin, not last)
        x = layer_norm(DEEPNORM_ALPHA * x + mx[5] * y, ln_g[l, 1], ln_b[l, 1])

        x = layer_norm(DEEPNORM_ALPHA * x + 0.5 * mx[8] * swiglu(modulate(x, mx[6], mx[7]), ffn2_w_in[l], ffn2_w_out[l]),
                       ln_g[l, 2], ln_b[l, 2])
        if not last:
            xc = layer_norm(DEEPNORM_ALPHA * xc + mc[5] * y_ctx, ln_g[l, 1], ln_b[l, 1])
            xc = layer_norm(DEEPNORM_ALPHA * xc + 0.5 * mc[8] * swiglu(modulate(xc, mc[6], mc[7]), ffn2_w_in[l], ffn2_w_out[l]),
                            ln_g[l, 2], ln_b[l, 2])
    return x
```

```python
import functools
import math

import jax
import jax.numpy as jnp
from jax import lax
from jax.experimental import pallas as pl
from jax.experimental.pallas import tpu as pltpu

F32 = jnp.float32
BF16 = jnp.bfloat16

D_MODEL = 2048
N_GROUPS = 4
GROUP_WIDTH = D_MODEL // N_GROUPS
HEAD_DIM = 64
GRID_W = 64
CHUNK = 128
B_GROUPS = GROUP_WIDTH // 128
A_HEADS = GROUP_WIDTH // (2 * HEAD_DIM)
C_HEADS = GROUP_WIDTH // HEAD_DIM
C_KV_HEADS = C_HEADS // 4
D_HEADS = GROUP_WIDTH // HEAD_DIM
WINDOW = 128
QBLOCK = 128
NA_ROWS = 8
NA_COLS = 16
D_FF = 256 * math.ceil(8 * D_MODEL / 3 / 256)
N_MOD = 9
ROPE_THETA = 10000.0
LN_EPS = 1e-6
NEG_INF = -1e30
MODEL_DEPTH = 4
DEEPNORM_ALPHA = (2 * MODEL_DEPTH) ** 0.25
QK_SCALE = HEAD_DIM ** -0.5

LANES = 128
TM = 512
TF = 512
SEG = 512
N_SEG = 10
VMEM_LIMIT = 56 * 1024 * 1024


def _params(sem):
    return pltpu.CompilerParams(dimension_semantics=sem, vmem_limit_bytes=VMEM_LIMIT)


def _ln(t, g, b):
    mu = jnp.mean(t, axis=-1, keepdims=True)
    tc = t - mu
    var = jnp.mean(tc * tc, axis=-1, keepdims=True)
    return tc * lax.rsqrt(var + LN_EPS) * g + b


def _dot_nt(a, b):
    return lax.dot_general(a, b, (((1,), (1,)), ((), ())), preferred_element_type=F32)


def _half_masks(shape):
    lane = lax.broadcasted_iota(jnp.int32, shape, 1)
    return lane < HEAD_DIM, lane >= HEAD_DIM


def _mod_kernel(cc_ref, w_ref, b_ref, o_ref):
    a = cc_ref[...]
    s = (a * jax.nn.sigmoid(a)).astype(BF16)
    o_ref[0] = jnp.dot(s, w_ref[0].astype(BF16), preferred_element_type=F32) + b_ref[0]


def _mod_vectors(cc, w_mod, b_mod):
    depth, d, nm = w_mod.shape
    tn = 1024
    return pl.pallas_call(
        _mod_kernel,
        grid=(depth, nm // tn),
        in_specs=[
            pl.BlockSpec((8, d), lambda l, j: (0, 0)),
            pl.BlockSpec((1, d, tn), lambda l, j: (l, 0, j)),
            pl.BlockSpec((1, 1, tn), lambda l, j: (l, 0, j)),
        ],
        out_specs=pl.BlockSpec((1, 8, tn), lambda l, j: (l, 0, j)),
        out_shape=jax.ShapeDtypeStruct((depth, 8, nm), F32),
        compiler_params=_params(("parallel", "parallel")),
        name="mod_vectors",
    )(cc, w_mod, b_mod.reshape(depth, 1, nm))


def _ffn_kernel(x_ref, sh_ref, sc_ref, gt_ref, lng_ref, lnb_ref, wg_ref, wu_ref, wo_ref, *rest, emit_h):
    if emit_h:
        sh2_ref, sc2_ref, o_ref, h_ref, xin_scr, acc_scr = rest
    else:
        o_ref, xin_scr, acc_scr = rest
    j = pl.program_id(1)

    @pl.when(j == 0)
    def _():
        xin_scr[...] = (x_ref[...] * (1.0 + sc_ref[0]) + sh_ref[0]).astype(BF16)
        acc_scr[...] = jnp.zeros_like(acc_scr)

    xin = xin_scr[...]
    g = jnp.dot(xin, wg_ref[...], preferred_element_type=F32)
    u = jnp.dot(xin, wu_ref[...], preferred_element_type=F32)
    hh = ((g * jax.nn.sigmoid(g)) * u).astype(BF16)
    acc_scr[...] += jnp.dot(hh, wo_ref[...], preferred_element_type=F32)

    @pl.when(j == pl.num_programs(1) - 1)
    def _():
        t = DEEPNORM_ALPHA * x_ref[...] + (0.5 * gt_ref[0]) * acc_scr[...]
        y = _ln(t, lng_ref[...], lnb_ref[...])
        o_ref[...] = y
        if emit_h:
            h_ref[...] = (y * (1.0 + sc2_ref[0]) + sh2_ref[0]).astype(BF16)


def _ffn(x, mods_l, ks, lng, lnb, w_in, w_out, n_tiles, nlt, emit_ks=None):
    d = D_MODEL
    nj = D_FF // TF
    emit_h = emit_ks is not None

    def mod_spec(k):
        return pl.BlockSpec((1, 1, d), lambda i, j, k=k: (jnp.where(i >= nlt, N_MOD, 0) + k, 0, 0))

    vec_spec = pl.BlockSpec((1, d), lambda i, j: (0, 0))
    in_specs = [
        pl.BlockSpec((TM, d), lambda i, j: (i, 0)),
        mod_spec(ks[0]), mod_spec(ks[1]), mod_spec(ks[2]),
        vec_spec, vec_spec,
        pl.BlockSpec((d, TF), lambda i, j: (0, j)),
        pl.BlockSpec((d, TF), lambda i, j: (0, nj + j)),
        pl.BlockSpec((TF, d), lambda i, j: (j, 0)),
    ]
    args = [x, mods_l, mods_l, mods_l, lng, lnb, w_in, w_in, w_out]
    out_specs = pl.BlockSpec((TM, d), lambda i, j: (i, 0))
    out_shape = jax.ShapeDtypeStruct((n_tiles * TM, d), F32)
    if emit_h:
        in_specs += [mod_spec(emit_ks[0]), mod_spec(emit_ks[1])]
        args += [mods_l, mods_l]
        out_specs = [out_specs, pl.BlockSpec((TM, d), lambda i, j: (i, 0))]
        out_shape = [out_shape, jax.ShapeDtypeStruct((n_tiles * TM, d), BF16)]
    return pl.pallas_call(
        functools.partial(_ffn_kernel, emit_h=emit_h),
        grid=(n_tiles, nj),
        in_specs=in_specs,
        out_specs=out_specs,
        out_shape=out_shape,
        scratch_shapes=[pltpu.VMEM((TM, d), BF16), pltpu.VMEM((TM, d), F32)],
        compiler_params=_params(("parallel", "arbitrary")),
        name="ffn_emit" if emit_h else "ffn",
    )(*args)


def _rope(t, cos_ref, sin_ref):
    w = t.shape[1]
    lane = lax.broadcasted_iota(jnp.int32, t.shape, 1)
    first = (lane % HEAD_DIM) < (HEAD_DIM // 2)
    rot = jnp.where(first, pltpu.roll(t, w - HEAD_DIM // 2, 1), pltpu.roll(t, HEAD_DIM // 2, 1))
    reps = w // LANES
    cos = jnp.concatenate([cos_ref[...]] * reps, axis=1) if reps > 1 else cos_ref[...]
    sin = jnp.concatenate([sin_ref[...]] * reps, axis=1) if reps > 1 else sin_ref[...]
    return t * cos + rot * sin


_SEG_AQ, _SEG_AK, _SEG_AV, _SEG_BU, _SEG_BV, _SEG_CQ, _SEG_CKV, _SEG_DQ, _SEG_DK, _SEG_DV = range(N_SEG)


def _inproj_kernel(h_ref, w_ref, cos_ref, sin_ref, *out_refs):
    j = pl.program_id(1)
    for k, o_ref in enumerate(out_refs):

        @pl.when(j == k)
        def _(k=k, o_ref=o_ref):
            t = jnp.dot(h_ref[...], w_ref[...], preferred_element_type=F32)
            if k in (_SEG_AQ, _SEG_AK, _SEG_CQ):
                t = _rope(t, cos_ref, sin_ref)
            if k in (_SEG_AQ, _SEG_CQ, _SEG_DQ):
                t = t * QK_SCALE
            if k == _SEG_CKV:
                t = jnp.concatenate([_rope(t[:, :LANES], cos_ref, sin_ref), t[:, LANES:2 * LANES]], axis=1)
            o_ref[...] = t.astype(o_ref.dtype)


def _inproj(h, w_perm, cos_t, sin_t):
    rows = h.shape[0]
    n_tiles = rows // TM
    d = D_MODEL
    widths = [SEG] * N_SEG
    widths[_SEG_CKV] = 2 * LANES
    dtypes = [BF16] * N_SEG
    dtypes[_SEG_BU] = F32
    dtypes[_SEG_BV] = F32
    return pl.pallas_call(
        _inproj_kernel,
        grid=(n_tiles, N_SEG),
        in_specs=[
            pl.BlockSpec((TM, d), lambda i, j: (i, 0)),
            pl.BlockSpec((d, SEG), lambda i, j: (0, j)),
            pl.BlockSpec((TM, LANES), lambda i, j: (i, 0)),
            pl.BlockSpec((TM, LANES), lambda i, j: (i, 0)),
        ],
        out_specs=[pl.BlockSpec((TM, w), lambda i, j: (i, 0)) for w in widths],
        out_shape=[jax.ShapeDtypeStruct((rows, w), dt) for w, dt in zip(widths, dtypes)],
        compiler_params=_params(("parallel", "arbitrary")),
        name="inproj",
    )(h, w_perm, cos_t, sin_t)


def _attn_a_kernel(lam_ref, q_ref, k_ref, v_ref, *rest, n_chunks, tk, tail_start, tail_len, lam_init, aliased):
    if aliased:
        _, o_ref, m_scr, l_scr, acc_scr = rest
    else:
        o_ref, m_scr, l_scr, acc_scr = rest
    q = q_ref[...]
    first, second = _half_masks(q.shape)
    zero = jnp.zeros_like(q)
    qs = (jnp.where(first, q, zero), jnp.where(second, q, zero))
    m_scr[...] = jnp.full(m_scr.shape, NEG_INF, F32)
    l_scr[...] = jnp.zeros_like(l_scr)
    acc_scr[...] = jnp.zeros_like(acc_scr)

    def process(start, size):
        k = k_ref[pl.ds(start, size), :]
        v = v_ref[pl.ds(start, size), :]
        for c in range(2):
            s = _dot_nt(qs[c], k)
            m_old = m_scr[c]
            m_new = jnp.maximum(m_old, jnp.max(s, axis=1, keepdims=True))
            alpha = jnp.exp(m_old - m_new)
            p = jnp.exp(s - m_new)
            l_scr[c] = alpha * l_scr[c] + jnp.sum(p, axis=1, keepdims=True)
            acc_scr[c] = alpha * acc_scr[c] + jnp.dot(p.astype(BF16), v, preferred_element_type=F32)
            m_scr[c] = m_new

    if n_chunks:
        def body(i, carry):
            process(pl.multiple_of(i * tk, tk), tk)
            return carry
        lax.fori_loop(0, n_chunks, body, 0)
    process(tail_start, tail_len)

    lv = lam_ref[...]
    lam = (jnp.exp(jnp.sum(lv[0:1] * lv[1:2], axis=1, keepdims=True))
           - jnp.exp(jnp.sum(lv[2:3] * lv[3:4], axis=1, keepdims=True)) + lam_init)
    o = acc_scr[0] / l_scr[0] - lam * (acc_scr[1] / l_scr[1])
    ms = jnp.mean(o * o, axis=-1, keepdims=True)
    o_ref[...] = ((o * lax.rsqrt(ms + LN_EPS)) * (1.0 - lam_init)).astype(BF16)


def _attn_a(lam_vecs, aq, ak, av, n, l_ctx, lam_init, need_ctx):
    rows = aq.shape[0]
    tq = 256
    tk = 1024
    scratch = lambda t: [pltpu.VMEM((2, t, 1), F32), pltpu.VMEM((2, t, 1), F32), pltpu.VMEM((2, t, LANES), F32)]
    lam_spec = pl.BlockSpec((4, HEAD_DIM), lambda h, i: (0, 0))
    ya = pl.pallas_call(
        functools.partial(_attn_a_kernel, n_chunks=n // tk, tk=tk, tail_start=n, tail_len=l_ctx,
                          lam_init=lam_init, aliased=False),
        grid=(A_HEADS, n // tq),
        in_specs=[
            lam_spec,
            pl.BlockSpec((tq, LANES), lambda h, i: (i, h)),
            pl.BlockSpec((rows, LANES), lambda h, i: (0, h)),
            pl.BlockSpec((rows, LANES), lambda h, i: (0, h)),
        ],
        out_specs=pl.BlockSpec((tq, LANES), lambda h, i: (i, h)),
        out_shape=jax.ShapeDtypeStruct((rows, GROUP_WIDTH), BF16),
        scratch_shapes=scratch(tq),
        compiler_params=_params(("parallel", "parallel")),
        name="attn_a",
    )(lam_vecs, aq, ak, av)
    if not need_ctx:
        return ya
    ct = n // TM
    cb = n // l_ctx
    return pl.pallas_call(
        functools.partial(_attn_a_kernel, n_chunks=0, tk=tk, tail_start=0, tail_len=l_ctx,
                          lam_init=lam_init, aliased=True),
        grid=(A_HEADS, 1),
        in_specs=[
            lam_spec,
            pl.BlockSpec((TM, LANES), lambda h, i: (ct, h)),
            pl.BlockSpec((l_ctx, LANES), lambda h, i: (cb, h)),
            pl.BlockSpec((l_ctx, LANES), lambda h, i: (cb, h)),
            pl.BlockSpec(memory_space=pl.ANY),
        ],
        out_specs=pl.BlockSpec((TM, LANES), lambda h, i: (ct, h)),
        out_shape=jax.ShapeDtypeStruct((rows, GROUP_WIDTH), BF16),
        scratch_shapes=scratch(TM),
        input_output_aliases={4: 0},
        compiler_params=_params(("parallel", "parallel")),
        name="attn_a_ctx",
    )(lam_vecs, aq, ak, av, ya)


def _mixb_kernel(u_ref, v_ref, g_ref, b_ref, ws_ref, bs_ref, o_ref):
    u = jax.nn.gelu(u_ref[...])
    v = _ln(jax.nn.gelu(v_ref[...]), g_ref[...], b_ref[...]).astype(BF16)
    for c in range(u.shape[0] // CHUNK):
        rs = slice(c * CHUNK, (c + 1) * CHUNK)
        for g in range(B_GROUPS):
            cs = slice(g * LANES, (g + 1) * LANES)
            mixed = jnp.dot(ws_ref[g].astype(BF16), v[rs, cs], preferred_element_type=F32) + bs_ref[:, g:g + 1]
            o_ref[rs, cs] = (u[rs, cs] * mixed).astype(BF16)


def _mixb(bu, bv, gn_g, gn_b, w_s, b_s_t):
    rows = bu.shape[0]
    gw = GROUP_WIDTH
    tile = pl.BlockSpec((TM, gw), lambda i: (i, 0))
    return pl.pallas_call(
        _mixb_kernel,
        grid=(rows // TM,),
        in_specs=[
            tile, tile,
            pl.BlockSpec((1, gw), lambda i: (0, 0)),
            pl.BlockSpec((1, gw), lambda i: (0, 0)),
            pl.BlockSpec((B_GROUPS, CHUNK, CHUNK), lambda i: (0, 0, 0)),
            pl.BlockSpec((CHUNK, B_GROUPS), lambda i: (0, 0)),
        ],
        out_specs=tile,
        out_shape=jax.ShapeDtypeStruct((rows, gw), BF16),
        compiler_params=_params(("parallel",)),
        name="mix_b",
    )(bu, bv, gn_g, gn_b, w_s, b_s_t)


def _softmax_pv(scores, values, sink=None):
    m = jnp.max(scores[0], axis=1, keepdims=True)
    for s in scores[1:]:
        m = jnp.maximum(m, jnp.max(s, axis=1, keepdims=True))
    if sink is not None:
        m = jnp.maximum(m, sink)
    den = None
    num = None
    for s, v in zip(scores, values):
        p = jnp.exp(s - m)
        ps = jnp.sum(p, axis=1, keepdims=True)
        den = ps if den is None else den + ps
        pv = jnp.dot(p.astype(BF16), v, preferred_element_type=F32)
        num = pv if num is None else num + pv
    if sink is not None:
        den = den + jnp.exp(sink - m)
    return num / den


def _attn_c_kernel(sink_ref, q_ref, k_ref, v_ref, *rest, n, l_ctx, local, aliased):
    o_ref = rest[-1]
    i = pl.program_id(0)
    n_sub = q_ref.shape[0] // QBLOCK
    wlen = 3 * QBLOCK
    ctx_start = n if local else 0
    kc = k_ref[pl.ds(ctx_start, l_ctx), :]
    vc = v_ref[pl.ds(ctx_start, l_ctx), :]
    kfirst, ksecond = _half_masks(vc.shape)
    vcs = (jnp.where(kfirst, vc, jnp.zeros_like(vc)), jnp.where(ksecond, vc, jnp.zeros_like(vc)))

    def sub_block(b, carry):
        r0 = pl.multiple_of(b * QBLOCK, QBLOCK)
        if local:
            blk = i * n_sub + b
            start = pl.multiple_of(jnp.clip((blk - 1) * QBLOCK, 0, n - wlen), QBLOCK)
            kl = k_ref[pl.ds(start, wlen), :]
            vl = v_ref[pl.ds(start, wlen), :]
            lfirst, lsecond = _half_masks(vl.shape)
            vls = (jnp.where(lfirst, vl, jnp.zeros_like(vl)), jnp.where(lsecond, vl, jnp.zeros_like(vl)))
            qpos = blk * QBLOCK + lax.broadcasted_iota(jnp.int32, (QBLOCK, wlen), 0)
            kpos = start + lax.broadcasted_iota(jnp.int32, (QBLOCK, wlen), 1)
            valid = jnp.abs(kpos - qpos) <= WINDOW
        for c in range(C_HEADS // 2):
            qb = q_ref[pl.ds(r0, QBLOCK), c * LANES:(c + 1) * LANES]
            qfirst, qsecond = _half_masks(qb.shape)
            out = None
            for half in range(2):
                qm = jnp.where(qfirst if half == 0 else qsecond, qb, jnp.zeros_like(qb))
                sink = sink_ref[half * (C_HEADS // 2) + c]
                scores = [_dot_nt(qm, kc)]
                values = [vcs[half]]
                if local:
                    scores.append(jnp.where(valid, _dot_nt(qm, kl), NEG_INF))
                    values.append(vls[half])
                o = _softmax_pv(scores, values, sink)
                out = o if out is None else out + o
            o_ref[pl.ds(r0, QBLOCK), c * LANES:(c + 1) * LANES] = out.astype(BF16)
        return carry

    lax.fori_loop(0, n_sub, sub_block, 0)


def _attn_c(sink, cq, ckv, n, l_ctx, need_ctx):
    rows = cq.shape[0]
    gw = GROUP_WIDTH
    smem = pl.BlockSpec(memory_space=pltpu.SMEM)
    yc = pl.pallas_call(
        functools.partial(_attn_c_kernel, n=n, l_ctx=l_ctx, local=True, aliased=False),
        grid=(n // TM,),
        in_specs=[
            smem,
            pl.BlockSpec((TM, gw), lambda i: (i, 0)),
            pl.BlockSpec((rows, LANES), lambda i: (0, 0)),
            pl.BlockSpec((rows, LANES), lambda i: (0, 1)),
        ],
        out_specs=pl.BlockSpec((TM, gw), lambda i: (i, 0)),
        out_shape=jax.ShapeDtypeStruct((rows, gw), BF16),
        compiler_params=_params(("parallel",)),
        name="attn_c",
    )(sink, cq, ckv, ckv)
    if not need_ctx:
        return yc
    ct = n // TM
    cb = n // l_ctx
    return pl.pallas_call(
        functools.partial(_attn_c_kernel, n=n, l_ctx=l_ctx, local=False, aliased=True),
        grid=(1,),
        in_specs=[
            smem,
            pl.BlockSpec((TM, gw), lambda i: (ct, 0)),
            pl.BlockSpec((l_ctx, LANES), lambda i: (cb, 0)),
            pl.BlockSpec((l_ctx, LANES), lambda i: (cb, 1)),
            pl.BlockSpec(memory_space=pl.ANY),
        ],
        out_specs=pl.BlockSpec((TM, gw), lambda i: (ct, 0)),
        out_shape=jax.ShapeDtypeStruct((rows, gw), BF16),
        input_output_aliases={4: 0},
        compiler_params=_params(("parallel",)),
        name="attn_c_ctx",
    )(sink, cq, ckv, ckv, yc)


def _attn_d_kernel(q_ref, kc_ref, vc_ref, *rest, local, aliased):
    o_ref = rest[-1]
    if local:
        k_ref, v_ref, b_ref = rest[0], rest[1], rest[2]
    for c in range(D_HEADS // 2):
        cs = slice(c * LANES, (c + 1) * LANES)
        qb = q_ref[:, cs]
        qfirst, qsecond = _half_masks(qb.shape)
        kcb = kc_ref[:, cs]
        vcb = vc_ref[:, cs]
        cfirst, csecond = _half_masks(vcb.shape)
        if local:
            kb = k_ref[:, cs]
            vb = v_ref[:, cs]
            nfirst, nsecond = _half_masks(vb.shape)
        out = None
        for half in range(2):
            qm = jnp.where(qfirst if half == 0 else qsecond, qb, jnp.zeros_like(qb))
            scores = [_dot_nt(qm, kcb)]
            values = [jnp.where(cfirst if half == 0 else csecond, vcb, jnp.zeros_like(vcb))]
            if local:
                scores.append(_dot_nt(qm, kb) + b_ref[0, 2 * c + half])
                values.append(jnp.where(nfirst if half == 0 else nsecond, vb, jnp.zeros_like(vb)))
            o = _softmax_pv(scores, values)
            out = o if out is None else out + o
        o_ref[:, cs] = out.astype(BF16)


def _attn_d(dq, dk, dv, bias, n, l_ctx, need_ctx):
    rows = dq.shape[0]
    gw = GROUP_WIDTH
    n_rows = n // GRID_W
    nk = NA_ROWS * GRID_W
    cb = n // l_ctx

    def row_start(r):
        return jnp.clip(r - NA_ROWS // 2, 0, n_rows - NA_ROWS)

    yd = pl.pallas_call(
        functools.partial(_attn_d_kernel, local=True, aliased=False),
        grid=(n_rows,),
        in_specs=[
            pl.BlockSpec((GRID_W, gw), lambda r: (r, 0)),
            pl.BlockSpec((l_ctx, gw), lambda r: (cb, 0)),
            pl.BlockSpec((l_ctx, gw), lambda r: (cb, 0)),
            pl.BlockSpec((pl.Element(nk), pl.Element(gw)), lambda r: (row_start(r) * GRID_W, 0)),
            pl.BlockSpec((pl.Element(nk), pl.Element(gw)), lambda r: (row_start(r) * GRID_W, 0)),
            pl.BlockSpec((1, D_HEADS, GRID_W, nk), lambda r: (r - row_start(r), 0, 0, 0)),
        ],
        out_specs=pl.BlockSpec((GRID_W, gw), lambda r: (r, 0)),
        out_shape=jax.ShapeDtypeStruct((rows, gw), BF16),
        compiler_params=_params(("parallel",)),
        name="attn_d",
    )(dq, dk, dv, dk, dv, bias)
    if not need_ctx:
        return yd
    ct = n // TM
    return pl.pallas_call(
        functools.partial(_attn_d_kernel, local=False, aliased=True),
        grid=(1,),
        in_specs=[
            pl.BlockSpec((TM, gw), lambda i: (ct, 0)),
            pl.BlockSpec((l_ctx, gw), lambda i: (cb, 0)),
            pl.BlockSpec((l_ctx, gw), lambda i: (cb, 0)),
            pl.BlockSpec(memory_space=pl.ANY),
        ],
        out_specs=pl.BlockSpec((TM, gw), lambda i: (ct, 0)),
        out_shape=jax.ShapeDtypeStruct((rows, gw), BF16),
        input_output_aliases={3: 0},
        compiler_params=_params(("parallel",)),
        name="attn_d_ctx",
    )(dq, dk, dv, yd)


def _outproj_kernel(ya_ref, yb_ref, yc_ref, yd_ref, w_ref, x_ref, gt_ref, lng_ref, lnb_ref, o_ref):
    y_in = jnp.concatenate([ya_ref[...], yb_ref[...], yc_ref[...], yd_ref[...]], axis=1)
    y = jnp.dot(y_in, w_ref[...], preferred_element_type=F32)
    t = DEEPNORM_ALPHA * x_ref[...] + gt_ref[0] * y
    o_ref[...] = _ln(t, lng_ref[...], lnb_ref[...])


def _outproj(ya, yb, yc, yd, w_out, x, mods_l, gate_k, lng, lnb, n_tiles, nlt):
    d = D_MODEL
    gw = GROUP_WIDTH
    ytile = pl.BlockSpec((TM, gw), lambda i: (i, 0))
    vec = pl.BlockSpec((1, d), lambda i: (0, 0))
    return pl.pallas_call(
        _outproj_kernel,
        grid=(n_tiles,),
        in_specs=[
            ytile, ytile, ytile, ytile,
            pl.BlockSpec((d, d), lambda i: (0, 0)),
            pl.BlockSpec((TM, d), lambda i: (i, 0)),
            pl.BlockSpec((1, 1, d), lambda i: (jnp.where(i >= nlt, N_MOD, 0) + gate_k, 0, 0)),
            vec, vec,
        ],
        out_specs=pl.BlockSpec((TM, d), lambda i: (i, 0)),
        out_shape=jax.ShapeDtypeStruct((n_tiles * TM, d), F32),
        compiler_params=_params(("parallel",)),
        name="outproj",
    )(ya, yb, yc, yd, w_out, x, mods_l, lng, lnb)


def _rope_tables(n, rows):
    t = jnp.arange(n, dtype=jnp.int32)
    row = (t // GRID_W).astype(F32)
    col = (t % GRID_W).astype(F32)
    n_freq = HEAD_DIM // 4
    inv = ROPE_THETA ** (-jnp.arange(n_freq, dtype=F32) / n_freq)
    ang = jnp.concatenate([row[:, None] * inv, col[:, None] * inv], axis=-1)
    cos, sin = jnp.cos(ang), jnp.sin(ang)
    cos_t = jnp.tile(jnp.concatenate([cos, cos], axis=-1), (1, LANES // HEAD_DIM))
    sin_t = jnp.tile(jnp.concatenate([-sin, sin], axis=-1), (1, LANES // HEAD_DIM))
    pad = rows - n
    cos_t = jnp.concatenate([cos_t, jnp.ones((pad, LANES), F32)], axis=0)
    sin_t = jnp.concatenate([sin_t, jnp.zeros((pad, LANES), F32)], axis=0)
    return cos_t, sin_t


def _c_head_perm():
    cols = []
    for c in range(C_HEADS // 2):
        for head in (c, c + C_HEADS // 2):
            cols.extend(range(head * HEAD_DIM, (head + 1) * HEAD_DIM))
    return jnp.asarray(cols, dtype=jnp.int32)


def _permute_w_in(w):
    gw = GROUP_WIDTH
    kvw = C_KV_HEADS * HEAD_DIM
    cuts = [gw, gw, gw, gw, gw, gw, kvw, kvw, gw, gw, gw]
    offs = [0]
    for c in cuts:
        offs.append(offs[-1] + c)
    aq, ak, av, bu, bv, cq, ck, cv, dq, dk, dv = [w[:, offs[i]:offs[i + 1]] for i in range(11)]
    cq = cq[:, _c_head_perm()]
    pad = jnp.zeros((w.shape[0], SEG - 2 * kvw), w.dtype)
    return jnp.concatenate([aq, ak, av, bu, bv, cq, ck, cv, pad, dq, dk, dv], axis=1).astype(BF16)


def _permute_w_out(w):
    gw = GROUP_WIDTH
    wc = w[2 * gw:3 * gw][_c_head_perm()]
    return jnp.concatenate([w[:2 * gw], wc, w[3 * gw:]], axis=0).astype(BF16)


def _na_bias(rpb):
    cq = jnp.arange(GRID_W)
    cs = jnp.clip(cq - NA_COLS // 2, 0, GRID_W - NA_COLS)
    col_ok = (cq[None, :] >= cs[:, None]) & (cq[None, :] < cs[:, None] + NA_COLS)
    dc = jnp.clip(cq[None, :] - cq[:, None], -(NA_COLS - 1), NA_COLS - 1) + (NA_COLS - 1)
    shift = jnp.arange(NA_ROWS)[:, None]
    dr = jnp.arange(NA_ROWS)[None, :] - shift + (NA_ROWS - 1)
    bias = rpb[:, dr[:, None, :, None], dc[None, :, None, :]].astype(F32)
    bias = jnp.where(col_ok[None, None, :, None, :], bias, NEG_INF)
    return jnp.moveaxis(bias, 0, 1).reshape(NA_ROWS, rpb.shape[0], GRID_W, NA_ROWS * GRID_W)


def kernel(x, c, ctx, c_ctx, w_mod, b_mod, ln_g, ln_b, ffn1_w_in, ffn1_w_out, ffn2_w_in, ffn2_w_out, mix_w_in, mix_w_out, a_lambda, b_norm_g, b_norm_b, b_spatial_w, b_spatial_b, c_sink, d_rpb):
    depth = w_mod.shape[0]
    n = x.shape[1]
    l_ctx = ctx.shape[1]
    d = D_MODEL
    assert x.shape[0] == 1 and n % TM == 0 and n // GRID_W >= NA_ROWS and n % 1024 == 0
    assert l_ctx % CHUNK == 0 and l_ctx <= TM and n % l_ctx == 0
    nlt = n // TM
    rows = n + TM

    xs = jnp.concatenate([x[0], ctx[0], jnp.zeros((rows - n - l_ctx, d), F32)], axis=0)
    cc = jnp.concatenate([c, c_ctx[None], jnp.zeros((6, d), F32)], axis=0)
    mods = _mod_vectors(cc, w_mod, b_mod)[:, :2].reshape(depth, 2 * N_MOD, 1, d)
    cos_t, sin_t = _rope_tables(n, rows)

    for l in range(depth):
        last = l == depth - 1
        lam_init = 0.8 - 0.6 * math.exp(-0.3 * l)
        mods_l = mods[l]
        lng = [ln_g[l, k][None] for k in range(3)]
        lnb = [ln_b[l, k][None] for k in range(3)]

        xs, h = _ffn(xs, mods_l, (0, 1, 2), lng[0], lnb[0], ffn1_w_in[l].astype(BF16), ffn1_w_out[l].astype(BF16),
                     nlt + 1, nlt, emit_ks=(3, 4))
        aq, ak, av, bu, bv, cq, ckv, dq, dk, dv = _inproj(h, _permute_w_in(mix_w_in[l]), cos_t, sin_t)
        ya = _attn_a(a_lambda[l], aq, ak, av, n, l_ctx, lam_init, not last)
        yb = _mixb(bu, bv, b_norm_g[l][None], b_norm_b[l][None], b_spatial_w[l], b_spatial_b[l].T)
        yc = _attn_c(c_sink[l], cq, ckv, n, l_ctx, not last)
        yd = _attn_d(dq, dk, dv, _na_bias(d_rpb[l]), n, l_ctx, not last)
        n_tiles = nlt if last else nlt + 1
        xs = _outproj(ya, yb, yc, yd, _permute_w_out(mix_w_out[l]), xs, mods_l, 5, lng[1], lnb[1], n_tiles, nlt)
        xs = _ffn(xs, mods_l, (6, 7, 8), lng[2], lnb[2], ffn2_w_in[l].astype(BF16), ffn2_w_out[l].astype(BF16),
                  n_tiles, nlt)
    return xs[None]
```

```python
import functools
import math

import jax
import jax.numpy as jnp
from jax import lax
from jax.experimental import pallas as pl
from jax.experimental.pallas import tpu as pltpu

F32 = jnp.float32
BF16 = jnp.bfloat16

D_MODEL = 2048
N_GROUPS = 4
GROUP_WIDTH = D_MODEL // N_GROUPS
HEAD_DIM = 64
GRID_W = 64
CHUNK = 128
B_GROUPS = GROUP_WIDTH // 128
A_HEADS = GROUP_WIDTH // (2 * HEAD_DIM)
C_HEADS = GROUP_WIDTH // HEAD_DIM
C_KV_HEADS = C_HEADS // 4
D_HEADS = GROUP_WIDTH // HEAD_DIM
WINDOW = 128
QBLOCK = 128
NA_ROWS = 8
NA_COLS = 16
D_FF = 256 * math.ceil(8 * D_MODEL / 3 / 256)
N_MOD = 9
ROPE_THETA = 10000.0
LN_EPS = 1e-6
NEG_INF = -1e30
MODEL_DEPTH = 4
DEEPNORM_ALPHA = (2 * MODEL_DEPTH) ** 0.25
QK_SCALE = HEAD_DIM ** -0.5
LOG2E = math.log2(math.e)

LANES = 128
TM = 512
TF = 512
SEG = 512
N_SEG = 10
VMEM_LIMIT = 56 * 1024 * 1024


def _params(sem):
    return pltpu.CompilerParams(dimension_semantics=sem, vmem_limit_bytes=VMEM_LIMIT)


def _ln(t, g, b):
    mu = jnp.mean(t, axis=-1, keepdims=True)
    tc = t - mu
    var = jnp.mean(tc * tc, axis=-1, keepdims=True)
    return tc * lax.rsqrt(var + LN_EPS) * g + b


def _dot_nt(a, b):
    return lax.dot_general(a, b, (((1,), (1,)), ((), ())), preferred_element_type=F32)


def _half_masks(shape):
    lane = lax.broadcasted_iota(jnp.int32, shape, 1)
    return lane < HEAD_DIM, lane >= HEAD_DIM


def _mod_kernel(cc_ref, w_ref, b_ref, o_ref):
    a = cc_ref[...]
    s = (a * jax.nn.sigmoid(a)).astype(BF16)
    o_ref[0] = jnp.dot(s, w_ref[0].astype(BF16), preferred_element_type=F32) + b_ref[0]


def _mod_vectors(cc, w_mod, b_mod):
    depth, d, nm = w_mod.shape
    tn = 1024
    return pl.pallas_call(
        _mod_kernel,
        grid=(depth, nm // tn),
        in_specs=[
            pl.BlockSpec((8, d), lambda l, j: (0, 0)),
            pl.BlockSpec((1, d, tn), lambda l, j: (l, 0, j)),
            pl.BlockSpec((1, 1, tn), lambda l, j: (l, 0, j)),
        ],
        out_specs=pl.BlockSpec((1, 8, tn), lambda l, j: (l, 0, j)),
        out_shape=jax.ShapeDtypeStruct((depth, 8, nm), F32),
        compiler_params=_params(("parallel", "parallel")),
        name="mod_vectors",
    )(cc, w_mod, b_mod.reshape(depth, 1, nm))


def _ffn_kernel(x_ref, sh_ref, sc_ref, gt_ref, lng_ref, lnb_ref, wg_ref, wu_ref, wo_ref, *rest, emit_h):
    if emit_h:
        sh2_ref, sc2_ref, o_ref, h_ref, xin_scr, acc_scr = rest
    else:
        o_ref, xin_scr, acc_scr = rest
    j = pl.program_id(1)

    @pl.when(j == 0)
    def _():
        xin_scr[...] = (x_ref[...] * (1.0 + sc_ref[0]) + sh_ref[0]).astype(BF16)
        acc_scr[...] = jnp.zeros_like(acc_scr)

    xin = xin_scr[...]
    g = jnp.dot(xin, wg_ref[...], preferred_element_type=F32)
    u = jnp.dot(xin, wu_ref[...], preferred_element_type=F32)
    hh = ((g * jax.nn.sigmoid(g)) * u).astype(BF16)
    acc_scr[...] += jnp.dot(hh, wo_ref[...], preferred_element_type=F32)

    @pl.when(j == pl.num_programs(1) - 1)
    def _():
        t = DEEPNORM_ALPHA * x_ref[...] + (0.5 * gt_ref[0]) * acc_scr[...]
        y = _ln(t, lng_ref[...], lnb_ref[...])
        o_ref[...] = y
        if emit_h:
            h_ref[...] = (y * (1.0 + sc2_ref[0]) + sh2_ref[0]).astype(BF16)


def _ffn(x, mods_l, ks, lng, lnb, w_in, w_out, n_tiles, nlt, emit_ks=None):
    d = D_MODEL
    nj = D_FF // TF
    emit_h = emit_ks is not None

    def mod_spec(k):
        return pl.BlockSpec((1, 1, d), lambda i, j, k=k: (jnp.where(i >= nlt, N_MOD, 0) + k, 0, 0))

    vec_spec = pl.BlockSpec((1, d), lambda i, j: (0, 0))
    in_specs = [
        pl.BlockSpec((TM, d), lambda i, j: (i, 0)),
        mod_spec(ks[0]), mod_spec(ks[1]), mod_spec(ks[2]),
        vec_spec, vec_spec,
        pl.BlockSpec((d, TF), lambda i, j: (0, j)),
        pl.BlockSpec((d, TF), lambda i, j: (0, nj + j)),
        pl.BlockSpec((TF, d), lambda i, j: (j, 0)),
    ]
    args = [x, mods_l, mods_l, mods_l, lng, lnb, w_in, w_in, w_out]
    out_specs = pl.BlockSpec((TM, d), lambda i, j: (i, 0))
    out_shape = jax.ShapeDtypeStruct((n_tiles * TM, d), F32)
    if emit_h:
        in_specs += [mod_spec(emit_ks[0]), mod_spec(emit_ks[1])]
        args += [mods_l, mods_l]
        out_specs = [out_specs, pl.BlockSpec((TM, d), lambda i, j: (i, 0))]
        out_shape = [out_shape, jax.ShapeDtypeStruct((n_tiles * TM, d), BF16)]
    return pl.pallas_call(
        functools.partial(_ffn_kernel, emit_h=emit_h),
        grid=(n_tiles, nj),
        in_specs=in_specs,
        out_specs=out_specs,
        out_shape=out_shape,
        scratch_shapes=[pltpu.VMEM((TM, d), BF16), pltpu.VMEM((TM, d), F32)],
        compiler_params=_params(("parallel", "arbitrary")),
        name="ffn_emit" if emit_h else "ffn",
    )(*args)


def _rope(t, cos_ref, sin_ref):
    w = t.shape[1]
    lane = lax.broadcasted_iota(jnp.int32, t.shape, 1)
    first = (lane % HEAD_DIM) < (HEAD_DIM // 2)
    rot = jnp.where(first, pltpu.roll(t, w - HEAD_DIM // 2, 1), pltpu.roll(t, HEAD_DIM // 2, 1))
    reps = w // LANES
    cos = jnp.concatenate([cos_ref[...]] * reps, axis=1) if reps > 1 else cos_ref[...]
    sin = jnp.concatenate([sin_ref[...]] * reps, axis=1) if reps > 1 else sin_ref[...]
    return t * cos + rot * sin


_SEG_AQ, _SEG_AK, _SEG_AV, _SEG_BU, _SEG_BV, _SEG_CQ, _SEG_CKV, _SEG_DQ, _SEG_DK, _SEG_DV = range(N_SEG)


def _inproj_kernel(h_ref, w_ref, cos_ref, sin_ref, *out_refs):
    j = pl.program_id(1)
    for k, o_ref in enumerate(out_refs):

        @pl.when(j == k)
        def _(k=k, o_ref=o_ref):
            t = jnp.dot(h_ref[...], w_ref[...], preferred_element_type=F32)
            if k in (_SEG_AQ, _SEG_AK, _SEG_CQ):
                t = _rope(t, cos_ref, sin_ref)
            if k == _SEG_AQ:
                t = t * (QK_SCALE * LOG2E)
            if k in (_SEG_CQ, _SEG_DQ):
                t = t * QK_SCALE
            if k == _SEG_CKV:
                t = jnp.concatenate([_rope(t[:, :LANES], cos_ref, sin_ref), t[:, LANES:2 * LANES]], axis=1)
            if k in (_SEG_AQ, _SEG_AV):
                o_ref[0] = t.T.astype(o_ref.dtype)
            else:
                o_ref[...] = t.astype(o_ref.dtype)


def _inproj(h, w_perm, cos_t, sin_t):
    rows = h.shape[0]
    n_tiles = rows // TM
    d = D_MODEL
    widths = [SEG] * N_SEG
    widths[_SEG_CKV] = 2 * LANES
    dtypes = [BF16] * N_SEG
    dtypes[_SEG_BU] = F32
    dtypes[_SEG_BV] = F32
    out_specs = [pl.BlockSpec((TM, w), lambda i, j: (i, 0)) for w in widths]
    out_shape = [jax.ShapeDtypeStruct((rows, w), dt) for w, dt in zip(widths, dtypes)]
    for k in (_SEG_AQ, _SEG_AV):
        out_specs[k] = pl.BlockSpec((1, SEG, TM), lambda i, j: (i, 0, 0))
        out_shape[k] = jax.ShapeDtypeStruct((n_tiles, SEG, TM), BF16)
    return pl.pallas_call(
        _inproj_kernel,
        grid=(n_tiles, N_SEG),
        in_specs=[
            pl.BlockSpec((TM, d), lambda i, j: (i, 0)),
            pl.BlockSpec((d, SEG), lambda i, j: (0, j)),
            pl.BlockSpec((TM, LANES), lambda i, j: (i, 0)),
            pl.BlockSpec((TM, LANES), lambda i, j: (i, 0)),
        ],
        out_specs=out_specs,
        out_shape=out_shape,
        compiler_params=_params(("parallel", "arbitrary")),
        name="inproj",
    )(h, w_perm, cos_t, sin_t)


def _attn_a_kernel(lam_ref, qt_ref, k_ref, vt_ref, *rest, n_chunks, tail_k_start, tail_chunk, tail_len, lam_init,
                   aliased):
    if aliased:
        rest = rest[1:]
    o_ref, m_scr, l_scr, acc_scr, sa_scr, sb_scr, ma_scr, mb_scr = rest
    qt = qt_ref[0]
    comp = lax.broadcasted_iota(jnp.int32, qt.shape, 0) < HEAD_DIM
    zero = jnp.zeros_like(qt)
    qts = (jnp.where(comp, qt, zero), jnp.where(comp, zero, qt))
    m_scr[...] = jnp.full(m_scr.shape, NEG_INF, F32)
    l_scr[...] = jnp.zeros_like(l_scr)
    acc_scr[...] = jnp.zeros_like(acc_scr)

    def scores(k, dst, mdst, size):
        for c in range(2):
            s = jnp.dot(k, qts[c], preferred_element_type=F32)
            dst[c, :size] = s
            mdst[c] = jnp.max(s, axis=0, keepdims=True)

    def consume(src, msrc, vt, size):
        for c in range(2):
            m_old = m_scr[c]
            m_new = jnp.maximum(m_old, msrc[c])
            alpha = jnp.exp2(m_old - m_new)
            p = jnp.exp2(src[c, :size] - m_new)
            l_scr[c] = alpha * l_scr[c] + jnp.sum(p, axis=0, keepdims=True)
            acc_scr[c] = alpha * acc_scr[c] + jnp.dot(vt, p.astype(BF16), preferred_element_type=F32)
            m_scr[c] = m_new

    def k_chunk(i):
        return k_ref[pl.ds(pl.multiple_of(i * TM, TM), TM), :]

    if n_chunks:
        assert n_chunks % 2 == 0
        scores(k_chunk(0), sa_scr, ma_scr, TM)

        def body(p, carry):
            c0 = 2 * p
            scores(k_chunk(c0 + 1), sb_scr, mb_scr, TM)
            consume(sa_scr, ma_scr, vt_ref[c0], TM)
            scores(k_chunk(jnp.minimum(c0 + 2, n_chunks - 1)), sa_scr, ma_scr, TM)
            consume(sb_scr, mb_scr, vt_ref[c0 + 1], TM)
            return carry
        lax.fori_loop(0, n_chunks // 2, body, 0)
    scores(k_ref[pl.ds(tail_k_start, tail_len), :], sa_scr, ma_scr, tail_len)
    consume(sa_scr, ma_scr, vt_ref[tail_chunk][:, :tail_len], tail_len)

    lv = lam_ref[...]
    lam = (jnp.exp(jnp.sum(lv[0:1] * lv[1:2], axis=1, keepdims=True))
           - jnp.exp(jnp.sum(lv[2:3] * lv[3:4], axis=1, keepdims=True)) + lam_init)
    ot = acc_scr[0] / l_scr[0] - lam * (acc_scr[1] / l_scr[1])
    ms = jnp.mean(ot * ot, axis=0, keepdims=True)
    ot = (ot * lax.rsqrt(ms + LN_EPS)) * (1.0 - lam_init)
    o_ref[...] = ot.T.astype(BF16)


def _attn_a(lam_vecs, aqt, ak, avt, n, l_ctx, lam_init, need_ctx):
    rows = ak.shape[0]
    nt = rows // TM
    nlt = n // TM
    scratch = [pltpu.VMEM((2, 1, TM), F32), pltpu.VMEM((2, 1, TM), F32), pltpu.VMEM((2, LANES, TM), F32),
               pltpu.VMEM((2, TM, TM), F32), pltpu.VMEM((2, TM, TM), F32),
               pltpu.VMEM((2, 1, TM), F32), pltpu.VMEM((2, 1, TM), F32)]
    lam_spec = pl.BlockSpec((4, HEAD_DIM), lambda h, i: (0, 0))
    ya = pl.pallas_call(
        functools.partial(_attn_a_kernel, n_chunks=nlt, tail_k_start=n, tail_chunk=nlt, tail_len=l_ctx,
                          lam_init=lam_init, aliased=False),
        grid=(A_HEADS, nlt),
        in_specs=[
            lam_spec,
            pl.BlockSpec((1, LANES, TM), lambda h, i: (i, h, 0)),
            pl.BlockSpec((rows, LANES), lambda h, i: (0, h)),
            pl.BlockSpec((nt, LANES, TM), lambda h, i: (0, h, 0)),
        ],
        out_specs=pl.BlockSpec((TM, LANES), lambda h, i: (i, h)),
        out_shape=jax.ShapeDtypeStruct((rows, GROUP_WIDTH), BF16),
        scratch_shapes=scratch,
        compiler_params=_params(("parallel", "parallel")),
        name="attn_a",
    )(lam_vecs, aqt, ak, avt)
    if not need_ctx:
        return ya
    cb = n // l_ctx
    return pl.pallas_call(
        functools.partial(_attn_a_kernel, n_chunks=0, tail_k_start=0, tail_chunk=0, tail_len=l_ctx,
                          lam_init=lam_init, aliased=True),
        grid=(A_HEADS, 1),
        in_specs=[
            lam_spec,
            pl.BlockSpec((1, LANES, TM), lambda h, i: (nlt, h, 0)),
            pl.BlockSpec((l_ctx, LANES), lambda h, i: (cb, h)),
            pl.BlockSpec((1, LANES, TM), lambda h, i: (nlt, h, 0)),
            pl.BlockSpec(memory_space=pl.ANY),
        ],
        out_specs=pl.BlockSpec((TM, LANES), lambda h, i: (nlt, h)),
        out_shape=jax.ShapeDtypeStruct((rows, GROUP_WIDTH), BF16),
        scratch_shapes=scratch,
        input_output_aliases={4: 0},
        compiler_params=_params(("parallel", "parallel")),
        name="attn_a_ctx",
    )(lam_vecs, aqt, ak, avt, ya)


def _mixb_kernel(u_ref, v_ref, g_ref, b_ref, ws_ref, bs_ref, o_ref):
    u = jax.nn.gelu(u_ref[...])
    v = _ln(jax.nn.gelu(v_ref[...]), g_ref[...], b_ref[...]).astype(BF16)
    for c in range(u.shape[0] // CHUNK):
        rs = slice(c * CHUNK, (c + 1) * CHUNK)
        for g in range(B_GROUPS):
            cs = slice(g * LANES, (g + 1) * LANES)
            mixed = jnp.dot(ws_ref[g].astype(BF16), v[rs, cs], preferred_element_type=F32) + bs_ref[:, g:g + 1]
            o_ref[rs, cs] = (u[rs, cs] * mixed).astype(BF16)


def _mixb(bu, bv, gn_g, gn_b, w_s, b_s_t):
    rows = bu.shape[0]
    gw = GROUP_WIDTH
    tile = pl.BlockSpec((TM, gw), lambda i: (i, 0))
    return pl.pallas_call(
        _mixb_kernel,
        grid=(rows // TM,),
        in_specs=[
            tile, tile,
            pl.BlockSpec((1, gw), lambda i: (0, 0)),
            pl.BlockSpec((1, gw), lambda i: (0, 0)),
            pl.BlockSpec((B_GROUPS, CHUNK, CHUNK), lambda i: (0, 0, 0)),
            pl.BlockSpec((CHUNK, B_GROUPS), lambda i: (0, 0)),
        ],
        out_specs=tile,
        out_shape=jax.ShapeDtypeStruct((rows, gw), BF16),
        compiler_params=_params(("parallel",)),
        name="mix_b",
    )(bu, bv, gn_g, gn_b, w_s, b_s_t)


def _softmax_pv(scores, values, sink=None):
    m = jnp.max(scores[0], axis=1, keepdims=True)
    for s in scores[1:]:
        m = jnp.maximum(m, jnp.max(s, axis=1, keepdims=True))
    if sink is not None:
        m = jnp.maximum(m, sink)
    den = None
    num = None
    for s, v in zip(scores, values):
        p = jnp.exp(s - m)
        ps = jnp.sum(p, axis=1, keepdims=True)
        den = ps if den is None else den + ps
        pv = jnp.dot(p.astype(BF16), v, preferred_element_type=F32)
        num = pv if num is None else num + pv
    if sink is not None:
        den = den + jnp.exp(sink - m)
    return num / den


def _attn_c_kernel(sink_ref, q_ref, k_ref, v_ref, *rest, n, l_ctx, local, aliased):
    o_ref = rest[-1]
    i = pl.program_id(0)
    n_sub = q_ref.shape[0] // QBLOCK
    wlen = 3 * QBLOCK
    ctx_start = n if local else 0
    kc = k_ref[pl.ds(ctx_start, l_ctx), :]
    vc = v_ref[pl.ds(ctx_start, l_ctx), :]
    kfirst, ksecond = _half_masks(vc.shape)
    vcs = (jnp.where(kfirst, vc, jnp.zeros_like(vc)), jnp.where(ksecond, vc, jnp.zeros_like(vc)))

    def sub_block(b, carry):
        r0 = pl.multiple_of(b * QBLOCK, QBLOCK)
        if local:
            blk = i * n_sub + b
            start = pl.multiple_of(jnp.clip((blk - 1) * QBLOCK, 0, n - wlen), QBLOCK)
            kl = k_ref[pl.ds(start, wlen), :]
            vl = v_ref[pl.ds(start, wlen), :]
            lfirst, lsecond = _half_masks(vl.shape)
            vls = (jnp.where(lfirst, vl, jnp.zeros_like(vl)), jnp.where(lsecond, vl, jnp.zeros_like(vl)))
            qpos = blk * QBLOCK + lax.broadcasted_iota(jnp.int32, (QBLOCK, wlen), 0)
            kpos = start + lax.broadcasted_iota(jnp.int32, (QBLOCK, wlen), 1)
            valid = jnp.abs(kpos - qpos) <= WINDOW
        for c in range(C_HEADS // 2):
            qb = q_ref[pl.ds(r0, QBLOCK), c * LANES:(c + 1) * LANES]
            qfirst, qsecond = _half_masks(qb.shape)
            out = None
            for half in range(2):
                qm = jnp.where(qfirst if half == 0 else qsecond, qb, jnp.zeros_like(qb))
                sink = sink_ref[half * (C_HEADS // 2) + c]
                scores = [_dot_nt(qm, kc)]
                values = [vcs[half]]
                if local:
                    scores.append(jnp.where(valid, _dot_nt(qm, kl), NEG_INF))
                    values.append(vls[half])
                o = _softmax_pv(scores, values, sink)
                out = o if out is None else out + o
            o_ref[pl.ds(r0, QBLOCK), c * LANES:(c + 1) * LANES] = out.astype(BF16)
        return carry

    lax.fori_loop(0, n_sub, sub_block, 0)


def _attn_c(sink, cq, ckv, n, l_ctx, need_ctx):
    rows = cq.shape[0]
    gw = GROUP_WIDTH
    smem = pl.BlockSpec(memory_space=pltpu.SMEM)
    yc = pl.pallas_call(
        functools.partial(_attn_c_kernel, n=n, l_ctx=l_ctx, local=True, aliased=False),
        grid=(n // TM,),
        in_specs=[
            smem,
            pl.BlockSpec((TM, gw), lambda i: (i, 0)),
            pl.BlockSpec((rows, LANES), lambda i: (0, 0)),
            pl.BlockSpec((rows, LANES), lambda i: (0, 1)),
        ],
        out_specs=pl.BlockSpec((TM, gw), lambda i: (i, 0)),
        out_shape=jax.ShapeDtypeStruct((rows, gw), BF16),
        compiler_params=_params(("parallel",)),
        name="attn_c",
    )(sink, cq, ckv, ckv)
    if not need_ctx:
        return yc
    ct = n // TM
    cb = n // l_ctx
    return pl.pallas_call(
        functools.partial(_attn_c_kernel, n=n, l_ctx=l_ctx, local=False, aliased=True),
        grid=(1,),
        in_specs=[
            smem,
            pl.BlockSpec((TM, gw), lambda i: (ct, 0)),
            pl.BlockSpec((l_ctx, LANES), lambda i: (cb, 0)),
            pl.BlockSpec((l_ctx, LANES), lambda i: (cb, 1)),
            pl.BlockSpec(memory_space=pl.ANY),
        ],
        out_specs=pl.BlockSpec((TM, gw), lambda i: (ct, 0)),
        out_shape=jax.ShapeDtypeStruct((rows, gw), BF16),
        input_output_aliases={4: 0},
        compiler_params=_params(("parallel",)),
        name="attn_c_ctx",
    )(sink, cq, ckv, ckv, yc)


def _attn_d_kernel(q_ref, kc_ref, vc_ref, *rest, local, aliased):
    o_ref = rest[-1]
    if local:
        k_ref, v_ref, b_ref = rest[0], rest[1], rest[2]
    for c in range(D_HEADS // 2):
        cs = slice(c * LANES, (c + 1) * LANES)
        qb = q_ref[:, cs]
        qfirst, qsecond = _half_masks(qb.shape)
        kcb = kc_ref[:, cs]
        vcb = vc_ref[:, cs]
        cfirst, csecond = _half_masks(vcb.shape)
        if local:
            kb = k_ref[:, cs]
            vb = v_ref[:, cs]
            nfirst, nsecond = _half_masks(vb.shape)
        out = None
        for half in range(2):
            qm = jnp.where(qfirst if half == 0 else qsecond, qb, jnp.zeros_like(qb))
            scores = [_dot_nt(qm, kcb)]
            values = [jnp.where(cfirst if half == 0 else csecond, vcb, jnp.zeros_like(vcb))]
            if local:
                scores.append(_dot_nt(qm, kb) + b_ref[0, 2 * c + half])
                values.append(jnp.where(nfirst if half == 0 else nsecond, vb, jnp.zeros_like(vb)))
            o = _softmax_pv(scores, values)
            out = o if out is None else out + o
        o_ref[:, cs] = out.astype(BF16)


def _attn_d(dq, dk, dv, bias, n, l_ctx, need_ctx):
    rows = dq.shape[0]
    gw = GROUP_WIDTH
    n_rows = n // GRID_W
    nk = NA_ROWS * GRID_W
    cb = n // l_ctx

    def row_start(r):
        return jnp.clip(r - NA_ROWS // 2, 0, n_rows - NA_ROWS)

    yd = pl.pallas_call(
        functools.partial(_attn_d_kernel, local=True, aliased=False),
        grid=(n_rows,),
        in_specs=[
            pl.BlockSpec((GRID_W, gw), lambda r: (r, 0)),
            pl.BlockSpec((l_ctx, gw), lambda r: (cb, 0)),
            pl.BlockSpec((l_ctx, gw), lambda r: (cb, 0)),
            pl.BlockSpec((pl.Element(nk), pl.Element(gw)), lambda r: (row_start(r) * GRID_W, 0)),
            pl.BlockSpec((pl.Element(nk), pl.Element(gw)), lambda r: (row_start(r) * GRID_W, 0)),
            pl.BlockSpec((1, D_HEADS, GRID_W, nk), lambda r: (r - row_start(r), 0, 0, 0)),
        ],
        out_specs=pl.BlockSpec((GRID_W, gw), lambda r: (r, 0)),
        out_shape=jax.ShapeDtypeStruct((rows, gw), BF16),
        compiler_params=_params(("parallel",)),
        name="attn_d",
    )(dq, dk, dv, dk, dv, bias)
    if not need_ctx:
        return yd
    ct = n // TM
    return pl.pallas_call(
        functools.partial(_attn_d_kernel, local=False, aliased=True),
        grid=(1,),
        in_specs=[
            pl.BlockSpec((TM, gw), lambda i: (ct, 0)),
            pl.BlockSpec((l_ctx, gw), lambda i: (cb, 0)),
            pl.BlockSpec((l_ctx, gw), lambda i: (cb, 0)),
            pl.BlockSpec(memory_space=pl.ANY),
        ],
        out_specs=pl.BlockSpec((TM, gw), lambda i: (ct, 0)),
        out_shape=jax.ShapeDtypeStruct((rows, gw), BF16),
        input_output_aliases={3: 0},
        compiler_params=_params(("parallel",)),
        name="attn_d_ctx",
    )(dq, dk, dv, yd)


def _outproj_kernel(ya_ref, yb_ref, yc_ref, yd_ref, w_ref, x_ref, gt_ref, lng_ref, lnb_ref, o_ref):
    y_in = jnp.concatenate([ya_ref[...], yb_ref[...], yc_ref[...], yd_ref[...]], axis=1)
    y = jnp.dot(y_in, w_ref[...], preferred_element_type=F32)
    t = DEEPNORM_ALPHA * x_ref[...] + gt_ref[0] * y
    o_ref[...] = _ln(t, lng_ref[...], lnb_ref[...])


def _outproj(ya, yb, yc, yd, w_out, x, mods_l, gate_k, lng, lnb, n_tiles, nlt):
    d = D_MODEL
    gw = GROUP_WIDTH
    ytile = pl.BlockSpec((TM, gw), lambda i: (i, 0))
    vec = pl.BlockSpec((1, d), lambda i: (0, 0))
    return pl.pallas_call(
        _outproj_kernel,
        grid=(n_tiles,),
        in_specs=[
            ytile, ytile, ytile, ytile,
            pl.BlockSpec((d, d), lambda i: (0, 0)),
            pl.BlockSpec((TM, d), lambda i: (i, 0)),
            pl.BlockSpec((1, 1, d), lambda i: (jnp.where(i >= nlt, N_MOD, 0) + gate_k, 0, 0)),
            vec, vec,
        ],
        out_specs=pl.BlockSpec((TM, d), lambda i: (i, 0)),
        out_shape=jax.ShapeDtypeStruct((n_tiles * TM, d), F32),
        compiler_params=_params(("parallel",)),
        name="outproj",
    )(ya, yb, yc, yd, w_out, x, mods_l, lng, lnb)


def _rope_tables(n, rows):
    t = jnp.arange(n, dtype=jnp.int32)
    row = (t // GRID_W).astype(F32)
    col = (t % GRID_W).astype(F32)
    n_freq = HEAD_DIM // 4
    inv = ROPE_THETA ** (-jnp.arange(n_freq, dtype=F32) / n_freq)
    ang = jnp.concatenate([row[:, None] * inv, col[:, None] * inv], axis=-1)
    cos, sin = jnp.cos(ang), jnp.sin(ang)
    cos_t = jnp.tile(jnp.concatenate([cos, cos], axis=-1), (1, LANES // HEAD_DIM))
    sin_t = jnp.tile(jnp.concatenate([-sin, sin], axis=-1), (1, LANES // HEAD_DIM))
    pad = rows - n
    cos_t = jnp.concatenate([cos_t, jnp.ones((pad, LANES), F32)], axis=0)
    sin_t = jnp.concatenate([sin_t, jnp.zeros((pad, LANES), F32)], axis=0)
    return cos_t, sin_t


def _c_head_perm():
    cols = []
    for c in range(C_HEADS // 2):
        for head in (c, c + C_HEADS // 2):
            cols.extend(range(head * HEAD_DIM, (head + 1) * HEAD_DIM))
    return jnp.asarray(cols, dtype=jnp.int32)


def _permute_w_in(w):
    gw = GROUP_WIDTH
    kvw = C_KV_HEADS * HEAD_DIM
    cuts = [gw, gw, gw, gw, gw, gw, kvw, kvw, gw, gw, gw]
    offs = [0]
    for c in cuts:
        offs.append(offs[-1] + c)
    aq, ak, av, bu, bv, cq, ck, cv, dq, dk, dv = [w[:, offs[i]:offs[i + 1]] for i in range(11)]
    cq = cq[:, _c_head_perm()]
    pad = jnp.zeros((w.shape[0], SEG - 2 * kvw), w.dtype)
    return jnp.concatenate([aq, ak, av, bu, bv, cq, ck, cv, pad, dq, dk, dv], axis=1).astype(BF16)


def _permute_w_out(w):
    gw = GROUP_WIDTH
    wc = w[2 * gw:3 * gw][_c_head_perm()]
    return jnp.concatenate([w[:2 * gw], wc, w[3 * gw:]], axis=0).astype(BF16)


def _na_bias(rpb):
    w = GRID_W
    cq = jnp.arange(w)
    cs = jnp.clip(cq - NA_COLS // 2, 0, w - NA_COLS)
    col_ok = (cq[None, :] >= cs[:, None]) & (cq[None, :] < cs[:, None] + NA_COLS)
    edge = w - NA_COLS
    ext = jnp.pad(rpb.astype(F32), ((0, 0), (0, 0), (edge, edge)), mode="edge")
    toep = jnp.stack([ext[:, :, w - 1 - q:2 * w - 1 - q] for q in range(w)], axis=2)
    toep = jnp.where(col_ok[None, None], toep, NEG_INF)
    per_shift = []
    for shift in range(NA_ROWS):
        sl = toep[:, NA_ROWS - 1 - shift:2 * NA_ROWS - 1 - shift]
        per_shift.append(jnp.swapaxes(sl, 1, 2).reshape(rpb.shape[0], w, NA_ROWS * w))
    return jnp.stack(per_shift, axis=0)


def kernel(x, c, ctx, c_ctx, w_mod, b_mod, ln_g, ln_b, ffn1_w_in, ffn1_w_out, ffn2_w_in, ffn2_w_out, mix_w_in, mix_w_out, a_lambda, b_norm_g, b_norm_b, b_spatial_w, b_spatial_b, c_sink, d_rpb):
    depth = w_mod.shape[0]
    n = x.shape[1]
    l_ctx = ctx.shape[1]
    d = D_MODEL
    assert x.shape[0] == 1 and n % TM == 0 and n // GRID_W >= NA_ROWS and n % 1024 == 0
    assert l_ctx % CHUNK == 0 and l_ctx <= TM and n % l_ctx == 0
    nlt = n // TM
    rows = n + TM

    xs = jnp.concatenate([x[0], ctx[0], jnp.zeros((rows - n - l_ctx, d), F32)], axis=0)
    cc = jnp.concatenate([c, c_ctx[None], jnp.zeros((6, d), F32)], axis=0)
    mods = _mod_vectors(cc, w_mod, b_mod)[:, :2].reshape(depth, 2 * N_MOD, 1, d)
    cos_t, sin_t = _rope_tables(n, rows)

    for l in range(depth):
        last = l == depth - 1
        lam_init = 0.8 - 0.6 * math.exp(-0.3 * l)
        mods_l = mods[l]
        lng = [ln_g[l, k][None] for k in range(3)]
        lnb = [ln_b[l, k][None] for k in range(3)]

        xs, h = _ffn(xs, mods_l, (0, 1, 2), lng[0], lnb[0], ffn1_w_in[l].astype(BF16), ffn1_w_out[l].astype(BF16),
                     nlt + 1, nlt, emit_ks=(3, 4))
        aq, ak, av, bu, bv, cq, ckv, dq, dk, dv = _inproj(h, _permute_w_in(mix_w_in[l]), cos_t, sin_t)
        ya = _attn_a(a_lambda[l], aq, ak, av, n, l_ctx, lam_init, not last)
        yb = _mixb(bu, bv, b_norm_g[l][None], b_norm_b[l][None], b_spatial_w[l], b_spatial_b[l].T)
        yc = _attn_c(c_sink[l], cq, ckv, n, l_ctx, not last)
        yd = _attn_d(dq, dk, dv, _na_bias(d_rpb[l]), n, l_ctx, not last)
        n_tiles = nlt if last else nlt + 1
        xs = _outproj(ya, yb, yc, yd, _permute_w_out(mix_w_out[l]), xs, mods_l, 5, lng[1], lnb[1], n_tiles, nlt)
        xs = _ffn(xs, mods_l, (6, 7, 8), lng[2], lnb[2], ffn2_w_in[l].astype(BF16), ffn2_w_out[l].astype(BF16),
                  n_tiles, nlt)
    return xs[None]
```

```python
import functools
import math

import jax
import jax.numpy as jnp
from jax import lax
from jax.experimental import pallas as pl
from jax.experimental.pallas import tpu as pltpu

F32 = jnp.float32
BF16 = jnp.bfloat16

D_MODEL = 2048
N_GROUPS = 4
GROUP_WIDTH = D_MODEL // N_GROUPS
HEAD_DIM = 64
GRID_W = 64
CHUNK = 128
B_GROUPS = GROUP_WIDTH // 128
A_HEADS = GROUP_WIDTH // (2 * HEAD_DIM)
C_HEADS = GROUP_WIDTH // HEAD_DIM
C_KV_HEADS = C_HEADS // 4
D_HEADS = GROUP_WIDTH // HEAD_DIM
WINDOW = 128
QBLOCK = 128
NA_ROWS = 8
NA_COLS = 16
D_FF = 256 * math.ceil(8 * D_MODEL / 3 / 256)
N_MOD = 9
ROPE_THETA = 10000.0
LN_EPS = 1e-6
NEG_INF = -1e30
MODEL_DEPTH = 4
DEEPNORM_ALPHA = (2 * MODEL_DEPTH) ** 0.25
QK_SCALE = HEAD_DIM ** -0.5
LOG2E = math.log2(math.e)
A_VROWS = 2 * HEAD_DIM + 16
A_UNROLL = 8

LANES = 128
TM = 512
TF = 512
SEG = 512
N_SEG = 10
VMEM_LIMIT = 56 * 1024 * 1024


def _params(sem):
    return pltpu.CompilerParams(dimension_semantics=sem, vmem_limit_bytes=VMEM_LIMIT)


def _ln(t, g, b):
    mu = jnp.mean(t, axis=-1, keepdims=True)
    tc = t - mu
    var = jnp.mean(tc * tc, axis=-1, keepdims=True)
    return tc * lax.rsqrt(var + LN_EPS) * g + b


def _dot_nt(a, b):
    return lax.dot_general(a, b, (((1,), (1,)), ((), ())), preferred_element_type=F32)


def _half_masks(shape):
    lane = lax.broadcasted_iota(jnp.int32, shape, 1)
    return lane < HEAD_DIM, lane >= HEAD_DIM


def _mod_kernel(cc_ref, w_ref, b_ref, o_ref):
    a = cc_ref[...]
    s = (a * jax.nn.sigmoid(a)).astype(BF16)
    o_ref[0] = jnp.dot(s, w_ref[0].astype(BF16), preferred_element_type=F32) + b_ref[0]


def _mod_vectors(cc, w_mod, b_mod):
    depth, d, nm = w_mod.shape
    tn = 1024
    return pl.pallas_call(
        _mod_kernel,
        grid=(depth, nm // tn),
        in_specs=[
            pl.BlockSpec((8, d), lambda l, j: (0, 0)),
            pl.BlockSpec((1, d, tn), lambda l, j: (l, 0, j)),
            pl.BlockSpec((1, 1, tn), lambda l, j: (l, 0, j)),
        ],
        out_specs=pl.BlockSpec((1, 8, tn), lambda l, j: (l, 0, j)),
        out_shape=jax.ShapeDtypeStruct((depth, 8, nm), F32),
        compiler_params=_params(("parallel", "parallel")),
        name="mod_vectors",
    )(cc, w_mod, b_mod.reshape(depth, 1, nm))


def _ffn_kernel(x_ref, sh_ref, sc_ref, gt_ref, lng_ref, lnb_ref, wg_ref, wu_ref, wo_ref, *rest, emit_h):
    if emit_h:
        sh2_ref, sc2_ref, o_ref, h_ref, xin_scr, acc_scr = rest
    else:
        o_ref, xin_scr, acc_scr = rest
    j = pl.program_id(1)

    @pl.when(j == 0)
    def _():
        xin_scr[...] = (x_ref[...] * (1.0 + sc_ref[0]) + sh_ref[0]).astype(BF16)
        acc_scr[...] = jnp.zeros_like(acc_scr)

    xin = xin_scr[...]
    g = jnp.dot(xin, wg_ref[...], preferred_element_type=F32)
    u = jnp.dot(xin, wu_ref[...], preferred_element_type=F32)
    hh = ((g * jax.nn.sigmoid(g)) * u).astype(BF16)
    acc_scr[...] += jnp.dot(hh, wo_ref[...], preferred_element_type=F32)

    @pl.when(j == pl.num_programs(1) - 1)
    def _():
        t = DEEPNORM_ALPHA * x_ref[...] + (0.5 * gt_ref[0]) * acc_scr[...]
        y = _ln(t, lng_ref[...], lnb_ref[...])
        o_ref[...] = y
        if emit_h:
            h_ref[...] = (y * (1.0 + sc2_ref[0]) + sh2_ref[0]).astype(BF16)


def _ffn(x, mods_l, ks, lng, lnb, w_in, w_out, n_tiles, nlt, emit_ks=None):
    d = D_MODEL
    nj = D_FF // TF
    emit_h = emit_ks is not None

    def mod_spec(k):
        return pl.BlockSpec((1, 1, d), lambda i, j, k=k: (jnp.where(i >= nlt, N_MOD, 0) + k, 0, 0))

    vec_spec = pl.BlockSpec((1, d), lambda i, j: (0, 0))
    in_specs = [
        pl.BlockSpec((TM, d), lambda i, j: (i, 0)),
        mod_spec(ks[0]), mod_spec(ks[1]), mod_spec(ks[2]),
        vec_spec, vec_spec,
        pl.BlockSpec((d, TF), lambda i, j: (0, j)),
        pl.BlockSpec((d, TF), lambda i, j: (0, nj + j)),
        pl.BlockSpec((TF, d), lambda i, j: (j, 0)),
    ]
    args = [x, mods_l, mods_l, mods_l, lng, lnb, w_in, w_in, w_out]
    out_specs = pl.BlockSpec((TM, d), lambda i, j: (i, 0))
    out_shape = jax.ShapeDtypeStruct((n_tiles * TM, d), F32)
    if emit_h:
        in_specs += [mod_spec(emit_ks[0]), mod_spec(emit_ks[1])]
        args += [mods_l, mods_l]
        out_specs = [out_specs, pl.BlockSpec((TM, d), lambda i, j: (i, 0))]
        out_shape = [out_shape, jax.ShapeDtypeStruct((n_tiles * TM, d), BF16)]
    return pl.pallas_call(
        functools.partial(_ffn_kernel, emit_h=emit_h),
        grid=(n_tiles, nj),
        in_specs=in_specs,
        out_specs=out_specs,
        out_shape=out_shape,
        scratch_shapes=[pltpu.VMEM((TM, d), BF16), pltpu.VMEM((TM, d), F32)],
        compiler_params=_params(("parallel", "arbitrary")),
        name="ffn_emit" if emit_h else "ffn",
    )(*args)


def _rope(t, cos_ref, sin_ref):
    w = t.shape[1]
    lane = lax.broadcasted_iota(jnp.int32, t.shape, 1)
    first = (lane % HEAD_DIM) < (HEAD_DIM // 2)
    rot = jnp.where(first, pltpu.roll(t, w - HEAD_DIM // 2, 1), pltpu.roll(t, HEAD_DIM // 2, 1))
    reps = w // LANES
    cos = jnp.concatenate([cos_ref[...]] * reps, axis=1) if reps > 1 else cos_ref[...]
    sin = jnp.concatenate([sin_ref[...]] * reps, axis=1) if reps > 1 else sin_ref[...]
    return t * cos + rot * sin


_SEG_AQ, _SEG_AK, _SEG_AV, _SEG_BU, _SEG_BV, _SEG_CQ, _SEG_CKV, _SEG_DQ, _SEG_DK, _SEG_DV = range(N_SEG)


def _inproj_kernel(h_ref, w_ref, cos_ref, sin_ref, *out_refs):
    j = pl.program_id(1)
    for k, o_ref in enumerate(out_refs):

        @pl.when(j == k)
        def _(k=k, o_ref=o_ref):
            t = jnp.dot(h_ref[...], w_ref[...], preferred_element_type=F32)
            if k in (_SEG_AQ, _SEG_AK, _SEG_CQ):
                t = _rope(t, cos_ref, sin_ref)
            if k == _SEG_AQ:
                t = t * (QK_SCALE * LOG2E)
            if k in (_SEG_CQ, _SEG_DQ):
                t = t * QK_SCALE
            if k == _SEG_CKV:
                t = jnp.concatenate([_rope(t[:, :LANES], cos_ref, sin_ref), t[:, LANES:2 * LANES]], axis=1)
            if k == _SEG_AQ:
                o_ref[0] = t.T.astype(o_ref.dtype)
            elif k == _SEG_AV:
                tt = t.T.astype(o_ref.dtype)
                extra = A_VROWS - 2 * HEAD_DIM
                ones_row = (lax.broadcasted_iota(jnp.int32, (extra, tt.shape[1]), 0) == 0).astype(o_ref.dtype)
                for hd in range(A_HEADS):
                    o_ref[0, hd * A_VROWS:hd * A_VROWS + 2 * HEAD_DIM] = tt[hd * 2 * HEAD_DIM:(hd + 1) * 2 * HEAD_DIM]
                    o_ref[0, hd * A_VROWS + 2 * HEAD_DIM:(hd + 1) * A_VROWS] = ones_row
            else:
                o_ref[...] = t.astype(o_ref.dtype)


def _inproj(h, w_perm, cos_t, sin_t):
    rows = h.shape[0]
    n_tiles = rows // TM
    d = D_MODEL
    widths = [SEG] * N_SEG
    widths[_SEG_CKV] = 2 * LANES
    dtypes = [BF16] * N_SEG
    dtypes[_SEG_BU] = F32
    dtypes[_SEG_BV] = F32
    out_specs = [pl.BlockSpec((TM, w), lambda i, j: (i, 0)) for w in widths]
    out_shape = [jax.ShapeDtypeStruct((rows, w), dt) for w, dt in zip(widths, dtypes)]
    for k in (_SEG_AQ, _SEG_AV):
        slab = SEG if k == _SEG_AQ else A_HEADS * A_VROWS
        out_specs[k] = pl.BlockSpec((1, slab, TM), lambda i, j: (i, 0, 0))
        out_shape[k] = jax.ShapeDtypeStruct((n_tiles, slab, TM), BF16)
    return pl.pallas_call(
        _inproj_kernel,
        grid=(n_tiles, N_SEG),
        in_specs=[
            pl.BlockSpec((TM, d), lambda i, j: (i, 0)),
            pl.BlockSpec((d, SEG), lambda i, j: (0, j)),
            pl.BlockSpec((TM, LANES), lambda i, j: (i, 0)),
            pl.BlockSpec((TM, LANES), lambda i, j: (i, 0)),
        ],
        out_specs=out_specs,
        out_shape=out_shape,
        compiler_params=_params(("parallel", "arbitrary")),
        name="inproj",
    )(h, w_perm, cos_t, sin_t)


def _attn_a_kernel(lam_ref, qt_ref, k_ref, vt_ref, *rest, n_chunks, tail_k_start, tail_chunk, tail_len, lam_init,
                   aliased):
    if aliased:
        rest = rest[1:]
    o_ref, m_scr, acc_scr, s_scr, mc_scr = rest
    qt = qt_ref[0]
    comp = lax.broadcasted_iota(jnp.int32, qt.shape, 0) < HEAD_DIM
    zero = jnp.zeros_like(qt)
    qts = (jnp.where(comp, qt, zero), jnp.where(comp, zero, qt))
    m_scr[...] = jnp.full(m_scr.shape, NEG_INF, F32)
    acc_scr[...] = jnp.zeros_like(acc_scr)

    def scores(k, slot, size):
        for c in range(2):
            s = jnp.dot(k, qts[c], preferred_element_type=F32)
            s_scr[slot, c, :size] = s
            mc_scr[slot, c] = jnp.max(s, axis=0, keepdims=True)

    def consume(slot, vt, size):
        for c in range(2):
            m_old = m_scr[c]
            m_new = jnp.maximum(m_old, mc_scr[slot, c])
            alpha = jnp.exp2(m_old - m_new)
            p = jnp.exp2(s_scr[slot, c, :size] - m_new).astype(BF16)
            acc_scr[c] = alpha * acc_scr[c] + jnp.dot(vt, p, preferred_element_type=F32)
            m_scr[c] = m_new

    def k_chunk(i):
        i = jnp.minimum(i, n_chunks - 1)
        return k_ref[pl.ds(pl.multiple_of(i * TM, TM), TM), :]

    if n_chunks:
        assert n_chunks % A_UNROLL == 0 and A_UNROLL % 2 == 0
        scores(k_chunk(0), 0, TM)

        def body(t, carry):
            j = A_UNROLL * t
            for u in range(A_UNROLL):
                scores(k_chunk(j + u + 1), (u + 1) % 2, TM)
                consume(u % 2, vt_ref[j + u], TM)
            return carry
        lax.fori_loop(0, n_chunks // A_UNROLL, body, 0)
    scores(k_ref[pl.ds(tail_k_start, tail_len), :], 0, tail_len)
    consume(0, vt_ref[tail_chunk][:, :tail_len], tail_len)

    lv = lam_ref[...]
    lam = (jnp.exp(jnp.sum(lv[0:1] * lv[1:2], axis=1, keepdims=True))
           - jnp.exp(jnp.sum(lv[2:3] * lv[3:4], axis=1, keepdims=True)) + lam_init)
    nv = 2 * HEAD_DIM
    ot = (acc_scr[0, :nv] / acc_scr[0, nv:nv + 1]
          - lam * (acc_scr[1, :nv] / acc_scr[1, nv:nv + 1]))
    ms = jnp.mean(ot * ot, axis=0, keepdims=True)
    ot = (ot * lax.rsqrt(ms + LN_EPS)) * (1.0 - lam_init)
    o_ref[...] = ot.T.astype(BF16)


def _attn_a(lam_vecs, aqt, ak, avt, n, l_ctx, lam_init, need_ctx):
    rows = ak.shape[0]
    nt = rows // TM
    nlt = n // TM
    scratch = [pltpu.VMEM((2, 1, TM), F32), pltpu.VMEM((2, A_VROWS, TM), F32),
               pltpu.VMEM((2, 2, TM, TM), F32), pltpu.VMEM((2, 2, 1, TM), F32)]
    lam_spec = pl.BlockSpec((4, HEAD_DIM), lambda h, i: (0, 0))
    ya = pl.pallas_call(
        functools.partial(_attn_a_kernel, n_chunks=nlt, tail_k_start=n, tail_chunk=nlt, tail_len=l_ctx,
                          lam_init=lam_init, aliased=False),
        grid=(A_HEADS, nlt),
        in_specs=[
            lam_spec,
            pl.BlockSpec((1, LANES, TM), lambda h, i: (i, h, 0)),
            pl.BlockSpec((rows, LANES), lambda h, i: (0, h)),
            pl.BlockSpec((nt, A_VROWS, TM), lambda h, i: (0, h, 0)),
        ],
        out_specs=pl.BlockSpec((TM, LANES), lambda h, i: (i, h)),
        out_shape=jax.ShapeDtypeStruct((rows, GROUP_WIDTH), BF16),
        scratch_shapes=scratch,
        compiler_params=_params(("parallel", "parallel")),
        name="attn_a",
    )(lam_vecs, aqt, ak, avt)
    if not need_ctx:
        return ya
    cb = n // l_ctx
    return pl.pallas_call(
        functools.partial(_attn_a_kernel, n_chunks=0, tail_k_start=0, tail_chunk=0, tail_len=l_ctx,
                          lam_init=lam_init, aliased=True),
        grid=(A_HEADS, 1),
        in_specs=[
            lam_spec,
            pl.BlockSpec((1, LANES, TM), lambda h, i: (nlt, h, 0)),
            pl.BlockSpec((l_ctx, LANES), lambda h, i: (cb, h)),
            pl.BlockSpec((1, A_VROWS, TM), lambda h, i: (nlt, h, 0)),
            pl.BlockSpec(memory_space=pl.ANY),
        ],
        out_specs=pl.BlockSpec((TM, LANES), lambda h, i: (nlt, h)),
        out_shape=jax.ShapeDtypeStruct((rows, GROUP_WIDTH), BF16),
        scratch_shapes=scratch,
        input_output_aliases={4: 0},
        compiler_params=_params(("parallel", "parallel")),
        name="attn_a_ctx",
    )(lam_vecs, aqt, ak, avt, ya)


def _mixb_kernel(u_ref, v_ref, g_ref, b_ref, ws_ref, bs_ref, o_ref):
    u = jax.nn.gelu(u_ref[...])
    v = _ln(jax.nn.gelu(v_ref[...]), g_ref[...], b_ref[...]).astype(BF16)
    for c in range(u.shape[0] // CHUNK):
        rs = slice(c * CHUNK, (c + 1) * CHUNK)
        for g in range(B_GROUPS):
            cs = slice(g * LANES, (g + 1) * LANES)
            mixed = jnp.dot(ws_ref[g].astype(BF16), v[rs, cs], preferred_element_type=F32) + bs_ref[:, g:g + 1]
            o_ref[rs, cs] = (u[rs, cs] * mixed).astype(BF16)


def _mixb(bu, bv, gn_g, gn_b, w_s, b_s_t):
    rows = bu.shape[0]
    gw = GROUP_WIDTH
    tile = pl.BlockSpec((TM, gw), lambda i: (i, 0))
    return pl.pallas_call(
        _mixb_kernel,
        grid=(rows // TM,),
        in_specs=[
            tile, tile,
            pl.BlockSpec((1, gw), lambda i: (0, 0)),
            pl.BlockSpec((1, gw), lambda i: (0, 0)),
            pl.BlockSpec((B_GROUPS, CHUNK, CHUNK), lambda i: (0, 0, 0)),
            pl.BlockSpec((CHUNK, B_GROUPS), lambda i: (0, 0)),
        ],
        out_specs=tile,
        out_shape=jax.ShapeDtypeStruct((rows, gw), BF16),
        compiler_params=_params(("parallel",)),
        name="mix_b",
    )(bu, bv, gn_g, gn_b, w_s, b_s_t)


def _softmax_pv(scores, values, sink=None):
    m = jnp.max(scores[0], axis=1, keepdims=True)
    for s in scores[1:]:
        m = jnp.maximum(m, jnp.max(s, axis=1, keepdims=True))
    if sink is not None:
        m = jnp.maximum(m, sink)
    den = None
    num = None
    for s, v in zip(scores, values):
        p = jnp.exp(s - m)
        ps = jnp.sum(p, axis=1, keepdims=True)
        den = ps if den is None else den + ps
        pv = jnp.dot(p.astype(BF16), v, preferred_element_type=F32)
        num = pv if num is None else num + pv
    if sink is not None:
        den = den + jnp.exp(sink - m)
    return num / den


def _attn_c_kernel(sink_ref, q_ref, k_ref, v_ref, *rest, n, l_ctx, local, aliased):
    o_ref = rest[-1]
    i = pl.program_id(0)
    n_sub = q_ref.shape[0] // QBLOCK
    wlen = 3 * QBLOCK
    ctx_start = n if local else 0
    kc = k_ref[pl.ds(ctx_start, l_ctx), :]
    vc = v_ref[pl.ds(ctx_start, l_ctx), :]
    kfirst, ksecond = _half_masks(vc.shape)
    vcs = (jnp.where(kfirst, vc, jnp.zeros_like(vc)), jnp.where(ksecond, vc, jnp.zeros_like(vc)))

    def sub_block(b, carry):
        r0 = pl.multiple_of(b * QBLOCK, QBLOCK)
        if local:
            blk = i * n_sub + b
            start = pl.multiple_of(jnp.clip((blk - 1) * QBLOCK, 0, n - wlen), QBLOCK)
            kl = k_ref[pl.ds(start, wlen), :]
            vl = v_ref[pl.ds(start, wlen), :]
            lfirst, lsecond = _half_masks(vl.shape)
            vls = (jnp.where(lfirst, vl, jnp.zeros_like(vl)), jnp.where(lsecond, vl, jnp.zeros_like(vl)))
            qpos = blk * QBLOCK + lax.broadcasted_iota(jnp.int32, (QBLOCK, wlen), 0)
            kpos = start + lax.broadcasted_iota(jnp.int32, (QBLOCK, wlen), 1)
            valid = jnp.abs(kpos - qpos) <= WINDOW
        for c in range(C_HEADS // 2):
            qb = q_ref[pl.ds(r0, QBLOCK), c * LANES:(c + 1) * LANES]
            qfirst, qsecond = _half_masks(qb.shape)
            out = None
            for half in range(2):
                qm = jnp.where(qfirst if half == 0 else qsecond, qb, jnp.zeros_like(qb))
                sink = sink_ref[half * (C_HEADS // 2) + c]
                scores = [_dot_nt(qm, kc)]
                values = [vcs[half]]
                if local:
                    scores.append(jnp.where(valid, _dot_nt(qm, kl), NEG_INF))
                    values.append(vls[half])
                o = _softmax_pv(scores, values, sink)
                out = o if out is None else out + o
            o_ref[pl.ds(r0, QBLOCK), c * LANES:(c + 1) * LANES] = out.astype(BF16)
        return carry

    lax.fori_loop(0, n_sub, sub_block, 0)


def _attn_c(sink, cq, ckv, n, l_ctx, need_ctx):
    rows = cq.shape[0]
    gw = GROUP_WIDTH
    smem = pl.BlockSpec(memory_space=pltpu.SMEM)
    yc = pl.pallas_call(
        functools.partial(_attn_c_kernel, n=n, l_ctx=l_ctx, local=True, aliased=False),
        grid=(n // TM,),
        in_specs=[
            smem,
            pl.BlockSpec((TM, gw), lambda i: (i, 0)),
            pl.BlockSpec((rows, LANES), lambda i: (0, 0)),
            pl.BlockSpec((rows, LANES), lambda i: (0, 1)),
        ],
        out_specs=pl.BlockSpec((TM, gw), lambda i: (i, 0)),
        out_shape=jax.ShapeDtypeStruct((rows, gw), BF16),
        compiler_params=_params(("parallel",)),
        name="attn_c",
    )(sink, cq, ckv, ckv)
    if not need_ctx:
        return yc
    ct = n // TM
    cb = n // l_ctx
    return pl.pallas_call(
        functools.partial(_attn_c_kernel, n=n, l_ctx=l_ctx, local=False, aliased=True),
        grid=(1,),
        in_specs=[
            smem,
            pl.BlockSpec((TM, gw), lambda i: (ct, 0)),
            pl.BlockSpec((l_ctx, LANES), lambda i: (cb, 0)),
            pl.BlockSpec((l_ctx, LANES), lambda i: (cb, 1)),
            pl.BlockSpec(memory_space=pl.ANY),
        ],
        out_specs=pl.BlockSpec((TM, gw), lambda i: (ct, 0)),
        out_shape=jax.ShapeDtypeStruct((rows, gw), BF16),
        input_output_aliases={4: 0},
        compiler_params=_params(("parallel",)),
        name="attn_c_ctx",
    )(sink, cq, ckv, ckv, yc)


def _attn_d_kernel(q_ref, kc_ref, vc_ref, *rest, local, aliased):
    o_ref = rest[-1]
    if local:
        k_ref, v_ref, b_ref = rest[0], rest[1], rest[2]
    for c in range(D_HEADS // 2):
        cs = slice(c * LANES, (c + 1) * LANES)
        qb = q_ref[:, cs]
        qfirst, qsecond = _half_masks(qb.shape)
        kcb = kc_ref[:, cs]
        vcb = vc_ref[:, cs]
        cfirst, csecond = _half_masks(vcb.shape)
        if local:
            kb = k_ref[:, cs]
            vb = v_ref[:, cs]
            nfirst, nsecond = _half_masks(vb.shape)
        out = None
        for half in range(2):
            qm = jnp.where(qfirst if half == 0 else qsecond, qb, jnp.zeros_like(qb))
            scores = [_dot_nt(qm, kcb)]
            values = [jnp.where(cfirst if half == 0 else csecond, vcb, jnp.zeros_like(vcb))]
            if local:
                scores.append(_dot_nt(qm, kb) + b_ref[0, 2 * c + half])
                values.append(jnp.where(nfirst if half == 0 else nsecond, vb, jnp.zeros_like(vb)))
            o = _softmax_pv(scores, values)
            out = o if out is None else out + o
        o_ref[:, cs] = out.astype(BF16)


def _attn_d(dq, dk, dv, bias, n, l_ctx, need_ctx):
    rows = dq.shape[0]
    gw = GROUP_WIDTH
    n_rows = n // GRID_W
    nk = NA_ROWS * GRID_W
    cb = n // l_ctx

    def row_start(r):
        return jnp.clip(r - NA_ROWS // 2, 0, n_rows - NA_ROWS)

    yd = pl.pallas_call(
        functools.partial(_attn_d_kernel, local=True, aliased=False),
        grid=(n_rows,),
        in_specs=[
            pl.BlockSpec((GRID_W, gw), lambda r: (r, 0)),
            pl.BlockSpec((l_ctx, gw), lambda r: (cb, 0)),
            pl.BlockSpec((l_ctx, gw), lambda r: (cb, 0)),
            pl.BlockSpec((pl.Element(nk), pl.Element(gw)), lambda r: (row_start(r) * GRID_W, 0)),
            pl.BlockSpec((pl.Element(nk), pl.Element(gw)), lambda r: (row_start(r) * GRID_W, 0)),
            pl.BlockSpec((1, D_HEADS, GRID_W, nk), lambda r: (r - row_start(r), 0, 0, 0)),
        ],
        out_specs=pl.BlockSpec((GRID_W, gw), lambda r: (r, 0)),
        out_shape=jax.ShapeDtypeStruct((rows, gw), BF16),
        compiler_params=_params(("parallel",)),
        name="attn_d",
    )(dq, dk, dv, dk, dv, bias)
    if not need_ctx:
        return yd
    ct = n // TM
    return pl.pallas_call(
        functools.partial(_attn_d_kernel, local=False, aliased=True),
        grid=(1,),
        in_specs=[
            pl.BlockSpec((TM, gw), lambda i: (ct, 0)),
            pl.BlockSpec((l_ctx, gw), lambda i: (cb, 0)),
            pl.BlockSpec((l_ctx, gw), lambda i: (cb, 0)),
            pl.BlockSpec(memory_space=pl.ANY),
        ],
        out_specs=pl.BlockSpec((TM, gw), lambda i: (ct, 0)),
        out_shape=jax.ShapeDtypeStruct((rows, gw), BF16),
        input_output_aliases={3: 0},
        compiler_params=_params(("parallel",)),
        name="attn_d_ctx",
    )(dq, dk, dv, yd)


def _outproj_kernel(ya_ref, yb_ref, yc_ref, yd_ref, w_ref, x_ref, gt_ref, lng_ref, lnb_ref, o_ref):
    y_in = jnp.concatenate([ya_ref[...], yb_ref[...], yc_ref[...], yd_ref[...]], axis=1)
    y = jnp.dot(y_in, w_ref[...], preferred_element_type=F32)
    t = DEEPNORM_ALPHA * x_ref[...] + gt_ref[0] * y
    o_ref[...] = _ln(t, lng_ref[...], lnb_ref[...])


def _outproj(ya, yb, yc, yd, w_out, x, mods_l, gate_k, lng, lnb, n_tiles, nlt):
    d = D_MODEL
    gw = GROUP_WIDTH
    ytile = pl.BlockSpec((TM, gw), lambda i: (i, 0))
    vec = pl.BlockSpec((1, d), lambda i: (0, 0))
    return pl.pallas_call(
        _outproj_kernel,
        grid=(n_tiles,),
        in_specs=[
            ytile, ytile, ytile, ytile,
            pl.BlockSpec((d, d), lambda i: (0, 0)),
            pl.BlockSpec((TM, d), lambda i: (i, 0)),
            pl.BlockSpec((1, 1, d), lambda i: (jnp.where(i >= nlt, N_MOD, 0) + gate_k, 0, 0)),
            vec, vec,
        ],
        out_specs=pl.BlockSpec((TM, d), lambda i: (i, 0)),
        out_shape=jax.ShapeDtypeStruct((n_tiles * TM, d), F32),
        compiler_params=_params(("parallel",)),
        name="outproj",
    )(ya, yb, yc, yd, w_out, x, mods_l, lng, lnb)


def _rope_tables(n, rows):
    t = jnp.arange(n, dtype=jnp.int32)
    row = (t // GRID_W).astype(F32)
    col = (t % GRID_W).astype(F32)
    n_freq = HEAD_DIM // 4
    inv = ROPE_THETA ** (-jnp.arange(n_freq, dtype=F32) / n_freq)
    ang = jnp.concatenate([row[:, None] * inv, col[:, None] * inv], axis=-1)
    cos, sin = jnp.cos(ang), jnp.sin(ang)
    cos_t = jnp.tile(jnp.concatenate([cos, cos], axis=-1), (1, LANES // HEAD_DIM))
    sin_t = jnp.tile(jnp.concatenate([-sin, sin], axis=-1), (1, LANES // HEAD_DIM))
    pad = rows - n
    cos_t = jnp.concatenate([cos_t, jnp.ones((pad, LANES), F32)], axis=0)
    sin_t = jnp.concatenate([sin_t, jnp.zeros((pad, LANES), F32)], axis=0)
    return cos_t, sin_t


def _c_head_perm():
    cols = []
    for c in range(C_HEADS // 2):
        for head in (c, c + C_HEADS // 2):
            cols.extend(range(head * HEAD_DIM, (head + 1) * HEAD_DIM))
    return jnp.asarray(cols, dtype=jnp.int32)


def _permute_w_in(w):
    gw = GROUP_WIDTH
    kvw = C_KV_HEADS * HEAD_DIM
    cuts = [gw, gw, gw, gw, gw, gw, kvw, kvw, gw, gw, gw]
    offs = [0]
    for c in cuts:
        offs.append(offs[-1] + c)
    aq, ak, av, bu, bv, cq, ck, cv, dq, dk, dv = [w[:, offs[i]:offs[i + 1]] for i in range(11)]
    cq = cq[:, _c_head_perm()]
    pad = jnp.zeros((w.shape[0], SEG - 2 * kvw), w.dtype)
    return jnp.concatenate([aq, ak, av, bu, bv, cq, ck, cv, pad, dq, dk, dv], axis=1).astype(BF16)


def _permute_w_out(w):
    gw = GROUP_WIDTH
    wc = w[2 * gw:3 * gw][_c_head_perm()]
    return jnp.concatenate([w[:2 * gw], wc, w[3 * gw:]], axis=0).astype(BF16)


def _na_bias(rpb):
    w = GRID_W
    cq = jnp.arange(w)
    cs = jnp.clip(cq - NA_COLS // 2, 0, w - NA_COLS)
    col_ok = (cq[None, :] >= cs[:, None]) & (cq[None, :] < cs[:, None] + NA_COLS)
    edge = w - NA_COLS
    ext = jnp.pad(rpb.astype(F32), ((0, 0), (0, 0), (edge, edge)), mode="edge")
    toep = jnp.stack([ext[:, :, w - 1 - q:2 * w - 1 - q] for q in range(w)], axis=2)
    toep = jnp.where(col_ok[None, None], toep, NEG_INF)
    per_shift = []
    for shift in range(NA_ROWS):
        sl = toep[:, NA_ROWS - 1 - shift:2 * NA_ROWS - 1 - shift]
        per_shift.append(jnp.swapaxes(sl, 1, 2).reshape(rpb.shape[0], w, NA_ROWS * w))
    return jnp.stack(per_shift, axis=0)


def kernel(x, c, ctx, c_ctx, w_mod, b_mod, ln_g, ln_b, ffn1_w_in, ffn1_w_out, ffn2_w_in, ffn2_w_out, mix_w_in, mix_w_out, a_lambda, b_norm_g, b_norm_b, b_spatial_w, b_spatial_b, c_sink, d_rpb):
    depth = w_mod.shape[0]
    n = x.shape[1]
    l_ctx = ctx.shape[1]
    d = D_MODEL
    assert x.shape[0] == 1 and n % TM == 0 and n // GRID_W >= NA_ROWS and n % 1024 == 0
    assert l_ctx % CHUNK == 0 and l_ctx <= TM and n % l_ctx == 0
    nlt = n // TM
    rows = n + TM

    xs = jnp.concatenate([x[0], ctx[0], jnp.zeros((rows - n - l_ctx, d), F32)], axis=0)
    cc = jnp.concatenate([c, c_ctx[None], jnp.zeros((6, d), F32)], axis=0)
    mods = _mod_vectors(cc, w_mod, b_mod)[:, :2].reshape(depth, 2 * N_MOD, 1, d)
    cos_t, sin_t = _rope_tables(n, rows)

    for l in range(depth):
        last = l == depth - 1
        lam_init = 0.8 - 0.6 * math.exp(-0.3 * l)
        mods_l = mods[l]
        lng = [ln_g[l, k][None] for k in range(3)]
        lnb = [ln_b[l, k][None] for k in range(3)]

        xs, h = _ffn(xs, mods_l, (0, 1, 2), lng[0], lnb[0], ffn1_w_in[l].astype(BF16), ffn1_w_out[l].astype(BF16),
                     nlt + 1, nlt, emit_ks=(3, 4))
        aq, ak, av, bu, bv, cq, ckv, dq, dk, dv = _inproj(h, _permute_w_in(mix_w_in[l]), cos_t, sin_t)
        ya = _attn_a(a_lambda[l], aq, ak, av, n, l_ctx, lam_init, not last)
        yb = _mixb(bu, bv, b_norm_g[l][None], b_norm_b[l][None], b_spatial_w[l], b_spatial_b[l].T)
        yc = _attn_c(c_sink[l], cq, ckv, n, l_ctx, not last)
        yd = _attn_d(dq, dk, dv, _na_bias(d_rpb[l]), n, l_ctx, not last)
        n_tiles = nlt if last else nlt + 1
        xs = _outproj(ya, yb, yc, yd, _permute_w_out(mix_w_out[l]), xs, mods_l, 5, lng[1], lnb[1], n_tiles, nlt)
        xs = _ffn(xs, mods_l, (6, 7, 8), lng[2], lnb[2], ffn2_w_in[l].astype(BF16), ffn2_w_out[l].astype(BF16),
                  n_tiles, nlt)
    return xs[None]
```

```python
import functools
import math

import jax
import jax.numpy as jnp
from jax import lax
from jax.experimental import pallas as pl
from jax.experimental.pallas import tpu as pltpu

F32 = jnp.float32
BF16 = jnp.bfloat16

D_MODEL = 2048
N_GROUPS = 4
GROUP_WIDTH = D_MODEL // N_GROUPS
HEAD_DIM = 64
GRID_W = 64
CHUNK = 128
B_GROUPS = GROUP_WIDTH // 128
A_HEADS = GROUP_WIDTH // (2 * HEAD_DIM)
C_HEADS = GROUP_WIDTH // HEAD_DIM
C_KV_HEADS = C_HEADS // 4
D_HEADS = GROUP_WIDTH // HEAD_DIM
WINDOW = 128
QBLOCK = 128
NA_ROWS = 8
NA_COLS = 16
D_FF = 256 * math.ceil(8 * D_MODEL / 3 / 256)
N_MOD = 9
ROPE_THETA = 10000.0
LN_EPS = 1e-6
NEG_INF = -1e30
MODEL_DEPTH = 4
DEEPNORM_ALPHA = (2 * MODEL_DEPTH) ** 0.25
QK_SCALE = HEAD_DIM ** -0.5
LOG2E = math.log2(math.e)
A_VROWS = 2 * HEAD_DIM + 16
A_UNROLL = 8

LANES = 128
TM = 512
TF = 512
SEG = 512
N_SEG = 10
VMEM_LIMIT = 56 * 1024 * 1024


def _params(sem):
    return pltpu.CompilerParams(dimension_semantics=sem, vmem_limit_bytes=VMEM_LIMIT)


def _ln(t, g, b):
    mu = jnp.mean(t, axis=-1, keepdims=True)
    tc = t - mu
    var = jnp.mean(tc * tc, axis=-1, keepdims=True)
    return tc * lax.rsqrt(var + LN_EPS) * g + b


def _dot_nt(a, b):
    return lax.dot_general(a, b, (((1,), (1,)), ((), ())), preferred_element_type=F32)


def _half_masks(shape):
    lane = lax.broadcasted_iota(jnp.int32, shape, 1)
    return lane < HEAD_DIM, lane >= HEAD_DIM


def _mod_kernel(cc_ref, w_ref, b_ref, o_ref):
    a = cc_ref[...]
    s = (a * jax.nn.sigmoid(a)).astype(BF16)
    o_ref[0] = jnp.dot(s, w_ref[0].astype(BF16), preferred_element_type=F32) + b_ref[0]


def _mod_vectors(cc, w_mod, b_mod):
    depth, d, nm = w_mod.shape
    tn = 1024
    return pl.pallas_call(
        _mod_kernel,
        grid=(depth, nm // tn),
        in_specs=[
            pl.BlockSpec((8, d), lambda l, j: (0, 0)),
            pl.BlockSpec((1, d, tn), lambda l, j: (l, 0, j)),
            pl.BlockSpec((1, 1, tn), lambda l, j: (l, 0, j)),
        ],
        out_specs=pl.BlockSpec((1, 8, tn), lambda l, j: (l, 0, j)),
        out_shape=jax.ShapeDtypeStruct((depth, 8, nm), F32),
        compiler_params=_params(("parallel", "parallel")),
        name="mod_vectors",
    )(cc, w_mod, b_mod.reshape(depth, 1, nm))


def _ffn_kernel(x_ref, sh_ref, sc_ref, gt_ref, lng_ref, lnb_ref, wg_ref, wu_ref, wo_ref, *rest, emit_h):
    if emit_h:
        sh2_ref, sc2_ref, o_ref, h_ref, xin_scr, acc_scr = rest
    else:
        o_ref, xin_scr, acc_scr = rest
    j = pl.program_id(1)

    @pl.when(j == 0)
    def _():
        xin_scr[...] = (x_ref[...] * (1.0 + sc_ref[0]) + sh_ref[0]).astype(BF16)
        acc_scr[...] = jnp.zeros_like(acc_scr)

    xin = xin_scr[...]
    g = jnp.dot(xin, wg_ref[...], preferred_element_type=F32)
    u = jnp.dot(xin, wu_ref[...], preferred_element_type=F32)
    hh = ((g * jax.nn.sigmoid(g)) * u).astype(BF16)
    acc_scr[...] += jnp.dot(hh, wo_ref[...], preferred_element_type=F32)

    @pl.when(j == pl.num_programs(1) - 1)
    def _():
        t = DEEPNORM_ALPHA * x_ref[...] + (0.5 * gt_ref[0]) * acc_scr[...]
        y = _ln(t, lng_ref[...], lnb_ref[...])
        o_ref[...] = y
        if emit_h:
            h_ref[...] = (y * (1.0 + sc2_ref[0]) + sh2_ref[0]).astype(BF16)


def _ffn(x, mods_l, ks, lng, lnb, w_in, w_out, n_tiles, nlt, emit_ks=None):
    d = D_MODEL
    nj = D_FF // TF
    emit_h = emit_ks is not None

    def mod_spec(k):
        return pl.BlockSpec((1, 1, d), lambda i, j, k=k: (jnp.where(i >= nlt, N_MOD, 0) + k, 0, 0))

    vec_spec = pl.BlockSpec((1, d), lambda i, j: (0, 0))
    in_specs = [
        pl.BlockSpec((TM, d), lambda i, j: (i, 0)),
        mod_spec(ks[0]), mod_spec(ks[1]), mod_spec(ks[2]),
        vec_spec, vec_spec,
        pl.BlockSpec((d, TF), lambda i, j: (0, j)),
        pl.BlockSpec((d, TF), lambda i, j: (0, nj + j)),
        pl.BlockSpec((TF, d), lambda i, j: (j, 0)),
    ]
    args = [x, mods_l, mods_l, mods_l, lng, lnb, w_in, w_in, w_out]
    out_specs = pl.BlockSpec((TM, d), lambda i, j: (i, 0))
    out_shape = jax.ShapeDtypeStruct((n_tiles * TM, d), F32)
    if emit_h:
        in_specs += [mod_spec(emit_ks[0]), mod_spec(emit_ks[1])]
        args += [mods_l, mods_l]
        out_specs = [out_specs, pl.BlockSpec((TM, d), lambda i, j: (i, 0))]
        out_shape = [out_shape, jax.ShapeDtypeStruct((n_tiles * TM, d), BF16)]
    return pl.pallas_call(
        functools.partial(_ffn_kernel, emit_h=emit_h),
        grid=(n_tiles, nj),
        in_specs=in_specs,
        out_specs=out_specs,
        out_shape=out_shape,
        scratch_shapes=[pltpu.VMEM((TM, d), BF16), pltpu.VMEM((TM, d), F32)],
        compiler_params=_params(("parallel", "arbitrary")),
        name="ffn_emit" if emit_h else "ffn",
    )(*args)


def _rope(t, cos_ref, sin_ref):
    w = t.shape[1]
    lane = lax.broadcasted_iota(jnp.int32, t.shape, 1)
    first = (lane % HEAD_DIM) < (HEAD_DIM // 2)
    rot = jnp.where(first, pltpu.roll(t, w - HEAD_DIM // 2, 1), pltpu.roll(t, HEAD_DIM // 2, 1))
    reps = w // LANES
    cos = jnp.concatenate([cos_ref[...]] * reps, axis=1) if reps > 1 else cos_ref[...]
    sin = jnp.concatenate([sin_ref[...]] * reps, axis=1) if reps > 1 else sin_ref[...]
    return t * cos + rot * sin


_SEG_AQ, _SEG_AK, _SEG_AV, _SEG_BU, _SEG_BV, _SEG_CQ, _SEG_CKV, _SEG_DQ, _SEG_DK, _SEG_DV = range(N_SEG)


def _inproj_kernel(h_ref, w_ref, cos_ref, sin_ref, *out_refs):
    j = pl.program_id(1)
    for k, o_ref in enumerate(out_refs):

        @pl.when(j == k)
        def _(k=k, o_ref=o_ref):
            t = jnp.dot(h_ref[...], w_ref[...], preferred_element_type=F32)
            if k in (_SEG_AQ, _SEG_AK, _SEG_CQ):
                t = _rope(t, cos_ref, sin_ref)
            if k == _SEG_AQ:
                t = t * (QK_SCALE * LOG2E)
            if k in (_SEG_CQ, _SEG_DQ):
                t = t * QK_SCALE
            if k == _SEG_CKV:
                t = jnp.concatenate([_rope(t[:, :LANES], cos_ref, sin_ref), t[:, LANES:2 * LANES]], axis=1)
            if k == _SEG_AQ:
                o_ref[0] = t.T.astype(o_ref.dtype)
            elif k == _SEG_AV:
                tt = t.T.astype(o_ref.dtype)
                extra = A_VROWS - 2 * HEAD_DIM
                ones_row = (lax.broadcasted_iota(jnp.int32, (extra, tt.shape[1]), 0) == 0).astype(o_ref.dtype)
                for hd in range(A_HEADS):
                    o_ref[0, hd * A_VROWS:hd * A_VROWS + 2 * HEAD_DIM] = tt[hd * 2 * HEAD_DIM:(hd + 1) * 2 * HEAD_DIM]
                    o_ref[0, hd * A_VROWS + 2 * HEAD_DIM:(hd + 1) * A_VROWS] = ones_row
            else:
                o_ref[...] = t.astype(o_ref.dtype)


def _inproj(h, w_perm, cos_t, sin_t):
    rows = h.shape[0]
    n_tiles = rows // TM
    d = D_MODEL
    widths = [SEG] * N_SEG
    widths[_SEG_CKV] = 2 * LANES
    dtypes = [BF16] * N_SEG
    dtypes[_SEG_BU] = F32
    dtypes[_SEG_BV] = F32
    out_specs = [pl.BlockSpec((TM, w), lambda i, j: (i, 0)) for w in widths]
    out_shape = [jax.ShapeDtypeStruct((rows, w), dt) for w, dt in zip(widths, dtypes)]
    for k in (_SEG_AQ, _SEG_AV):
        slab = SEG if k == _SEG_AQ else A_HEADS * A_VROWS
        out_specs[k] = pl.BlockSpec((1, slab, TM), lambda i, j: (i, 0, 0))
        out_shape[k] = jax.ShapeDtypeStruct((n_tiles, slab, TM), BF16)
    return pl.pallas_call(
        _inproj_kernel,
        grid=(n_tiles, N_SEG),
        in_specs=[
            pl.BlockSpec((TM, d), lambda i, j: (i, 0)),
            pl.BlockSpec((d, SEG), lambda i, j: (0, j)),
            pl.BlockSpec((TM, LANES), lambda i, j: (i, 0)),
            pl.BlockSpec((TM, LANES), lambda i, j: (i, 0)),
        ],
        out_specs=out_specs,
        out_shape=out_shape,
        compiler_params=_params(("parallel", "arbitrary")),
        name="inproj",
    )(h, w_perm, cos_t, sin_t)


def _attn_a_kernel(lam_ref, qt_ref, k_ref, vt_ref, *rest, n_chunks, tail_k_start, tail_chunk, tail_len, lam_init,
                   aliased):
    if aliased:
        rest = rest[1:]
    o_ref, m_scr, acc_scr, s_scr, mc_scr = rest
    qt = qt_ref[0]
    comp = lax.broadcasted_iota(jnp.int32, qt.shape, 0) < HEAD_DIM
    zero = jnp.zeros_like(qt)
    qts = (jnp.where(comp, qt, zero), jnp.where(comp, zero, qt))
    m_scr[...] = jnp.full(m_scr.shape, NEG_INF, F32)
    acc_scr[...] = jnp.zeros_like(acc_scr)

    def scores(k, slot, size):
        for c in range(2):
            s = jnp.dot(k, qts[c], preferred_element_type=F32)
            s_scr[slot, c, :size] = s
            mc_scr[slot, c] = jnp.max(s, axis=0, keepdims=True)

    def consume(slot, vt, size):
        for c in range(2):
            m_old = m_scr[c]
            m_new = jnp.maximum(m_old, mc_scr[slot, c])
            alpha = jnp.exp2(m_old - m_new)
            p = jnp.exp2(s_scr[slot, c, :size] - m_new).astype(BF16)
            acc_scr[c] = alpha * acc_scr[c] + jnp.dot(vt, p, preferred_element_type=F32)
            m_scr[c] = m_new

    def k_chunk(i):
        i = jnp.minimum(i, n_chunks - 1)
        return k_ref[pl.ds(pl.multiple_of(i * TM, TM), TM), :]

    if n_chunks:
        unroll = math.gcd(n_chunks, A_UNROLL)
        assert unroll % 2 == 0
        scores(k_chunk(0), 0, TM)

        def body(t, carry):
            j = unroll * t
            for u in range(unroll):
                scores(k_chunk(j + u + 1), (u + 1) % 2, TM)
                consume(u % 2, vt_ref[j + u], TM)
            return carry
        lax.fori_loop(0, n_chunks // unroll, body, 0)
    scores(k_ref[pl.ds(tail_k_start, tail_len), :], 0, tail_len)
    consume(0, vt_ref[tail_chunk][:, :tail_len], tail_len)

    lv = lam_ref[...]
    lam = (jnp.exp(jnp.sum(lv[0:1] * lv[1:2], axis=1, keepdims=True))
           - jnp.exp(jnp.sum(lv[2:3] * lv[3:4], axis=1, keepdims=True)) + lam_init)
    nv = 2 * HEAD_DIM
    ot = (acc_scr[0, :nv] / acc_scr[0, nv:nv + 1]
          - lam * (acc_scr[1, :nv] / acc_scr[1, nv:nv + 1]))
    ms = jnp.mean(ot * ot, axis=0, keepdims=True)
    ot = (ot * lax.rsqrt(ms + LN_EPS)) * (1.0 - lam_init)
    o_ref[...] = ot.T.astype(BF16)


def _attn_a(lam_vecs, aqt, ak, avt, n, l_ctx, lam_init, need_ctx):
    rows = ak.shape[0]
    nt = rows // TM
    nlt = n // TM
    scratch = [pltpu.VMEM((2, 1, TM), F32), pltpu.VMEM((2, A_VROWS, TM), F32),
               pltpu.VMEM((2, 2, TM, TM), F32), pltpu.VMEM((2, 2, 1, TM), F32)]
    lam_spec = pl.BlockSpec((4, HEAD_DIM), lambda h, i: (0, 0))
    ya = pl.pallas_call(
        functools.partial(_attn_a_kernel, n_chunks=nlt, tail_k_start=n, tail_chunk=nlt, tail_len=l_ctx,
                          lam_init=lam_init, aliased=False),
        grid=(A_HEADS, nlt),
        in_specs=[
            lam_spec,
            pl.BlockSpec((1, LANES, TM), lambda h, i: (i, h, 0)),
            pl.BlockSpec((rows, LANES), lambda h, i: (0, h)),
            pl.BlockSpec((nt, A_VROWS, TM), lambda h, i: (0, h, 0)),
        ],
        out_specs=pl.BlockSpec((TM, LANES), lambda h, i: (i, h)),
        out_shape=jax.ShapeDtypeStruct((rows, GROUP_WIDTH), BF16),
        scratch_shapes=scratch,
        compiler_params=_params(("parallel", "parallel")),
        name="attn_a",
    )(lam_vecs, aqt, ak, avt)
    if not need_ctx:
        return ya
    cb = n // l_ctx
    return pl.pallas_call(
        functools.partial(_attn_a_kernel, n_chunks=0, tail_k_start=0, tail_chunk=0, tail_len=l_ctx,
                          lam_init=lam_init, aliased=True),
        grid=(A_HEADS, 1),
        in_specs=[
            lam_spec,
            pl.BlockSpec((1, LANES, TM), lambda h, i: (nlt, h, 0)),
            pl.BlockSpec((l_ctx, LANES), lambda h, i: (cb, h)),
            pl.BlockSpec((1, A_VROWS, TM), lambda h, i: (nlt, h, 0)),
            pl.BlockSpec(memory_space=pl.ANY),
        ],
        out_specs=pl.BlockSpec((TM, LANES), lambda h, i: (nlt, h)),
        out_shape=jax.ShapeDtypeStruct((rows, GROUP_WIDTH), BF16),
        scratch_shapes=scratch,
        input_output_aliases={4: 0},
        compiler_params=_params(("parallel", "parallel")),
        name="attn_a_ctx",
    )(lam_vecs, aqt, ak, avt, ya)


def _mixb_kernel(u_ref, v_ref, g_ref, b_ref, ws_ref, bs_ref, o_ref):
    u = jax.nn.gelu(u_ref[...])
    v = _ln(jax.nn.gelu(v_ref[...]), g_ref[...], b_ref[...]).astype(BF16)
    for c in range(u.shape[0] // CHUNK):
        rs = slice(c * CHUNK, (c + 1) * CHUNK)
        for g in range(B_GROUPS):
            cs = slice(g * LANES, (g + 1) * LANES)
            mixed = jnp.dot(ws_ref[g].astype(BF16), v[rs, cs], preferred_element_type=F32) + bs_ref[:, g:g + 1]
            o_ref[rs, cs] = (u[rs, cs] * mixed).astype(BF16)


def _mixb(bu, bv, gn_g, gn_b, w_s, b_s_t):
    rows = bu.shape[0]
    gw = GROUP_WIDTH
    tile = pl.BlockSpec((TM, gw), lambda i: (i, 0))
    return pl.pallas_call(
        _mixb_kernel,
        grid=(rows // TM,),
        in_specs=[
            tile, tile,
            pl.BlockSpec((1, gw), lambda i: (0, 0)),
            pl.BlockSpec((1, gw), lambda i: (0, 0)),
            pl.BlockSpec((B_GROUPS, CHUNK, CHUNK), lambda i: (0, 0, 0)),
            pl.BlockSpec((CHUNK, B_GROUPS), lambda i: (0, 0)),
        ],
        out_specs=tile,
        out_shape=jax.ShapeDtypeStruct((rows, gw), BF16),
        compiler_params=_params(("parallel",)),
        name="mix_b",
    )(bu, bv, gn_g, gn_b, w_s, b_s_t)


def _paired_heads_attention(score_fn, values, n_pairs, sinks=None):
    all_scores = [score_fn(c, half) for c in range(n_pairs) for half in range(2)]
    all_probs = []
    for idx, scores in enumerate(all_scores):
        m = jnp.max(scores[0], axis=1, keepdims=True)
        for s in scores[1:]:
            m = jnp.maximum(m, jnp.max(s, axis=1, keepdims=True))
        if sinks is not None:
            m = jnp.maximum(m, sinks[idx])
        ps = [jnp.exp(s - m) for s in scores]
        den = jnp.sum(ps[0], axis=1, keepdims=True)
        for p in ps[1:]:
            den = den + jnp.sum(p, axis=1, keepdims=True)
        if sinks is not None:
            den = den + jnp.exp(sinks[idx] - m)
        all_probs.append(([p.astype(BF16) for p in ps], den))
    outs = []
    for c in range(n_pairs):
        out = None
        for half in range(2):
            ps, den = all_probs[2 * c + half]
            num = None
            for p, v in zip(ps, values[c]):
                first, second = _half_masks(v.shape)
                pv = jnp.dot(p, jnp.where(first if half == 0 else second, v, jnp.zeros_like(v)),
                             preferred_element_type=F32)
                num = pv if num is None else num + pv
            o = num / den
            out = o if out is None else out + o
        outs.append(out)
    return outs


def _attn_c_kernel(sink_ref, q_ref, k_ref, v_ref, *rest, n, l_ctx, local, aliased):
    o_ref = rest[-1]
    i = pl.program_id(0)
    n_sub = q_ref.shape[0] // QBLOCK
    wlen = 3 * QBLOCK
    ctx_start = n if local else 0
    n_pairs = C_HEADS // 2
    kc = k_ref[pl.ds(ctx_start, l_ctx), :]
    vc = v_ref[pl.ds(ctx_start, l_ctx), :]
    sinks = [sink_ref[half * n_pairs + c] for c in range(n_pairs) for half in range(2)]

    def sub_block(b, carry):
        r0 = pl.multiple_of(b * QBLOCK, QBLOCK)
        values = [vc]
        if local:
            blk = i * n_sub + b
            start = pl.multiple_of(jnp.clip((blk - 1) * QBLOCK, 0, n - wlen), QBLOCK)
            kl = k_ref[pl.ds(start, wlen), :]
            values.append(v_ref[pl.ds(start, wlen), :])
            qpos = blk * QBLOCK + lax.broadcasted_iota(jnp.int32, (QBLOCK, wlen), 0)
            kpos = start + lax.broadcasted_iota(jnp.int32, (QBLOCK, wlen), 1)
            valid = jnp.abs(kpos - qpos) <= WINDOW

        def score_fn(c, half):
            qb = q_ref[pl.ds(r0, QBLOCK), c * LANES:(c + 1) * LANES]
            first, second = _half_masks(qb.shape)
            qm = jnp.where(first if half == 0 else second, qb, jnp.zeros_like(qb))
            scores = [_dot_nt(qm, kc)]
            if local:
                scores.append(jnp.where(valid, _dot_nt(qm, kl), NEG_INF))
            return scores

        outs = _paired_heads_attention(score_fn, [values] * n_pairs, n_pairs, sinks)
        for c, out in enumerate(outs):
            o_ref[pl.ds(r0, QBLOCK), c * LANES:(c + 1) * LANES] = out.astype(BF16)
        return carry

    lax.fori_loop(0, n_sub, sub_block, 0)


def _attn_c(sink, cq, ckv, n, l_ctx, need_ctx):
    rows = cq.shape[0]
    gw = GROUP_WIDTH
    smem = pl.BlockSpec(memory_space=pltpu.SMEM)
    yc = pl.pallas_call(
        functools.partial(_attn_c_kernel, n=n, l_ctx=l_ctx, local=True, aliased=False),
        grid=(n // TM,),
        in_specs=[
            smem,
            pl.BlockSpec((TM, gw), lambda i: (i, 0)),
            pl.BlockSpec((rows, LANES), lambda i: (0, 0)),
            pl.BlockSpec((rows, LANES), lambda i: (0, 1)),
        ],
        out_specs=pl.BlockSpec((TM, gw), lambda i: (i, 0)),
        out_shape=jax.ShapeDtypeStruct((rows, gw), BF16),
        compiler_params=_params(("parallel",)),
        name="attn_c",
    )(sink, cq, ckv, ckv)
    if not need_ctx:
        return yc
    ct = n // TM
    cb = n // l_ctx
    return pl.pallas_call(
        functools.partial(_attn_c_kernel, n=n, l_ctx=l_ctx, local=False, aliased=True),
        grid=(1,),
        in_specs=[
            smem,
            pl.BlockSpec((TM, gw), lambda i: (ct, 0)),
            pl.BlockSpec((l_ctx, LANES), lambda i: (cb, 0)),
            pl.BlockSpec((l_ctx, LANES), lambda i: (cb, 1)),
            pl.BlockSpec(memory_space=pl.ANY),
        ],
        out_specs=pl.BlockSpec((TM, gw), lambda i: (ct, 0)),
        out_shape=jax.ShapeDtypeStruct((rows, gw), BF16),
        input_output_aliases={4: 0},
        compiler_params=_params(("parallel",)),
        name="attn_c_ctx",
    )(sink, cq, ckv, ckv, yc)


def _attn_d_kernel(q_ref, kc_ref, vc_ref, *rest, local, aliased):
    o_ref = rest[-1]
    if local:
        k_ref, v_ref, b_ref = rest[0], rest[1], rest[2]
    n_pairs = D_HEADS // 2
    lane_block = lambda c: slice(c * LANES, (c + 1) * LANES)

    def score_fn(c, half):
        qb = q_ref[:, lane_block(c)]
        first, second = _half_masks(qb.shape)
        qm = jnp.where(first if half == 0 else second, qb, jnp.zeros_like(qb))
        scores = [_dot_nt(qm, kc_ref[:, lane_block(c)])]
        if local:
            scores.append(_dot_nt(qm, k_ref[:, lane_block(c)]) + b_ref[0, 2 * c + half])
        return scores

    values = []
    for c in range(n_pairs):
        vals = [vc_ref[:, lane_block(c)]]
        if local:
            vals.append(v_ref[:, lane_block(c)])
        values.append(vals)
    outs = _paired_heads_attention(score_fn, values, n_pairs)
    for c, out in enumerate(outs):
        o_ref[:, lane_block(c)] = out.astype(BF16)


def _attn_d(dq, dk, dv, bias, n, l_ctx, need_ctx):
    rows = dq.shape[0]
    gw = GROUP_WIDTH
    n_rows = n // GRID_W
    nk = NA_ROWS * GRID_W
    cb = n // l_ctx

    def row_start(r):
        return jnp.clip(r - NA_ROWS // 2, 0, n_rows - NA_ROWS)

    yd = pl.pallas_call(
        functools.partial(_attn_d_kernel, local=True, aliased=False),
        grid=(n_rows,),
        in_specs=[
            pl.BlockSpec((GRID_W, gw), lambda r: (r, 0)),
            pl.BlockSpec((l_ctx, gw), lambda r: (cb, 0)),
            pl.BlockSpec((l_ctx, gw), lambda r: (cb, 0)),
            pl.BlockSpec((pl.Element(nk), pl.Element(gw)), lambda r: (row_start(r) * GRID_W, 0)),
            pl.BlockSpec((pl.Element(nk), pl.Element(gw)), lambda r: (row_start(r) * GRID_W, 0)),
            pl.BlockSpec((1, D_HEADS, GRID_W, nk), lambda r: (r - row_start(r), 0, 0, 0)),
        ],
        out_specs=pl.BlockSpec((GRID_W, gw), lambda r: (r, 0)),
        out_shape=jax.ShapeDtypeStruct((rows, gw), BF16),
        compiler_params=_params(("parallel",)),
        name="attn_d",
    )(dq, dk, dv, dk, dv, bias)
    if not need_ctx:
        return yd
    ct = n // TM
    return pl.pallas_call(
        functools.partial(_attn_d_kernel, local=False, aliased=True),
        grid=(1,),
        in_specs=[
            pl.BlockSpec((TM, gw), lambda i: (ct, 0)),
            pl.BlockSpec((l_ctx, gw), lambda i: (cb, 0)),
            pl.BlockSpec((l_ctx, gw), lambda i: (cb, 0)),
            pl.BlockSpec(memory_space=pl.ANY),
        ],
        out_specs=pl.BlockSpec((TM, gw), lambda i: (ct, 0)),
        out_shape=jax.ShapeDtypeStruct((rows, gw), BF16),
        input_output_aliases={3: 0},
        compiler_params=_params(("parallel",)),
        name="attn_d_ctx",
    )(dq, dk, dv, yd)


def _outproj_kernel(ya_ref, yb_ref, yc_ref, yd_ref, w_ref, x_ref, gt_ref, lng_ref, lnb_ref, o_ref):
    y_in = jnp.concatenate([ya_ref[...], yb_ref[...], yc_ref[...], yd_ref[...]], axis=1)
    y = jnp.dot(y_in, w_ref[...], preferred_element_type=F32)
    t = DEEPNORM_ALPHA * x_ref[...] + gt_ref[0] * y
    o_ref[...] = _ln(t, lng_ref[...], lnb_ref[...])


def _outproj(ya, yb, yc, yd, w_out, x, mods_l, gate_k, lng, lnb, n_tiles, nlt):
    d = D_MODEL
    gw = GROUP_WIDTH
    ytile = pl.BlockSpec((TM, gw), lambda i: (i, 0))
    vec = pl.BlockSpec((1, d), lambda i: (0, 0))
    return pl.pallas_call(
        _outproj_kernel,
        grid=(n_tiles,),
        in_specs=[
            ytile, ytile, ytile, ytile,
            pl.BlockSpec((d, d), lambda i: (0, 0)),
            pl.BlockSpec((TM, d), lambda i: (i, 0)),
            pl.BlockSpec((1, 1, d), lambda i: (jnp.where(i >= nlt, N_MOD, 0) + gate_k, 0, 0)),
            vec, vec,
        ],
        out_specs=pl.BlockSpec((TM, d), lambda i: (i, 0)),
        out_shape=jax.ShapeDtypeStruct((n_tiles * TM, d), F32),
        compiler_params=_params(("parallel",)),
        name="outproj",
    )(ya, yb, yc, yd, w_out, x, mods_l, lng, lnb)


def _rope_tables(n, rows):
    t = jnp.arange(n, dtype=jnp.int32)
    row = (t // GRID_W).astype(F32)
    col = (t % GRID_W).astype(F32)
    n_freq = HEAD_DIM // 4
    inv = ROPE_THETA ** (-jnp.arange(n_freq, dtype=F32) / n_freq)
    ang = jnp.concatenate([row[:, None] * inv, col[:, None] * inv], axis=-1)
    cos, sin = jnp.cos(ang), jnp.sin(ang)
    cos_t = jnp.tile(jnp.concatenate([cos, cos], axis=-1), (1, LANES // HEAD_DIM))
    sin_t = jnp.tile(jnp.concatenate([-sin, sin], axis=-1), (1, LANES // HEAD_DIM))
    pad = rows - n
    cos_t = jnp.concatenate([cos_t, jnp.ones((pad, LANES), F32)], axis=0)
    sin_t = jnp.concatenate([sin_t, jnp.zeros((pad, LANES), F32)], axis=0)
    return cos_t, sin_t


def _c_head_perm():
    cols = []
    for c in range(C_HEADS // 2):
        for head in (c, c + C_HEADS // 2):
            cols.extend(range(head * HEAD_DIM, (head + 1) * HEAD_DIM))
    return jnp.asarray(cols, dtype=jnp.int32)


def _permute_w_in(w):
    gw = GROUP_WIDTH
    kvw = C_KV_HEADS * HEAD_DIM
    cuts = [gw, gw, gw, gw, gw, gw, kvw, kvw, gw, gw, gw]
    offs = [0]
    for c in cuts:
        offs.append(offs[-1] + c)
    aq, ak, av, bu, bv, cq, ck, cv, dq, dk, dv = [w[:, offs[i]:offs[i + 1]] for i in range(11)]
    cq = cq[:, _c_head_perm()]
    pad = jnp.zeros((w.shape[0], SEG - 2 * kvw), w.dtype)
    return jnp.concatenate([aq, ak, av, bu, bv, cq, ck, cv, pad, dq, dk, dv], axis=1).astype(BF16)


def _permute_w_out(w):
    gw = GROUP_WIDTH
    wc = w[2 * gw:3 * gw][_c_head_perm()]
    return jnp.concatenate([w[:2 * gw], wc, w[3 * gw:]], axis=0).astype(BF16)


def _na_bias(rpb):
    w = GRID_W
    cq = jnp.arange(w)
    cs = jnp.clip(cq - NA_COLS // 2, 0, w - NA_COLS)
    col_ok = (cq[None, :] >= cs[:, None]) & (cq[None, :] < cs[:, None] + NA_COLS)
    edge = w - NA_COLS
    ext = jnp.pad(rpb.astype(F32), ((0, 0), (0, 0), (edge, edge)), mode="edge")
    toep = jnp.stack([ext[:, :, w - 1 - q:2 * w - 1 - q] for q in range(w)], axis=2)
    toep = jnp.where(col_ok[None, None], toep, NEG_INF)
    per_shift = []
    for shift in range(NA_ROWS):
        sl = toep[:, NA_ROWS - 1 - shift:2 * NA_ROWS - 1 - shift]
        per_shift.append(jnp.swapaxes(sl, 1, 2).reshape(rpb.shape[0], w, NA_ROWS * w))
    return jnp.stack(per_shift, axis=0)


def kernel(x, c, ctx, c_ctx, w_mod, b_mod, ln_g, ln_b, ffn1_w_in, ffn1_w_out, ffn2_w_in, ffn2_w_out, mix_w_in, mix_w_out, a_lambda, b_norm_g, b_norm_b, b_spatial_w, b_spatial_b, c_sink, d_rpb):
    depth = w_mod.shape[0]
    n = x.shape[1]
    l_ctx = ctx.shape[1]
    d = D_MODEL
    assert x.shape[0] == 1 and n % TM == 0 and n // GRID_W >= NA_ROWS and n % 1024 == 0
    assert l_ctx % CHUNK == 0 and l_ctx <= TM and n % l_ctx == 0
    nlt = n // TM
    rows = n + TM

    xs = jnp.concatenate([x[0], ctx[0], jnp.zeros((rows - n - l_ctx, d), F32)], axis=0)
    cc = jnp.concatenate([c, c_ctx[None], jnp.zeros((6, d), F32)], axis=0)
    mods = _mod_vectors(cc, w_mod, b_mod)[:, :2].reshape(depth, 2 * N_MOD, 1, d)
    cos_t, sin_t = _rope_tables(n, rows)

    for l in range(depth):
        last = l == depth - 1
        lam_init = 0.8 - 0.6 * math.exp(-0.3 * l)
        mods_l = mods[l]
        lng = [ln_g[l, k][None] for k in range(3)]
        lnb = [ln_b[l, k][None] for k in range(3)]

        xs, h = _ffn(xs, mods_l, (0, 1, 2), lng[0], lnb[0], ffn1_w_in[l].astype(BF16), ffn1_w_out[l].astype(BF16),
                     nlt + 1, nlt, emit_ks=(3, 4))
        aq, ak, av, bu, bv, cq, ckv, dq, dk, dv = _inproj(h, _permute_w_in(mix_w_in[l]), cos_t, sin_t)
        ya = _attn_a(a_lambda[l], aq, ak, av, n, l_ctx, lam_init, not last)
        yb = _mixb(bu, bv, b_norm_g[l][None], b_norm_b[l][None], b_spatial_w[l], b_spatial_b[l].T)
        yc = _attn_c(c_sink[l], cq, ckv, n, l_ctx, not last)
        yd = _attn_d(dq, dk, dv, _na_bias(d_rpb[l]), n, l_ctx, not last)
        n_tiles = nlt if last else nlt + 1
        xs = _outproj(ya, yb, yc, yd, _permute_w_out(mix_w_out[l]), xs, mods_l, 5, lng[1], lnb[1], n_tiles, nlt)
        xs = _ffn(xs, mods_l, (6, 7, 8), lng[2], lnb[2], ffn2_w_in[l].astype(BF16), ffn2_w_out[l].astype(BF16),
                  n_tiles, nlt)
    return xs[None]
```

```python
import functools
import math

import jax
import jax.numpy as jnp
from jax import lax
from jax.experimental import pallas as pl
from jax.experimental.pallas import tpu as pltpu

F32 = jnp.float32
BF16 = jnp.bfloat16

D_MODEL = 2048
N_GROUPS = 4
GROUP_WIDTH = D_MODEL // N_GROUPS
HEAD_DIM = 64
GRID_W = 64
CHUNK = 128
B_GROUPS = GROUP_WIDTH // 128
A_HEADS = GROUP_WIDTH // (2 * HEAD_DIM)
C_HEADS = GROUP_WIDTH // HEAD_DIM
C_KV_HEADS = C_HEADS // 4
D_HEADS = GROUP_WIDTH // HEAD_DIM
WINDOW = 128
QBLOCK = 128
NA_ROWS = 8
NA_COLS = 16
D_FF = 256 * math.ceil(8 * D_MODEL / 3 / 256)
N_MOD = 9
ROPE_THETA = 10000.0
LN_EPS = 1e-6
NEG_INF = -1e30
MODEL_DEPTH = 4
DEEPNORM_ALPHA = (2 * MODEL_DEPTH) ** 0.25
QK_SCALE = HEAD_DIM ** -0.5
LOG2E = math.log2(math.e)
A_VROWS = 2 * HEAD_DIM + 16
A_UNROLL = 8
D_ROWS = 4

LANES = 128
TM = 512
TF = 512
SEG = 512
N_SEG = 10
VMEM_LIMIT = 56 * 1024 * 1024


def _params(sem):
    return pltpu.CompilerParams(dimension_semantics=sem, vmem_limit_bytes=VMEM_LIMIT)


def _ln(t, g, b):
    mu = jnp.mean(t, axis=-1, keepdims=True)
    tc = t - mu
    var = jnp.mean(tc * tc, axis=-1, keepdims=True)
    return tc * lax.rsqrt(var + LN_EPS) * g + b


def _dot_nt(a, b):
    return lax.dot_general(a, b, (((1,), (1,)), ((), ())), preferred_element_type=F32)


def _half_masks(shape):
    lane = lax.broadcasted_iota(jnp.int32, shape, 1)
    return lane < HEAD_DIM, lane >= HEAD_DIM


def _mod_kernel(cc_ref, w_ref, b_ref, o_ref):
    a = cc_ref[...]
    s = (a * jax.nn.sigmoid(a)).astype(BF16)
    o_ref[0] = jnp.dot(s, w_ref[0].astype(BF16), preferred_element_type=F32) + b_ref[0]


def _mod_vectors(cc, w_mod, b_mod):
    depth, d, nm = w_mod.shape
    tn = 1024
    return pl.pallas_call(
        _mod_kernel,
        grid=(depth, nm // tn),
        in_specs=[
            pl.BlockSpec((8, d), lambda l, j: (0, 0)),
            pl.BlockSpec((1, d, tn), lambda l, j: (l, 0, j)),
            pl.BlockSpec((1, 1, tn), lambda l, j: (l, 0, j)),
        ],
        out_specs=pl.BlockSpec((1, 8, tn), lambda l, j: (l, 0, j)),
        out_shape=jax.ShapeDtypeStruct((depth, 8, nm), F32),
        compiler_params=_params(("parallel", "parallel")),
        name="mod_vectors",
    )(cc, w_mod, b_mod.reshape(depth, 1, nm))


def _ffn_kernel(x_ref, sh_ref, sc_ref, gt_ref, lng_ref, lnb_ref, wg_ref, wu_ref, wo_ref, *rest, emit_h):
    if emit_h:
        sh2_ref, sc2_ref, o_ref, h_ref, xin_scr, acc_scr = rest
    else:
        o_ref, xin_scr, acc_scr = rest
    j = pl.program_id(1)

    @pl.when(j == 0)
    def _():
        xin_scr[...] = (x_ref[...] * (1.0 + sc_ref[0]) + sh_ref[0]).astype(BF16)
        acc_scr[...] = jnp.zeros_like(acc_scr)

    xin = xin_scr[...]
    g = jnp.dot(xin, wg_ref[...], preferred_element_type=F32)
    u = jnp.dot(xin, wu_ref[...], preferred_element_type=F32)
    hh = ((g * jax.nn.sigmoid(g)) * u).astype(BF16)
    acc_scr[...] += jnp.dot(hh, wo_ref[...], preferred_element_type=F32)

    @pl.when(j == pl.num_programs(1) - 1)
    def _():
        t = DEEPNORM_ALPHA * x_ref[...] + (0.5 * gt_ref[0]) * acc_scr[...]
        y = _ln(t, lng_ref[...], lnb_ref[...])
        o_ref[...] = y
        if emit_h:
            h_ref[...] = (y * (1.0 + sc2_ref[0]) + sh2_ref[0]).astype(BF16)


def _ffn(x, mods_l, ks, lng, lnb, w_in, w_out, n_tiles, nlt, emit_ks=None):
    d = D_MODEL
    nj = D_FF // TF
    emit_h = emit_ks is not None

    def mod_spec(k):
        return pl.BlockSpec((1, 1, d), lambda i, j, k=k: (jnp.where(i >= nlt, N_MOD, 0) + k, 0, 0))

    vec_spec = pl.BlockSpec((1, d), lambda i, j: (0, 0))
    in_specs = [
        pl.BlockSpec((TM, d), lambda i, j: (i, 0)),
        mod_spec(ks[0]), mod_spec(ks[1]), mod_spec(ks[2]),
        vec_spec, vec_spec,
        pl.BlockSpec((d, TF), lambda i, j: (0, j)),
        pl.BlockSpec((d, TF), lambda i, j: (0, nj + j)),
        pl.BlockSpec((TF, d), lambda i, j: (j, 0)),
    ]
    args = [x, mods_l, mods_l, mods_l, lng, lnb, w_in, w_in, w_out]
    out_specs = pl.BlockSpec((TM, d), lambda i, j: (i, 0))
    out_shape = jax.ShapeDtypeStruct((n_tiles * TM, d), F32)
    if emit_h:
        in_specs += [mod_spec(emit_ks[0]), mod_spec(emit_ks[1])]
        args += [mods_l, mods_l]
        out_specs = [out_specs, pl.BlockSpec((TM, d), lambda i, j: (i, 0))]
        out_shape = [out_shape, jax.ShapeDtypeStruct((n_tiles * TM, d), BF16)]
    return pl.pallas_call(
        functools.partial(_ffn_kernel, emit_h=emit_h),
        grid=(n_tiles, nj),
        in_specs=in_specs,
        out_specs=out_specs,
        out_shape=out_shape,
        scratch_shapes=[pltpu.VMEM((TM, d), BF16), pltpu.VMEM((TM, d), F32)],
        compiler_params=_params(("parallel", "arbitrary")),
        name="ffn_emit" if emit_h else "ffn",
    )(*args)


def _rope(t, cos_ref, sin_ref):
    w = t.shape[1]
    lane = lax.broadcasted_iota(jnp.int32, t.shape, 1)
    first = (lane % HEAD_DIM) < (HEAD_DIM // 2)
    rot = jnp.where(first, pltpu.roll(t, w - HEAD_DIM // 2, 1), pltpu.roll(t, HEAD_DIM // 2, 1))
    reps = w // LANES
    cos = jnp.concatenate([cos_ref[...]] * reps, axis=1) if reps > 1 else cos_ref[...]
    sin = jnp.concatenate([sin_ref[...]] * reps, axis=1) if reps > 1 else sin_ref[...]
    return t * cos + rot * sin


_SEG_AQ, _SEG_AK, _SEG_AV, _SEG_BU, _SEG_BV, _SEG_CQ, _SEG_CKV, _SEG_DQ, _SEG_DK, _SEG_DV = range(N_SEG)
_SEG_WIDTHS = tuple(2 * C_KV_HEADS * HEAD_DIM if k == _SEG_CKV else GROUP_WIDTH for k in range(N_SEG))
_SEG_OFFSETS = tuple(sum(_SEG_WIDTHS[:k]) for k in range(N_SEG))
D_PROJ = sum(_SEG_WIDTHS)


def _dup_halves(x):
    swapped = pltpu.roll(x, HEAD_DIM, 1)
    first, _ = _half_masks(x.shape)
    return jnp.where(first, x, swapped), jnp.where(first, swapped, x)


def _inproj_kernel(h_ref, w_ref, cos_ref, sin_ref, *out_refs):
    h = h_ref[...]

    def project(k):
        return jnp.dot(h, w_ref[:, _SEG_OFFSETS[k]:_SEG_OFFSETS[k] + _SEG_WIDTHS[k]], preferred_element_type=F32)

    def finish(k, t):
        o_ref = out_refs[k]
        if k in (_SEG_AQ, _SEG_AK, _SEG_CQ):
            t = _rope(t, cos_ref, sin_ref)
        if k == _SEG_AQ:
            t = t * (QK_SCALE * LOG2E)
        if k in (_SEG_CQ, _SEG_DQ):
            t = t * QK_SCALE
        if k == _SEG_AQ:
            o_ref[0] = t.T.astype(o_ref.dtype)
        elif k == _SEG_AV:
            tt = t.T.astype(o_ref.dtype)
            extra = A_VROWS - 2 * HEAD_DIM
            ones_row = (lax.broadcasted_iota(jnp.int32, (extra, tt.shape[1]), 0) == 0).astype(o_ref.dtype)
            for hd in range(A_HEADS):
                o_ref[0, hd * A_VROWS:hd * A_VROWS + 2 * HEAD_DIM] = tt[hd * 2 * HEAD_DIM:(hd + 1) * 2 * HEAD_DIM]
                o_ref[0, hd * A_VROWS + 2 * HEAD_DIM:(hd + 1) * A_VROWS] = ones_row
        elif k == _SEG_CKV:
            kk = _dup_halves(_rope(t[:, :LANES], cos_ref, sin_ref))
            vv = _dup_halves(t[:, LANES:])
            o_ref[...] = jnp.concatenate([kk[0], kk[1], vv[0], vv[1]], axis=1).astype(o_ref.dtype)
        else:
            o_ref[...] = t.astype(o_ref.dtype)

    t = project(0)
    for k in range(N_SEG):
        t_next = project(k + 1) if k + 1 < N_SEG else None
        finish(k, t)
        t = t_next


def _inproj(h, w_in, cos_t, sin_t):
    rows = h.shape[0]
    n_tiles = rows // TM
    d = D_MODEL
    assert C_KV_HEADS == 2 and w_in.shape == (d, D_PROJ)
    widths = [SEG] * N_SEG
    dtypes = [BF16] * N_SEG
    dtypes[_SEG_BU] = F32
    dtypes[_SEG_BV] = F32
    out_specs = [pl.BlockSpec((TM, w), lambda i: (i, 0)) for w in widths]
    out_shape = [jax.ShapeDtypeStruct((rows, w), dt) for w, dt in zip(widths, dtypes)]
    for k in (_SEG_AQ, _SEG_AV):
        slab = SEG if k == _SEG_AQ else A_HEADS * A_VROWS
        out_specs[k] = pl.BlockSpec((1, slab, TM), lambda i: (i, 0, 0))
        out_shape[k] = jax.ShapeDtypeStruct((n_tiles, slab, TM), BF16)
    return pl.pallas_call(
        _inproj_kernel,
        grid=(n_tiles,),
        in_specs=[
            pl.BlockSpec((TM, d), lambda i: (i, 0)),
            pl.BlockSpec((d, D_PROJ), lambda i: (0, 0), pipeline_mode=pl.Buffered(1)),
            pl.BlockSpec((TM, LANES), lambda i: (i, 0)),
            pl.BlockSpec((TM, LANES), lambda i: (i, 0)),
        ],
        out_specs=out_specs,
        out_shape=out_shape,
        compiler_params=_params(("parallel",)),
        name="inproj",
    )(h, w_in, cos_t, sin_t)


def _attn_a_kernel(lam_ref, qt_ref, k_ref, vt_ref, *rest, n_chunks, tail_k_start, tail_chunk, tail_len, lam_init,
                   aliased):
    if aliased:
        rest = rest[1:]
    o_ref, m_scr, acc_scr, s_scr, mc_scr = rest
    qt = qt_ref[0]
    comp = lax.broadcasted_iota(jnp.int32, qt.shape, 0) < HEAD_DIM
    zero = jnp.zeros_like(qt)
    qts = (jnp.where(comp, qt, zero), jnp.where(comp, zero, qt))
    m_scr[...] = jnp.full(m_scr.shape, NEG_INF, F32)
    acc_scr[...] = jnp.zeros_like(acc_scr)

    def scores(k, slot, size):
        for c in range(2):
            s = jnp.dot(k, qts[c], preferred_element_type=F32)
            s_scr[slot, c, :size] = s
            mc_scr[slot, c] = jnp.max(s, axis=0, keepdims=True)

    def consume(slot, vt, size):
        for c in range(2):
            m_old = m_scr[c]
            m_new = jnp.maximum(m_old, mc_scr[slot, c])
            alpha = jnp.exp2(m_old - m_new)
            p = jnp.exp2(s_scr[slot, c, :size] - m_new).astype(BF16)
            acc_scr[c] = alpha * acc_scr[c] + jnp.dot(vt, p, preferred_element_type=F32)
            m_scr[c] = m_new

    def k_chunk(i):
        i = jnp.minimum(i, n_chunks - 1)
        return k_ref[pl.ds(pl.multiple_of(i * TM, TM), TM), :]

    if n_chunks:
        unroll = math.gcd(n_chunks, A_UNROLL)
        assert unroll % 2 == 0
        scores(k_chunk(0), 0, TM)

        def body(t, carry):
            j = unroll * t
            for u in range(unroll):
                scores(k_chunk(j + u + 1), (u + 1) % 2, TM)
                consume(u % 2, vt_ref[j + u], TM)
            return carry
        lax.fori_loop(0, n_chunks // unroll, body, 0)
    scores(k_ref[pl.ds(tail_k_start, tail_len), :], 0, tail_len)
    consume(0, vt_ref[tail_chunk][:, :tail_len], tail_len)

    lv = lam_ref[...]
    lam = (jnp.exp(jnp.sum(lv[0:1] * lv[1:2], axis=1, keepdims=True))
           - jnp.exp(jnp.sum(lv[2:3] * lv[3:4], axis=1, keepdims=True)) + lam_init)
    nv = 2 * HEAD_DIM
    ot = (acc_scr[0, :nv] / acc_scr[0, nv:nv + 1]
          - lam * (acc_scr[1, :nv] / acc_scr[1, nv:nv + 1]))
    ms = jnp.mean(ot * ot, axis=0, keepdims=True)
    ot = (ot * lax.rsqrt(ms + LN_EPS)) * (1.0 - lam_init)
    o_ref[...] = ot.T.astype(BF16)


def _attn_a(lam_vecs, aqt, ak, avt, n, l_ctx, lam_init, need_ctx):
    rows = ak.shape[0]
    nt = rows // TM
    nlt = n // TM
    scratch = [pltpu.VMEM((2, 1, TM), F32), pltpu.VMEM((2, A_VROWS, TM), F32),
               pltpu.VMEM((2, 2, TM, TM), F32), pltpu.VMEM((2, 2, 1, TM), F32)]
    lam_spec = pl.BlockSpec((4, HEAD_DIM), lambda h, i: (0, 0))
    ya = pl.pallas_call(
        functools.partial(_attn_a_kernel, n_chunks=nlt, tail_k_start=n, tail_chunk=nlt, tail_len=l_ctx,
                          lam_init=lam_init, aliased=False),
        grid=(A_HEADS, nlt),
        in_specs=[
            lam_spec,
            pl.BlockSpec((1, LANES, TM), lambda h, i: (i, h, 0)),
            pl.BlockSpec((rows, LANES), lambda h, i: (0, h)),
            pl.BlockSpec((nt, A_VROWS, TM), lambda h, i: (0, h, 0)),
        ],
        out_specs=pl.BlockSpec((TM, LANES), lambda h, i: (i, h)),
        out_shape=jax.ShapeDtypeStruct((rows, GROUP_WIDTH), BF16),
        scratch_shapes=scratch,
        compiler_params=_params(("parallel", "parallel")),
        name="attn_a",
    )(lam_vecs, aqt, ak, avt)
    if not need_ctx:
        return ya
    cb = n // l_ctx
    return pl.pallas_call(
        functools.partial(_attn_a_kernel, n_chunks=0, tail_k_start=0, tail_chunk=0, tail_len=l_ctx,
                          lam_init=lam_init, aliased=True),
        grid=(A_HEADS, 1),
        in_specs=[
            lam_spec,
            pl.BlockSpec((1, LANES, TM), lambda h, i: (nlt, h, 0)),
            pl.BlockSpec((l_ctx, LANES), lambda h, i: (cb, h)),
            pl.BlockSpec((1, A_VROWS, TM), lambda h, i: (nlt, h, 0)),
            pl.BlockSpec(memory_space=pl.ANY),
        ],
        out_specs=pl.BlockSpec((TM, LANES), lambda h, i: (nlt, h)),
        out_shape=jax.ShapeDtypeStruct((rows, GROUP_WIDTH), BF16),
        scratch_shapes=scratch,
        input_output_aliases={4: 0},
        compiler_params=_params(("parallel", "parallel")),
        name="attn_a_ctx",
    )(lam_vecs, aqt, ak, avt, ya)


def _mixb_kernel(u_ref, v_ref, g_ref, b_ref, ws_ref, bs_ref, o_ref):
    u = jax.nn.gelu(u_ref[...])
    v = _ln(jax.nn.gelu(v_ref[...]), g_ref[...], b_ref[...]).astype(BF16)
    for c in range(u.shape[0] // CHUNK):
        rs = slice(c * CHUNK, (c + 1) * CHUNK)
        for g in range(B_GROUPS):
            cs = slice(g * LANES, (g + 1) * LANES)
            mixed = jnp.dot(ws_ref[g].astype(BF16), v[rs, cs], preferred_element_type=F32) + bs_ref[:, g:g + 1]
            o_ref[rs, cs] = (u[rs, cs] * mixed).astype(BF16)


def _mixb(bu, bv, gn_g, gn_b, w_s, b_s_t):
    rows = bu.shape[0]
    gw = GROUP_WIDTH
    tile = pl.BlockSpec((TM, gw), lambda i: (i, 0))
    return pl.pallas_call(
        _mixb_kernel,
        grid=(rows // TM,),
        in_specs=[
            tile, tile,
            pl.BlockSpec((1, gw), lambda i: (0, 0)),
            pl.BlockSpec((1, gw), lambda i: (0, 0)),
            pl.BlockSpec((B_GROUPS, CHUNK, CHUNK), lambda i: (0, 0, 0)),
            pl.BlockSpec((CHUNK, B_GROUPS), lambda i: (0, 0)),
        ],
        out_specs=tile,
        out_shape=jax.ShapeDtypeStruct((rows, gw), BF16),
        compiler_params=_params(("parallel",)),
        name="mix_b",
    )(bu, bv, gn_g, gn_b, w_s, b_s_t)


def _paired_heads_attention(score_fn, values, n_pairs, sinks=None):
    all_scores = [score_fn(c, half) for c in range(n_pairs) for half in range(2)]
    all_probs = []
    for idx, scores in enumerate(all_scores):
        m = jnp.max(scores[0], axis=1, keepdims=True)
        for s in scores[1:]:
            m = jnp.maximum(m, jnp.max(s, axis=1, keepdims=True))
        if sinks is not None:
            m = jnp.maximum(m, sinks[idx])
        ps = [jnp.exp(s - m) for s in scores]
        den = jnp.sum(ps[0], axis=1, keepdims=True)
        for p in ps[1:]:
            den = den + jnp.sum(p, axis=1, keepdims=True)
        if sinks is not None:
            den = den + jnp.exp(sinks[idx] - m)
        all_probs.append(([p.astype(BF16) for p in ps], den))
    outs = []
    for c in range(n_pairs):
        out = None
        for half in range(2):
            ps, den = all_probs[2 * c + half]
            num = None
            for p, v in zip(ps, values[c]):
                first, second = _half_masks(v.shape)
                pv = jnp.dot(p, jnp.where(first if half == 0 else second, v, jnp.zeros_like(v)),
                             preferred_element_type=F32)
                num = pv if num is None else num + pv
            o = num / den
            out = o if out is None else out + o
        outs.append(out)
    return outs


def _attn_c_kernel(sink_ref, q_ref, k_ref, v_ref, *rest, n, l_ctx, local, aliased):
    o_ref = rest[-1]
    i = pl.program_id(0)
    n_sub = q_ref.shape[0] // QBLOCK
    wlen = 3 * QBLOCK
    ctx_start = n if local else 0
    n_pairs = C_HEADS // 2
    grp_pairs = n_pairs // C_KV_HEADS
    kv_block = lambda c: slice((c // grp_pairs) * LANES, (c // grp_pairs + 1) * LANES)
    sinks = [sink_ref[idx] for idx in range(C_HEADS)]

    def sub_block(b, carry):
        r0 = pl.multiple_of(b * QBLOCK, QBLOCK)
        if local:
            blk = i * n_sub + b
            start = pl.multiple_of(jnp.clip((blk - 1) * QBLOCK, 0, n - wlen), QBLOCK)
            qpos = blk * QBLOCK + lax.broadcasted_iota(jnp.int32, (QBLOCK, wlen), 0)
            kpos = start + lax.broadcasted_iota(jnp.int32, (QBLOCK, wlen), 1)
            valid = jnp.abs(kpos - qpos) <= WINDOW

        def score_fn(c, half):
            qb = q_ref[pl.ds(r0, QBLOCK), c * LANES:(c + 1) * LANES]
            first, second = _half_masks(qb.shape)
            qm = jnp.where(first if half == 0 else second, qb, jnp.zeros_like(qb))
            scores = [_dot_nt(qm, k_ref[pl.ds(ctx_start, l_ctx), kv_block(c)])]
            if local:
                scores.append(jnp.where(valid, _dot_nt(qm, k_ref[pl.ds(start, wlen), kv_block(c)]), NEG_INF))
            return scores

        values = []
        for c in range(n_pairs):
            vals = [v_ref[pl.ds(ctx_start, l_ctx), kv_block(c)]]
            if local:
                vals.append(v_ref[pl.ds(start, wlen), kv_block(c)])
            values.append(vals)
        outs = _paired_heads_attention(score_fn, values, n_pairs, sinks)
        for c, out in enumerate(outs):
            o_ref[pl.ds(r0, QBLOCK), c * LANES:(c + 1) * LANES] = out.astype(BF16)
        return carry

    lax.fori_loop(0, n_sub, sub_block, 0)


def _attn_c(sink, cq, ckv, n, l_ctx, need_ctx):
    rows = cq.shape[0]
    gw = GROUP_WIDTH
    kvw = C_KV_HEADS * LANES
    smem = pl.BlockSpec(memory_space=pltpu.SMEM)
    yc = pl.pallas_call(
        functools.partial(_attn_c_kernel, n=n, l_ctx=l_ctx, local=True, aliased=False),
        grid=(n // TM,),
        in_specs=[
            smem,
            pl.BlockSpec((TM, gw), lambda i: (i, 0)),
            pl.BlockSpec((rows, kvw), lambda i: (0, 0)),
            pl.BlockSpec((rows, kvw), lambda i: (0, 1)),
        ],
        out_specs=pl.BlockSpec((TM, gw), lambda i: (i, 0)),
        out_shape=jax.ShapeDtypeStruct((rows, gw), BF16),
        compiler_params=_params(("parallel",)),
        name="attn_c",
    )(sink, cq, ckv, ckv)
    if not need_ctx:
        return yc
    ct = n // TM
    cb = n // l_ctx
    return pl.pallas_call(
        functools.partial(_attn_c_kernel, n=n, l_ctx=l_ctx, local=False, aliased=True),
        grid=(1,),
        in_specs=[
            smem,
            pl.BlockSpec((TM, gw), lambda i: (ct, 0)),
            pl.BlockSpec((l_ctx, kvw), lambda i: (cb, 0)),
            pl.BlockSpec((l_ctx, kvw), lambda i: (cb, 1)),
            pl.BlockSpec(memory_space=pl.ANY),
        ],
        out_specs=pl.BlockSpec((TM, gw), lambda i: (ct, 0)),
        out_shape=jax.ShapeDtypeStruct((rows, gw), BF16),
        input_output_aliases={4: 0},
        compiler_params=_params(("parallel",)),
        name="attn_c_ctx",
    )(sink, cq, ckv, ckv, yc)


def _attn_d_kernel(q_ref, kc_ref, vc_ref, *rest, local, aliased, n_rows):
    o_ref = rest[-1]
    n_pairs = D_HEADS // 2
    lane_block = lambda c: slice(c * LANES, (c + 1) * LANES)
    nk = NA_ROWS * GRID_W
    if local:
        k_ref, v_ref, b_ref = rest[0], rest[1], rest[2]
        g = pl.program_id(0)
        win0 = jnp.clip(g * D_ROWS - NA_ROWS // 2, 0, n_rows - (D_ROWS + NA_ROWS - 1))
        q_rows, offs, shifts = [], [], []
        for i in range(D_ROWS):
            r = g * D_ROWS + i
            rs = jnp.clip(r - NA_ROWS // 2, 0, n_rows - NA_ROWS)
            q_rows.append(slice(i * GRID_W, (i + 1) * GRID_W))
            offs.append(pl.multiple_of((rs - win0) * GRID_W, GRID_W))
            shifts.append(r - rs)
    else:
        q_rows = [slice(0, q_ref.shape[0])]

    def score_fn(idx, half):
        i, c = divmod(idx, n_pairs)
        qb = q_ref[q_rows[i], lane_block(c)]
        first, second = _half_masks(qb.shape)
        qm = jnp.where(first if half == 0 else second, qb, jnp.zeros_like(qb))
        scores = [_dot_nt(qm, kc_ref[:, lane_block(c)])]
        if local:
            scores.append(_dot_nt(qm, k_ref[pl.ds(offs[i], nk), lane_block(c)]) + b_ref[shifts[i], 2 * c + half])
        return scores

    values = []
    for i in range(len(q_rows)):
        for c in range(n_pairs):
            vals = [vc_ref[:, lane_block(c)]]
            if local:
                vals.append(v_ref[pl.ds(offs[i], nk), lane_block(c)])
            values.append(vals)
    outs = _paired_heads_attention(score_fn, values, len(q_rows) * n_pairs)
    for idx, out in enumerate(outs):
        i, c = divmod(idx, n_pairs)
        o_ref[q_rows[i], lane_block(c)] = out.astype(BF16)


def _attn_d(dq, dk, dv, bias, n, l_ctx, need_ctx):
    rows = dq.shape[0]
    gw = GROUP_WIDTH
    n_rows = n // GRID_W
    nk = NA_ROWS * GRID_W
    cb = n // l_ctx
    win = D_ROWS + NA_ROWS - 1
    assert n_rows % D_ROWS == 0 and n_rows >= win

    def window_start(g):
        return jnp.clip(g * D_ROWS - NA_ROWS // 2, 0, n_rows - win) * GRID_W

    once = dict(pipeline_mode=pl.Buffered(1))
    yd = pl.pallas_call(
        functools.partial(_attn_d_kernel, local=True, aliased=False, n_rows=n_rows),
        grid=(n_rows // D_ROWS,),
        in_specs=[
            pl.BlockSpec((D_ROWS * GRID_W, gw), lambda g: (g, 0)),
            pl.BlockSpec((l_ctx, gw), lambda g: (cb, 0), **once),
            pl.BlockSpec((l_ctx, gw), lambda g: (cb, 0), **once),
            pl.BlockSpec((pl.Element(win * GRID_W), pl.Element(gw)), lambda g: (window_start(g), 0)),
            pl.BlockSpec((pl.Element(win * GRID_W), pl.Element(gw)), lambda g: (window_start(g), 0)),
            pl.BlockSpec((NA_ROWS, D_HEADS, GRID_W, nk), lambda g: (0, 0, 0, 0), **once),
        ],
        out_specs=pl.BlockSpec((D_ROWS * GRID_W, gw), lambda g: (g, 0)),
        out_shape=jax.ShapeDtypeStruct((rows, gw), BF16),
        compiler_params=_params(("parallel",)),
        name="attn_d",
    )(dq, dk, dv, dk, dv, bias)
    if not need_ctx:
        return yd
    ct = n // TM
    return pl.pallas_call(
        functools.partial(_attn_d_kernel, local=False, aliased=True, n_rows=n_rows),
        grid=(1,),
        in_specs=[
            pl.BlockSpec((TM, gw), lambda i: (ct, 0)),
            pl.BlockSpec((l_ctx, gw), lambda i: (cb, 0)),
            pl.BlockSpec((l_ctx, gw), lambda i: (cb, 0)),
            pl.BlockSpec(memory_space=pl.ANY),
        ],
        out_specs=pl.BlockSpec((TM, gw), lambda i: (ct, 0)),
        out_shape=jax.ShapeDtypeStruct((rows, gw), BF16),
        input_output_aliases={3: 0},
        compiler_params=_params(("parallel",)),
        name="attn_d_ctx",
    )(dq, dk, dv, yd)


def _outproj_kernel(ya_ref, yb_ref, yc_ref, yd_ref, w_ref, x_ref, gt_ref, lng_ref, lnb_ref, o_ref):
    y_in = jnp.concatenate([ya_ref[...], yb_ref[...], yc_ref[...], yd_ref[...]], axis=1)
    y = jnp.dot(y_in, w_ref[...], preferred_element_type=F32)
    t = DEEPNORM_ALPHA * x_ref[...] + gt_ref[0] * y
    o_ref[...] = _ln(t, lng_ref[...], lnb_ref[...])


def _outproj(ya, yb, yc, yd, w_out, x, mods_l, gate_k, lng, lnb, n_tiles, nlt):
    d = D_MODEL
    gw = GROUP_WIDTH
    ytile = pl.BlockSpec((TM, gw), lambda i: (i, 0))
    vec = pl.BlockSpec((1, d), lambda i: (0, 0))
    return pl.pallas_call(
        _outproj_kernel,
        grid=(n_tiles,),
        in_specs=[
            ytile, ytile, ytile, ytile,
            pl.BlockSpec((d, d), lambda i: (0, 0)),
            pl.BlockSpec((TM, d), lambda i: (i, 0)),
            pl.BlockSpec((1, 1, d), lambda i: (jnp.where(i >= nlt, N_MOD, 0) + gate_k, 0, 0)),
            vec, vec,
        ],
        out_specs=pl.BlockSpec((TM, d), lambda i: (i, 0)),
        out_shape=jax.ShapeDtypeStruct((n_tiles * TM, d), F32),
        compiler_params=_params(("parallel",)),
        name="outproj",
    )(ya, yb, yc, yd, w_out, x, mods_l, lng, lnb)


def _rope_tables(n, rows):
    t = jnp.arange(n, dtype=jnp.int32)
    row = (t // GRID_W).astype(F32)
    col = (t % GRID_W).astype(F32)
    n_freq = HEAD_DIM // 4
    inv = ROPE_THETA ** (-jnp.arange(n_freq, dtype=F32) / n_freq)
    ang = jnp.concatenate([row[:, None] * inv, col[:, None] * inv], axis=-1)
    cos, sin = jnp.cos(ang), jnp.sin(ang)
    cos_t = jnp.tile(jnp.concatenate([cos, cos], axis=-1), (1, LANES // HEAD_DIM))
    sin_t = jnp.tile(jnp.concatenate([-sin, sin], axis=-1), (1, LANES // HEAD_DIM))
    pad = rows - n
    cos_t = jnp.concatenate([cos_t, jnp.ones((pad, LANES), F32)], axis=0)
    sin_t = jnp.concatenate([sin_t, jnp.zeros((pad, LANES), F32)], axis=0)
    return cos_t, sin_t


def _na_bias(rpb):
    w = GRID_W
    cq = jnp.arange(w)
    cs = jnp.clip(cq - NA_COLS // 2, 0, w - NA_COLS)
    col_ok = (cq[None, :] >= cs[:, None]) & (cq[None, :] < cs[:, None] + NA_COLS)
    edge = w - NA_COLS
    ext = jnp.pad(rpb.astype(F32), ((0, 0), (0, 0), (edge, edge)), mode="edge")
    toep = jnp.stack([ext[:, :, w - 1 - q:2 * w - 1 - q] for q in range(w)], axis=2)
    toep = jnp.where(col_ok[None, None], toep, NEG_INF)
    per_shift = []
    for shift in range(NA_ROWS):
        sl = toep[:, NA_ROWS - 1 - shift:2 * NA_ROWS - 1 - shift]
        per_shift.append(jnp.swapaxes(sl, 1, 2).reshape(rpb.shape[0], w, NA_ROWS * w))
    return jnp.stack(per_shift, axis=0)


def kernel(x, c, ctx, c_ctx, w_mod, b_mod, ln_g, ln_b, ffn1_w_in, ffn1_w_out, ffn2_w_in, ffn2_w_out, mix_w_in, mix_w_out, a_lambda, b_norm_g, b_norm_b, b_spatial_w, b_spatial_b, c_sink, d_rpb):
    depth = w_mod.shape[0]
    n = x.shape[1]
    l_ctx = ctx.shape[1]
    d = D_MODEL
    assert x.shape[0] == 1 and n % TM == 0 and n // GRID_W >= NA_ROWS and n % 1024 == 0
    assert l_ctx % CHUNK == 0 and l_ctx <= TM and n % l_ctx == 0
    nlt = n // TM
    rows = n + TM

    xs = jnp.concatenate([x[0], ctx[0], jnp.zeros((rows - n - l_ctx, d), F32)], axis=0)
    cc = jnp.concatenate([c, c_ctx[None], jnp.zeros((6, d), F32)], axis=0)
    mods = _mod_vectors(cc, w_mod, b_mod)[:, :2].reshape(depth, 2 * N_MOD, 1, d)
    cos_t, sin_t = _rope_tables(n, rows)

    for l in range(depth):
        last = l == depth - 1
        lam_init = 0.8 - 0.6 * math.exp(-0.3 * l)
        mods_l = mods[l]
        lng = [ln_g[l, k][None] for k in range(3)]
        lnb = [ln_b[l, k][None] for k in range(3)]

        xs, h = _ffn(xs, mods_l, (0, 1, 2), lng[0], lnb[0], ffn1_w_in[l].astype(BF16), ffn1_w_out[l].astype(BF16),
                     nlt + 1, nlt, emit_ks=(3, 4))
        aq, ak, av, bu, bv, cq, ckv, dq, dk, dv = _inproj(h, mix_w_in[l].astype(BF16), cos_t, sin_t)
        ya = _attn_a(a_lambda[l], aq, ak, av, n, l_ctx, lam_init, not last)
        yb = _mixb(bu, bv, b_norm_g[l][None], b_norm_b[l][None], b_spatial_w[l], b_spatial_b[l].T)
        yc = _attn_c(c_sink[l], cq, ckv, n, l_ctx, not last)
        yd = _attn_d(dq, dk, dv, _na_bias(d_rpb[l]), n, l_ctx, not last)
        n_tiles = nlt if last else nlt + 1
        xs = _outproj(ya, yb, yc, yd, mix_w_out[l].astype(BF16), xs, mods_l, 5, lng[1], lnb[1], n_tiles, nlt)
        xs = _ffn(xs, mods_l, (6, 7, 8), lng[2], lnb[2], ffn2_w_in[l].astype(BF16), ffn2_w_out[l].astype(BF16),
                  n_tiles, nlt)
    return xs[None]
```

```python
import functools
import math

import jax
import jax.numpy as jnp
from jax import lax
from jax.experimental import pallas as pl
from jax.experimental.pallas import tpu as pltpu

F32 = jnp.float32
BF16 = jnp.bfloat16

D_MODEL = 2048
N_GROUPS = 4
GROUP_WIDTH = D_MODEL // N_GROUPS
HEAD_DIM = 64
GRID_W = 64
CHUNK = 128
B_GROUPS = GROUP_WIDTH // 128
A_HEADS = GROUP_WIDTH // (2 * HEAD_DIM)
C_HEADS = GROUP_WIDTH // HEAD_DIM
C_KV_HEADS = C_HEADS // 4
D_HEADS = GROUP_WIDTH // HEAD_DIM
WINDOW = 128
QBLOCK = 128
NA_ROWS = 8
NA_COLS = 16
D_FF = 256 * math.ceil(8 * D_MODEL / 3 / 256)
N_MOD = 9
ROPE_THETA = 10000.0
LN_EPS = 1e-6
NEG_INF = -1e30
MODEL_DEPTH = 4
DEEPNORM_ALPHA = (2 * MODEL_DEPTH) ** 0.25
QK_SCALE = HEAD_DIM ** -0.5
LOG2E = math.log2(math.e)
A_VROWS = 2 * HEAD_DIM + 16
A_UNROLL = 8
D_ROWS = 4

LANES = 128
TM = 512
TM_FFN = 1024
TF = 512
SEG = 512
N_SEG = 10
VMEM_LIMIT = 60 * 1024 * 1024


def _params(sem):
    return pltpu.CompilerParams(dimension_semantics=sem, vmem_limit_bytes=VMEM_LIMIT)


def _ln(t, g, b):
    mu = jnp.mean(t, axis=-1, keepdims=True)
    tc = t - mu
    var = jnp.mean(tc * tc, axis=-1, keepdims=True)
    return tc * lax.rsqrt(var + LN_EPS) * g + b


def _dot_nt(a, b):
    return lax.dot_general(a, b, (((1,), (1,)), ((), ())), preferred_element_type=F32)


def _half_masks(shape):
    lane = lax.broadcasted_iota(jnp.int32, shape, 1)
    return lane < HEAD_DIM, lane >= HEAD_DIM


def _mod_kernel(cc_ref, w_ref, b_ref, o_ref):
    a = cc_ref[...]
    s = (a * jax.nn.sigmoid(a)).astype(BF16)
    o_ref[0] = jnp.dot(s, w_ref[0].astype(BF16), preferred_element_type=F32) + b_ref[0]


def _mod_vectors(cc, w_mod, b_mod):
    depth, d, nm = w_mod.shape
    tn = 1024
    return pl.pallas_call(
        _mod_kernel,
        grid=(depth, nm // tn),
        in_specs=[
            pl.BlockSpec((8, d), lambda l, j: (0, 0)),
            pl.BlockSpec((1, d, tn), lambda l, j: (l, 0, j)),
            pl.BlockSpec((1, 1, tn), lambda l, j: (l, 0, j)),
        ],
        out_specs=pl.BlockSpec((1, 8, tn), lambda l, j: (l, 0, j)),
        out_shape=jax.ShapeDtypeStruct((depth, 8, nm), F32),
        compiler_params=_params(("parallel", "parallel")),
        name="mod_vectors",
    )(cc, w_mod, b_mod.reshape(depth, 1, nm))


def _ffn_kernel(x_ref, sh_ref, sc_ref, gt_ref, lng_ref, lnb_ref, wg_ref, wu_ref, wo_ref, o_ref):
    j = pl.program_id(1)

    @pl.when(j == 0)
    def _():
        o_ref[...] = jnp.zeros_like(o_ref)

    half = x_ref.shape[0] // 2
    for r in range(2):
        rows = slice(r * half, (r + 1) * half)
        xin = (x_ref[rows] * (1.0 + sc_ref[0]) + sh_ref[0]).astype(BF16)
        g = jnp.dot(xin, wg_ref[...], preferred_element_type=F32)
        u = jnp.dot(xin, wu_ref[...], preferred_element_type=F32)
        hh = ((g * jax.nn.sigmoid(g)) * u).astype(BF16)
        o_ref[rows] += jnp.dot(hh, wo_ref[...], preferred_element_type=F32)

    @pl.when(j == pl.num_programs(1) - 1)
    def _():
        for r in range(2):
            rows = slice(r * half, (r + 1) * half)
            t = DEEPNORM_ALPHA * x_ref[rows] + (0.5 * gt_ref[0]) * o_ref[rows]
            o_ref[rows] = _ln(t, lng_ref[...], lnb_ref[...])


def _ffn(x, mods_l, ks, lng, lnb, w_in, w_out, layer, n_rows, n):
    d = D_MODEL
    nj = D_FF // TF
    assert n % TM_FFN == 0
    nlt = n // TM_FFN

    def mod_spec(k):
        return pl.BlockSpec((1, 1, d), lambda i, j, k=k: (jnp.where(i >= nlt, N_MOD, 0) + k, 0, 0))

    vec_spec = pl.BlockSpec((1, d), lambda i, j: (0, 0))
    return pl.pallas_call(
        _ffn_kernel,
        grid=(pl.cdiv(n_rows, TM_FFN), nj),
        in_specs=[
            pl.BlockSpec((TM_FFN, d), lambda i, j: (i, 0)),
            mod_spec(ks[0]), mod_spec(ks[1]), mod_spec(ks[2]),
            vec_spec, vec_spec,
            pl.BlockSpec((None, d, TF), lambda i, j: (layer, 0, j)),
            pl.BlockSpec((None, d, TF), lambda i, j: (layer, 0, nj + j)),
            pl.BlockSpec((None, TF, d), lambda i, j: (layer, j, 0)),
        ],
        out_specs=pl.BlockSpec((TM_FFN, d), lambda i, j: (i, 0)),
        out_shape=jax.ShapeDtypeStruct((n_rows, d), F32),
        compiler_params=_params(("parallel", "arbitrary")),
        name="ffn",
    )(x, mods_l, mods_l, mods_l, lng, lnb, w_in, w_in, w_out)


def _rope(t, cos_ref, sin_ref):
    w = t.shape[1]
    lane = lax.broadcasted_iota(jnp.int32, t.shape, 1)
    first = (lane % HEAD_DIM) < (HEAD_DIM // 2)
    rot = jnp.where(first, pltpu.roll(t, w - HEAD_DIM // 2, 1), pltpu.roll(t, HEAD_DIM // 2, 1))
    reps = w // LANES
    cos = jnp.concatenate([cos_ref[...]] * reps, axis=1) if reps > 1 else cos_ref[...]
    sin = jnp.concatenate([sin_ref[...]] * reps, axis=1) if reps > 1 else sin_ref[...]
    return t * cos + rot * sin


_SEG_AQ, _SEG_AK, _SEG_AV, _SEG_BU, _SEG_BV, _SEG_CQ, _SEG_CKV, _SEG_DQ, _SEG_DK, _SEG_DV = range(N_SEG)
_SEG_WIDTHS = tuple(2 * C_KV_HEADS * HEAD_DIM if k == _SEG_CKV else GROUP_WIDTH for k in range(N_SEG))
_SEG_OFFSETS = tuple(sum(_SEG_WIDTHS[:k]) for k in range(N_SEG))
D_PROJ = sum(_SEG_WIDTHS)


def _dup_halves(x):
    swapped = pltpu.roll(x, HEAD_DIM, 1)
    first, _ = _half_masks(x.shape)
    return jnp.where(first, x, swapped), jnp.where(first, swapped, x)


def _inproj_kernel(x_ref, sh_ref, sc_ref, w_ref, cos_ref, sin_ref, *out_refs):
    h = (x_ref[...] * (1.0 + sc_ref[0]) + sh_ref[0]).astype(BF16)

    def project(k):
        return jnp.dot(h, w_ref[:, _SEG_OFFSETS[k]:_SEG_OFFSETS[k] + _SEG_WIDTHS[k]], preferred_element_type=F32)

    def finish(k, t):
        o_ref = out_refs[k]
        if k in (_SEG_AQ, _SEG_AK, _SEG_CQ):
            t = _rope(t, cos_ref, sin_ref)
        if k == _SEG_AQ:
            t = t * (QK_SCALE * LOG2E)
        if k in (_SEG_CQ, _SEG_DQ):
            t = t * QK_SCALE
        if k == _SEG_AQ:
            o_ref[0] = t.T.astype(o_ref.dtype)
        elif k == _SEG_AV:
            tt = t.T.astype(o_ref.dtype)
            extra = A_VROWS - 2 * HEAD_DIM
            ones_row = (lax.broadcasted_iota(jnp.int32, (extra, tt.shape[1]), 0) == 0).astype(o_ref.dtype)
            for hd in range(A_HEADS):
                o_ref[0, hd * A_VROWS:hd * A_VROWS + 2 * HEAD_DIM] = tt[hd * 2 * HEAD_DIM:(hd + 1) * 2 * HEAD_DIM]
                o_ref[0, hd * A_VROWS + 2 * HEAD_DIM:(hd + 1) * A_VROWS] = ones_row
        elif k == _SEG_CKV:
            kk = _dup_halves(_rope(t[:, :LANES], cos_ref, sin_ref))
            vv = _dup_halves(t[:, LANES:])
            o_ref[...] = jnp.concatenate([kk[0], kk[1], vv[0], vv[1]], axis=1).astype(o_ref.dtype)
        else:
            o_ref[...] = t.astype(o_ref.dtype)

    t = project(0)
    for k in range(N_SEG):
        t_next = project(k + 1) if k + 1 < N_SEG else None
        finish(k, t)
        t = t_next


def _inproj(x, mods_l, shift_k, scale_k, w_in, layer, cos_t, sin_t, nlt):
    rows = x.shape[0]

    def mod_spec(k):
        return pl.BlockSpec((1, 1, D_MODEL), lambda i: (jnp.where(i >= nlt, N_MOD, 0) + k, 0, 0))

    n_tiles = rows // TM
    d = D_MODEL
    assert C_KV_HEADS == 2 and w_in.shape[1:] == (d, D_PROJ)
    widths = [SEG] * N_SEG
    dtypes = [BF16] * N_SEG
    dtypes[_SEG_BU] = F32
    dtypes[_SEG_BV] = F32
    out_specs = [pl.BlockSpec((TM, w), lambda i: (i, 0)) for w in widths]
    out_shape = [jax.ShapeDtypeStruct((rows, w), dt) for w, dt in zip(widths, dtypes)]
    for k in (_SEG_AQ, _SEG_AV):
        slab = SEG if k == _SEG_AQ else A_HEADS * A_VROWS
        out_specs[k] = pl.BlockSpec((1, slab, TM), lambda i: (i, 0, 0))
        out_shape[k] = jax.ShapeDtypeStruct((n_tiles, slab, TM), BF16)
    return pl.pallas_call(
        _inproj_kernel,
        grid=(n_tiles,),
        in_specs=[
            pl.BlockSpec((TM, d), lambda i: (i, 0)),
            mod_spec(shift_k), mod_spec(scale_k),
            pl.BlockSpec((None, d, D_PROJ), lambda i: (layer, 0, 0), pipeline_mode=pl.Buffered(1)),
            pl.BlockSpec((TM, LANES), lambda i: (i, 0)),
            pl.BlockSpec((TM, LANES), lambda i: (i, 0)),
        ],
        out_specs=out_specs,
        out_shape=out_shape,
        compiler_params=_params(("parallel",)),
        name="inproj",
    )(x, mods_l, mods_l, w_in, cos_t, sin_t)


def _attn_a_kernel(lam_ref, qt_ref, k_ref, vt_ref, *rest, n_chunks, tail_k_start, tail_chunk, tail_len, lam_init,
                   aliased):
    if aliased:
        rest = rest[1:]
    o_ref, m_scr, acc_scr, s_scr, mc_scr = rest
    qt = qt_ref[0]
    comp = lax.broadcasted_iota(jnp.int32, qt.shape, 0) < HEAD_DIM
    zero = jnp.zeros_like(qt)
    qts = (jnp.where(comp, qt, zero), jnp.where(comp, zero, qt))
    m_scr[...] = jnp.full(m_scr.shape, NEG_INF, F32)
    acc_scr[...] = jnp.zeros_like(acc_scr)

    def scores(k, slot, size):
        for c in range(2):
            s = jnp.dot(k, qts[c], preferred_element_type=F32)
            s_scr[slot, c, :size] = s
            mc_scr[slot, c] = jnp.max(s, axis=0, keepdims=True)

    def consume(slot, vt, size):
        for c in range(2):
            m_old = m_scr[c]
            m_new = jnp.maximum(m_old, mc_scr[slot, c])
            alpha = jnp.exp2(m_old - m_new)
            p = jnp.exp2(s_scr[slot, c, :size] - m_new).astype(BF16)
            acc_scr[c] = alpha * acc_scr[c] + jnp.dot(vt, p, preferred_element_type=F32)
            m_scr[c] = m_new

    def k_chunk(i):
        i = jnp.minimum(i, n_chunks - 1)
        return k_ref[pl.ds(pl.multiple_of(i * TM, TM), TM), :]

    if n_chunks:
        unroll = math.gcd(n_chunks, A_UNROLL)
        assert unroll % 2 == 0
        scores(k_chunk(0), 0, TM)

        def body(t, carry):
            j = unroll * t
            for u in range(unroll):
                scores(k_chunk(j + u + 1), (u + 1) % 2, TM)
                consume(u % 2, vt_ref[j + u], TM)
            return carry
        lax.fori_loop(0, n_chunks // unroll, body, 0)
    scores(k_ref[pl.ds(tail_k_start, tail_len), :], 0, tail_len)
    consume(0, vt_ref[tail_chunk][:, :tail_len], tail_len)

    lv = lam_ref[...]
    lam = (jnp.exp(jnp.sum(lv[0:1] * lv[1:2], axis=1, keepdims=True))
           - jnp.exp(jnp.sum(lv[2:3] * lv[3:4], axis=1, keepdims=True)) + lam_init)
    nv = 2 * HEAD_DIM
    ot = (acc_scr[0, :nv] / acc_scr[0, nv:nv + 1]
          - lam * (acc_scr[1, :nv] / acc_scr[1, nv:nv + 1]))
    ms = jnp.mean(ot * ot, axis=0, keepdims=True)
    ot = (ot * lax.rsqrt(ms + LN_EPS)) * (1.0 - lam_init)
    o_ref[...] = ot.T.astype(BF16)


def _attn_a(lam_vecs, aqt, ak, avt, n, l_ctx, lam_init, need_ctx):
    rows = ak.shape[0]
    nt = rows // TM
    nlt = n // TM
    scratch = [pltpu.VMEM((2, 1, TM), F32), pltpu.VMEM((2, A_VROWS, TM), F32),
               pltpu.VMEM((2, 2, TM, TM), F32), pltpu.VMEM((2, 2, 1, TM), F32)]
    lam_spec = pl.BlockSpec((4, HEAD_DIM), lambda h, i: (0, 0))
    ya = pl.pallas_call(
        functools.partial(_attn_a_kernel, n_chunks=nlt, tail_k_start=n, tail_chunk=nlt, tail_len=l_ctx,
                          lam_init=lam_init, aliased=False),
        grid=(A_HEADS, nlt),
        in_specs=[
            lam_spec,
            pl.BlockSpec((1, LANES, TM), lambda h, i: (i, h, 0)),
            pl.BlockSpec((rows, LANES), lambda h, i: (0, h)),
            pl.BlockSpec((nt, A_VROWS, TM), lambda h, i: (0, h, 0)),
        ],
        out_specs=pl.BlockSpec((TM, LANES), lambda h, i: (i, h)),
        out_shape=jax.ShapeDtypeStruct((rows, GROUP_WIDTH), BF16),
        scratch_shapes=scratch,
        compiler_params=_params(("parallel", "parallel")),
        name="attn_a",
    )(lam_vecs, aqt, ak, avt)
    if not need_ctx:
        return ya
    cb = n // l_ctx
    return pl.pallas_call(
        functools.partial(_attn_a_kernel, n_chunks=0, tail_k_start=0, tail_chunk=0, tail_len=l_ctx,
                          lam_init=lam_init, aliased=True),
        grid=(A_HEADS, 1),
        in_specs=[
            lam_spec,
            pl.BlockSpec((1, LANES, TM), lambda h, i: (nlt, h, 0)),
            pl.BlockSpec((l_ctx, LANES), lambda h, i: (cb, h)),
            pl.BlockSpec((1, A_VROWS, TM), lambda h, i: (nlt, h, 0)),
            pl.BlockSpec(memory_space=pl.ANY),
        ],
        out_specs=pl.BlockSpec((TM, LANES), lambda h, i: (nlt, h)),
        out_shape=jax.ShapeDtypeStruct((rows, GROUP_WIDTH), BF16),
        scratch_shapes=scratch,
        input_output_aliases={4: 0},
        compiler_params=_params(("parallel", "parallel")),
        name="attn_a_ctx",
    )(lam_vecs, aqt, ak, avt, ya)


def _mixb_kernel(u_ref, v_ref, g_ref, b_ref, ws_ref, bs_ref, o_ref):
    u = jax.nn.gelu(u_ref[...])
    v = _ln(jax.nn.gelu(v_ref[...]), g_ref[...], b_ref[...]).astype(BF16)
    for c in range(u.shape[0] // CHUNK):
        rs = slice(c * CHUNK, (c + 1) * CHUNK)
        for g in range(B_GROUPS):
            cs = slice(g * LANES, (g + 1) * LANES)
            mixed = jnp.dot(ws_ref[g].astype(BF16), v[rs, cs], preferred_element_type=F32) + bs_ref[:, g:g + 1]
            o_ref[rs, cs] = (u[rs, cs] * mixed).astype(BF16)


def _mixb(bu, bv, gn_g, gn_b, w_s, b_s_t):
    rows = bu.shape[0]
    gw = GROUP_WIDTH
    tile = pl.BlockSpec((TM, gw), lambda i: (i, 0))
    return pl.pallas_call(
        _mixb_kernel,
        grid=(rows // TM,),
        in_specs=[
            tile, tile,
            pl.BlockSpec((1, gw), lambda i: (0, 0)),
            pl.BlockSpec((1, gw), lambda i: (0, 0)),
            pl.BlockSpec((B_GROUPS, CHUNK, CHUNK), lambda i: (0, 0, 0)),
            pl.BlockSpec((CHUNK, B_GROUPS), lambda i: (0, 0)),
        ],
        out_specs=tile,
        out_shape=jax.ShapeDtypeStruct((rows, gw), BF16),
        compiler_params=_params(("parallel",)),
        name="mix_b",
    )(bu, bv, gn_g, gn_b, w_s, b_s_t)


def _paired_heads_attention(score_fn, values, n_pairs, sinks=None):
    all_scores = [score_fn(c, half) for c in range(n_pairs) for half in range(2)]
    all_probs = []
    for idx, scores in enumerate(all_scores):
        m = jnp.max(scores[0], axis=1, keepdims=True)
        for s in scores[1:]:
            m = jnp.maximum(m, jnp.max(s, axis=1, keepdims=True))
        if sinks is not None:
            m = jnp.maximum(m, sinks[idx])
        ps = [jnp.exp(s - m) for s in scores]
        den = jnp.sum(ps[0], axis=1, keepdims=True)
        for p in ps[1:]:
            den = den + jnp.sum(p, axis=1, keepdims=True)
        if sinks is not None:
            den = den + jnp.exp(sinks[idx] - m)
        all_probs.append(([p.astype(BF16) for p in ps], den))
    outs = []
    for c in range(n_pairs):
        out = None
        for half in range(2):
            ps, den = all_probs[2 * c + half]
            num = None
            for p, v in zip(ps, values[c]):
                first, second = _half_masks(v.shape)
                pv = jnp.dot(p, jnp.where(first if half == 0 else second, v, jnp.zeros_like(v)),
                             preferred_element_type=F32)
                num = pv if num is None else num + pv
            o = num / den
            out = o if out is None else out + o
        outs.append(out)
    return outs


def _attn_c_kernel(sink_ref, q_ref, k_ref, v_ref, *rest, n, l_ctx, local, aliased):
    o_ref = rest[-1]
    i = pl.program_id(0)
    n_sub = q_ref.shape[0] // QBLOCK
    wlen = 3 * QBLOCK
    ctx_start = n if local else 0
    n_pairs = C_HEADS // 2
    grp_pairs = n_pairs // C_KV_HEADS
    kv_block = lambda c: slice((c // grp_pairs) * LANES, (c // grp_pairs + 1) * LANES)
    sinks = [sink_ref[idx] for idx in range(C_HEADS)]

    def sub_block(b, carry):
        r0 = pl.multiple_of(b * QBLOCK, QBLOCK)
        if local:
            blk = i * n_sub + b
            start = pl.multiple_of(jnp.clip((blk - 1) * QBLOCK, 0, n - wlen), QBLOCK)
            qpos = blk * QBLOCK + lax.broadcasted_iota(jnp.int32, (QBLOCK, wlen), 0)
            kpos = start + lax.broadcasted_iota(jnp.int32, (QBLOCK, wlen), 1)
            valid = jnp.abs(kpos - qpos) <= WINDOW

        def score_fn(c, half):
            qb = q_ref[pl.ds(r0, QBLOCK), c * LANES:(c + 1) * LANES]
            first, second = _half_masks(qb.shape)
            qm = jnp.where(first if half == 0 else second, qb, jnp.zeros_like(qb))
            scores = [_dot_nt(qm, k_ref[pl.ds(ctx_start, l_ctx), kv_block(c)])]
            if local:
                scores.append(jnp.where(valid, _dot_nt(qm, k_ref[pl.ds(start, wlen), kv_block(c)]), NEG_INF))
            return scores

        values = []
        for c in range(n_pairs):
            vals = [v_ref[pl.ds(ctx_start, l_ctx), kv_block(c)]]
            if local:
                vals.append(v_ref[pl.ds(start, wlen), kv_block(c)])
            values.append(vals)
        outs = _paired_heads_attention(score_fn, values, n_pairs, sinks)
        for c, out in enumerate(outs):
            o_ref[pl.ds(r0, QBLOCK), c * LANES:(c + 1) * LANES] = out.astype(BF16)
        return carry

    lax.fori_loop(0, n_sub, sub_block, 0)


def _attn_c(sink, cq, ckv, n, l_ctx, need_ctx):
    rows = cq.shape[0]
    gw = GROUP_WIDTH
    kvw = C_KV_HEADS * LANES
    smem = pl.BlockSpec(memory_space=pltpu.SMEM)
    yc = pl.pallas_call(
        functools.partial(_attn_c_kernel, n=n, l_ctx=l_ctx, local=True, aliased=False),
        grid=(n // TM,),
        in_specs=[
            smem,
            pl.BlockSpec((TM, gw), lambda i: (i, 0)),
            pl.BlockSpec((rows, kvw), lambda i: (0, 0)),
            pl.BlockSpec((rows, kvw), lambda i: (0, 1)),
        ],
        out_specs=pl.BlockSpec((TM, gw), lambda i: (i, 0)),
        out_shape=jax.ShapeDtypeStruct((rows, gw), BF16),
        compiler_params=_params(("parallel",)),
        name="attn_c",
    )(sink, cq, ckv, ckv)
    if not need_ctx:
        return yc
    ct = n // TM
    cb = n // l_ctx
    return pl.pallas_call(
        functools.partial(_attn_c_kernel, n=n, l_ctx=l_ctx, local=False, aliased=True),
        grid=(1,),
        in_specs=[
            smem,
            pl.BlockSpec((TM, gw), lambda i: (ct, 0)),
            pl.BlockSpec((l_ctx, kvw), lambda i: (cb, 0)),
            pl.BlockSpec((l_ctx, kvw), lambda i: (cb, 1)),
            pl.BlockSpec(memory_space=pl.ANY),
        ],
        out_specs=pl.BlockSpec((TM, gw), lambda i: (ct, 0)),
        out_shape=jax.ShapeDtypeStruct((rows, gw), BF16),
        input_output_aliases={4: 0},
        compiler_params=_params(("parallel",)),
        name="attn_c_ctx",
    )(sink, cq, ckv, ckv, yc)


def _attn_d_kernel(q_ref, kc_ref, vc_ref, *rest, local, aliased, n_rows):
    o_ref = rest[-1]
    n_pairs = D_HEADS // 2
    lane_block = lambda c: slice(c * LANES, (c + 1) * LANES)
    nk = NA_ROWS * GRID_W
    if local:
        k_ref, v_ref, b_ref = rest[0], rest[1], rest[2]
        g = pl.program_id(0)
        win0 = jnp.clip(g * D_ROWS - NA_ROWS // 2, 0, n_rows - (D_ROWS + NA_ROWS - 1))
        q_rows, offs, shifts = [], [], []
        for i in range(D_ROWS):
            r = g * D_ROWS + i
            rs = jnp.clip(r - NA_ROWS // 2, 0, n_rows - NA_ROWS)
            q_rows.append(slice(i * GRID_W, (i + 1) * GRID_W))
            offs.append(pl.multiple_of((rs - win0) * GRID_W, GRID_W))
            shifts.append(r - rs)
    else:
        q_rows = [slice(0, q_ref.shape[0])]

    def score_fn(idx, half):
        i, c = divmod(idx, n_pairs)
        qb = q_ref[q_rows[i], lane_block(c)]
        first, second = _half_masks(qb.shape)
        qm = jnp.where(first if half == 0 else second, qb, jnp.zeros_like(qb))
        scores = [_dot_nt(qm, kc_ref[:, lane_block(c)])]
        if local:
            scores.append(_dot_nt(qm, k_ref[pl.ds(offs[i], nk), lane_block(c)]) + b_ref[shifts[i], 2 * c + half])
        return scores

    values = []
    for i in range(len(q_rows)):
        for c in range(n_pairs):
            vals = [vc_ref[:, lane_block(c)]]
            if local:
                vals.append(v_ref[pl.ds(offs[i], nk), lane_block(c)])
            values.append(vals)
    outs = _paired_heads_attention(score_fn, values, len(q_rows) * n_pairs)
    for idx, out in enumerate(outs):
        i, c = divmod(idx, n_pairs)
        o_ref[q_rows[i], lane_block(c)] = out.astype(BF16)


def _attn_d(dq, dk, dv, bias, n, l_ctx, need_ctx):
    rows = dq.shape[0]
    gw = GROUP_WIDTH
    n_rows = n // GRID_W
    nk = NA_ROWS * GRID_W
    cb = n // l_ctx
    win = D_ROWS + NA_ROWS - 1
    assert n_rows % D_ROWS == 0 and n_rows >= win

    def window_start(g):
        return jnp.clip(g * D_ROWS - NA_ROWS // 2, 0, n_rows - win) * GRID_W

    once = dict(pipeline_mode=pl.Buffered(1))
    yd = pl.pallas_call(
        functools.partial(_attn_d_kernel, local=True, aliased=False, n_rows=n_rows),
        grid=(n_rows // D_ROWS,),
        in_specs=[
            pl.BlockSpec((D_ROWS * GRID_W, gw), lambda g: (g, 0)),
            pl.BlockSpec((l_ctx, gw), lambda g: (cb, 0), **once),
            pl.BlockSpec((l_ctx, gw), lambda g: (cb, 0), **once),
            pl.BlockSpec((pl.Element(win * GRID_W), pl.Element(gw)), lambda g: (window_start(g), 0)),
            pl.BlockSpec((pl.Element(win * GRID_W), pl.Element(gw)), lambda g: (window_start(g), 0)),
            pl.BlockSpec((NA_ROWS, D_HEADS, GRID_W, nk), lambda g: (0, 0, 0, 0), **once),
        ],
        out_specs=pl.BlockSpec((D_ROWS * GRID_W, gw), lambda g: (g, 0)),
        out_shape=jax.ShapeDtypeStruct((rows, gw), BF16),
        compiler_params=_params(("parallel",)),
        name="attn_d",
    )(dq, dk, dv, dk, dv, bias)
    if not need_ctx:
        return yd
    ct = n // TM
    return pl.pallas_call(
        functools.partial(_attn_d_kernel, local=False, aliased=True, n_rows=n_rows),
        grid=(1,),
        in_specs=[
            pl.BlockSpec((TM, gw), lambda i: (ct, 0)),
            pl.BlockSpec((l_ctx, gw), lambda i: (cb, 0)),
            pl.BlockSpec((l_ctx, gw), lambda i: (cb, 0)),
            pl.BlockSpec(memory_space=pl.ANY),
        ],
        out_specs=pl.BlockSpec((TM, gw), lambda i: (ct, 0)),
        out_shape=jax.ShapeDtypeStruct((rows, gw), BF16),
        input_output_aliases={3: 0},
        compiler_params=_params(("parallel",)),
        name="attn_d_ctx",
    )(dq, dk, dv, yd)


def _outproj_kernel(ya_ref, yb_ref, yc_ref, yd_ref, w_ref, x_ref, gt_ref, lng_ref, lnb_ref, o_ref):
    y_in = jnp.concatenate([ya_ref[...], yb_ref[...], yc_ref[...], yd_ref[...]], axis=1)
    y = jnp.dot(y_in, w_ref[...], preferred_element_type=F32)
    t = DEEPNORM_ALPHA * x_ref[...] + gt_ref[0] * y
    o_ref[...] = _ln(t, lng_ref[...], lnb_ref[...])


def _outproj(ya, yb, yc, yd, w_out, layer, x, mods_l, gate_k, lng, lnb, n_tiles, nlt):
    d = D_MODEL
    gw = GROUP_WIDTH
    ytile = pl.BlockSpec((TM, gw), lambda i: (i, 0))
    vec = pl.BlockSpec((1, d), lambda i: (0, 0))
    return pl.pallas_call(
        _outproj_kernel,
        grid=(n_tiles,),
        in_specs=[
            ytile, ytile, ytile, ytile,
            pl.BlockSpec((None, d, d), lambda i: (layer, 0, 0), pipeline_mode=pl.Buffered(1)),
            pl.BlockSpec((TM, d), lambda i: (i, 0)),
            pl.BlockSpec((1, 1, d), lambda i: (jnp.where(i >= nlt, N_MOD, 0) + gate_k, 0, 0)),
            vec, vec,
        ],
        out_specs=pl.BlockSpec((TM, d), lambda i: (i, 0)),
        out_shape=jax.ShapeDtypeStruct((n_tiles * TM, d), F32),
        compiler_params=_params(("parallel",)),
        name="outproj",
    )(ya, yb, yc, yd, w_out, x, mods_l, lng, lnb)


def _rope_tables(n, rows):
    t = jnp.arange(n, dtype=jnp.int32)
    row = (t // GRID_W).astype(F32)
    col = (t % GRID_W).astype(F32)
    n_freq = HEAD_DIM // 4
    inv = ROPE_THETA ** (-jnp.arange(n_freq, dtype=F32) / n_freq)
    ang = jnp.concatenate([row[:, None] * inv, col[:, None] * inv], axis=-1)
    cos, sin = jnp.cos(ang), jnp.sin(ang)
    cos_t = jnp.tile(jnp.concatenate([cos, cos], axis=-1), (1, LANES // HEAD_DIM))
    sin_t = jnp.tile(jnp.concatenate([-sin, sin], axis=-1), (1, LANES // HEAD_DIM))
    pad = rows - n
    cos_t = jnp.concatenate([cos_t, jnp.ones((pad, LANES), F32)], axis=0)
    sin_t = jnp.concatenate([sin_t, jnp.zeros((pad, LANES), F32)], axis=0)
    return cos_t, sin_t


def _na_bias(rpb):
    w = GRID_W
    cq = jnp.arange(w)
    cs = jnp.clip(cq - NA_COLS // 2, 0, w - NA_COLS)
    col_ok = (cq[None, :] >= cs[:, None]) & (cq[None, :] < cs[:, None] + NA_COLS)
    edge = w - NA_COLS
    ext = jnp.pad(rpb.astype(F32), ((0, 0), (0, 0), (edge, edge)), mode="edge")
    toep = jnp.stack([ext[:, :, w - 1 - q:2 * w - 1 - q] for q in range(w)], axis=2)
    toep = jnp.where(col_ok[None, None], toep, NEG_INF)
    per_shift = []
    for shift in range(NA_ROWS):
        sl = toep[:, NA_ROWS - 1 - shift:2 * NA_ROWS - 1 - shift]
        per_shift.append(jnp.swapaxes(sl, 1, 2).reshape(rpb.shape[0], w, NA_ROWS * w))
    return jnp.stack(per_shift, axis=0)


def kernel(x, c, ctx, c_ctx, w_mod, b_mod, ln_g, ln_b, ffn1_w_in, ffn1_w_out, ffn2_w_in, ffn2_w_out, mix_w_in, mix_w_out, a_lambda, b_norm_g, b_norm_b, b_spatial_w, b_spatial_b, c_sink, d_rpb):
    depth = w_mod.shape[0]
    n = x.shape[1]
    l_ctx = ctx.shape[1]
    d = D_MODEL
    assert x.shape[0] == 1 and n % TM == 0 and n // GRID_W >= NA_ROWS and n % 1024 == 0
    assert l_ctx % CHUNK == 0 and l_ctx <= TM and n % l_ctx == 0
    nlt = n // TM
    rows = n + TM

    xs = jnp.concatenate([x[0], ctx[0], jnp.zeros((rows - n - l_ctx, d), F32)], axis=0)
    cc = jnp.concatenate([c, c_ctx[None], jnp.zeros((6, d), F32)], axis=0)
    mods = _mod_vectors(cc, w_mod, b_mod)[:, :2].reshape(depth, 2 * N_MOD, 1, d)
    cos_t, sin_t = _rope_tables(n, rows)
    w1_in, w1_out, w2_in, w2_out, wm_in, wm_out = (
        w.astype(BF16) for w in (ffn1_w_in, ffn1_w_out, ffn2_w_in, ffn2_w_out, mix_w_in, mix_w_out))

    for l in range(depth):
        last = l == depth - 1
        lam_init = 0.8 - 0.6 * math.exp(-0.3 * l)
        mods_l = mods[l]
        lng = [ln_g[l, k][None] for k in range(3)]
        lnb = [ln_b[l, k][None] for k in range(3)]

        xs = _ffn(xs, mods_l, (0, 1, 2), lng[0], lnb[0], w1_in, w1_out, l, rows, n)
        aq, ak, av, bu, bv, cq, ckv, dq, dk, dv = _inproj(xs, mods_l, 3, 4, wm_in, l, cos_t, sin_t, nlt)
        ya = _attn_a(a_lambda[l], aq, ak, av, n, l_ctx, lam_init, not last)
        yb = _mixb(bu, bv, b_norm_g[l][None], b_norm_b[l][None], b_spatial_w[l], b_spatial_b[l].T)
        yc = _attn_c(c_sink[l], cq, ckv, n, l_ctx, not last)
        yd = _attn_d(dq, dk, dv, _na_bias(d_rpb[l]), n, l_ctx, not last)
        n_tiles = nlt if last else nlt + 1
        xs = _outproj(ya, yb, yc, yd, wm_out, l, xs, mods_l, 5, lng[1], lnb[1], n_tiles, nlt)
        xs = _ffn(xs, mods_l, (6, 7, 8), lng[2], lnb[2], w2_in, w2_out, l, n_tiles * TM, n)
    return xs[None]
```

```python
import functools
import math

import jax
import jax.numpy as jnp
from jax import lax
from jax.experimental import pallas as pl
from jax.experimental.pallas import tpu as pltpu

F32 = jnp.float32
BF16 = jnp.bfloat16

D_MODEL = 2048
N_GROUPS = 4
GROUP_WIDTH = D_MODEL // N_GROUPS
HEAD_DIM = 64
GRID_W = 64
CHUNK = 128
B_GROUPS = GROUP_WIDTH // 128
A_HEADS = GROUP_WIDTH // (2 * HEAD_DIM)
C_HEADS = GROUP_WIDTH // HEAD_DIM
C_KV_HEADS = C_HEADS // 4
D_HEADS = GROUP_WIDTH // HEAD_DIM
WINDOW = 128
QBLOCK = 128
NA_ROWS = 8
NA_COLS = 16
D_FF = 256 * math.ceil(8 * D_MODEL / 3 / 256)
N_MOD = 9
ROPE_THETA = 10000.0
LN_EPS = 1e-6
NEG_INF = -1e30
MODEL_DEPTH = 4
DEEPNORM_ALPHA = (2 * MODEL_DEPTH) ** 0.25
QK_SCALE = HEAD_DIM ** -0.5
LOG2E = math.log2(math.e)
A_VROWS = 2 * HEAD_DIM + 16
A_UNROLL = 16
D_ROWS = 4

LANES = 128
TM = 512
TM_FFN = 1024
TF = 512
SEG = 512
N_SEG = 10
VMEM_LIMIT = 60 * 1024 * 1024


def _params(sem):
    return pltpu.CompilerParams(dimension_semantics=sem, vmem_limit_bytes=VMEM_LIMIT)


def _ln(t, g, b):
    mu = jnp.mean(t, axis=-1, keepdims=True)
    tc = t - mu
    var = jnp.mean(tc * tc, axis=-1, keepdims=True)
    return tc * lax.rsqrt(var + LN_EPS) * g + b


def _dot_nt(a, b):
    return lax.dot_general(a, b, (((1,), (1,)), ((), ())), preferred_element_type=F32)


def _half_masks(shape):
    lane = lax.broadcasted_iota(jnp.int32, shape, 1)
    return lane < HEAD_DIM, lane >= HEAD_DIM


def _mod_kernel(cc_ref, w_ref, b_ref, o_ref):
    a = cc_ref[...]
    s = (a * jax.nn.sigmoid(a)).astype(BF16)
    o_ref[0] = jnp.dot(s, w_ref[0].astype(BF16), preferred_element_type=F32) + b_ref[0]


def _mod_vectors(cc, w_mod, b_mod):
    depth, d, nm = w_mod.shape
    tn = 1024
    return pl.pallas_call(
        _mod_kernel,
        grid=(depth, nm // tn),
        in_specs=[
            pl.BlockSpec((8, d), lambda l, j: (0, 0)),
            pl.BlockSpec((1, d, tn), lambda l, j: (l, 0, j)),
            pl.BlockSpec((1, 1, tn), lambda l, j: (l, 0, j)),
        ],
        out_specs=pl.BlockSpec((1, 8, tn), lambda l, j: (l, 0, j)),
        out_shape=jax.ShapeDtypeStruct((depth, 8, nm), F32),
        compiler_params=_params(("parallel", "parallel")),
        name="mod_vectors",
    )(cc, w_mod, b_mod.reshape(depth, 1, nm))


def _ffn_kernel(x_ref, sh_ref, sc_ref, gt_ref, lng_ref, lnb_ref, wg_ref, wu_ref, wo_ref, o_ref):
    j = pl.program_id(1)

    @pl.when(j == 0)
    def _():
        o_ref[...] = jnp.zeros_like(o_ref)

    half = x_ref.shape[0] // 2
    for r in range(2):
        rows = slice(r * half, (r + 1) * half)
        xin = (x_ref[rows] * (1.0 + sc_ref[0]) + sh_ref[0]).astype(BF16)
        g = jnp.dot(xin, wg_ref[...], preferred_element_type=F32)
        u = jnp.dot(xin, wu_ref[...], preferred_element_type=F32)
        hh = ((g * jax.nn.sigmoid(g)) * u).astype(BF16)
        o_ref[rows] += jnp.dot(hh, wo_ref[...], preferred_element_type=F32)

    @pl.when(j == pl.num_programs(1) - 1)
    def _():
        for r in range(2):
            rows = slice(r * half, (r + 1) * half)
            t = DEEPNORM_ALPHA * x_ref[rows] + (0.5 * gt_ref[0]) * o_ref[rows]
            o_ref[rows] = _ln(t, lng_ref[...], lnb_ref[...])


def _ffn_kernel_aliased(*refs):
    _ffn_kernel(*refs[:-2], refs[-1])


def _ffn(x, mods_l, ks, lng, lnb, w_in, w_out, layer, n, with_ctx):
    d = D_MODEL
    nj = D_FF // TF
    assert n % TM_FFN == 0 and n % TM == 0
    out_rows = n + TM if with_ctx else n

    def call(tm, tile0, n_tiles, mod0, alias):
        mod_spec = lambda k: pl.BlockSpec((1, 1, d), lambda i, j: (mod0 + k, 0, 0))
        vec_spec = pl.BlockSpec((1, d), lambda i, j: (0, 0))
        in_specs = [
            pl.BlockSpec((tm, d), lambda i, j: (tile0 + i, 0)),
            mod_spec(ks[0]), mod_spec(ks[1]), mod_spec(ks[2]),
            vec_spec, vec_spec,
            pl.BlockSpec((None, d, TF), lambda i, j: (layer, 0, j)),
            pl.BlockSpec((None, d, TF), lambda i, j: (layer, 0, nj + j)),
            pl.BlockSpec((None, TF, d), lambda i, j: (layer, j, 0)),
        ]
        args = [x, mods_l, mods_l, mods_l, lng, lnb, w_in, w_in, w_out]
        kernel_fn = _ffn_kernel
        aliases = {}
        if alias is not None:
            in_specs.append(pl.BlockSpec(memory_space=pl.ANY))
            args.append(alias)
            aliases = {len(args) - 1: 0}
            kernel_fn = _ffn_kernel_aliased
        return pl.pallas_call(
            kernel_fn,
            grid=(n_tiles, nj),
            in_specs=in_specs,
            out_specs=pl.BlockSpec((tm, d), lambda i, j: (tile0 + i, 0)),
            out_shape=jax.ShapeDtypeStruct((out_rows, d), F32),
            input_output_aliases=aliases,
            compiler_params=_params(("parallel", "arbitrary")),
            name="ffn" if alias is None else "ffn_ctx",
        )(*args)

    y = call(TM_FFN, 0, n // TM_FFN, 0, None)
    if with_ctx:
        y = call(TM, n // TM, 1, N_MOD, y)
    return y


def _rope(t, cos_ref, sin_ref):
    w = t.shape[1]
    lane = lax.broadcasted_iota(jnp.int32, t.shape, 1)
    first = (lane % HEAD_DIM) < (HEAD_DIM // 2)
    rot = jnp.where(first, pltpu.roll(t, w - HEAD_DIM // 2, 1), pltpu.roll(t, HEAD_DIM // 2, 1))
    reps = w // LANES
    cos = jnp.concatenate([cos_ref[...]] * reps, axis=1) if reps > 1 else cos_ref[...]
    sin = jnp.concatenate([sin_ref[...]] * reps, axis=1) if reps > 1 else sin_ref[...]
    return t * cos + rot * sin


_SEG_AQ, _SEG_AK, _SEG_AV, _SEG_BU, _SEG_BV, _SEG_CQ, _SEG_CKV, _SEG_DQ, _SEG_DK, _SEG_DV = range(N_SEG)
_SEG_WIDTHS = tuple(2 * C_KV_HEADS * HEAD_DIM if k == _SEG_CKV else GROUP_WIDTH for k in range(N_SEG))
_SEG_OFFSETS = tuple(sum(_SEG_WIDTHS[:k]) for k in range(N_SEG))
D_PROJ = sum(_SEG_WIDTHS)


def _dup_halves(x):
    swapped = pltpu.roll(x, HEAD_DIM, 1)
    first, _ = _half_masks(x.shape)
    return jnp.where(first, x, swapped), jnp.where(first, swapped, x)


def _inproj_kernel(x_ref, sh_ref, sc_ref, w_ref, cos_ref, sin_ref, *out_refs):
    h = (x_ref[...] * (1.0 + sc_ref[0]) + sh_ref[0]).astype(BF16)

    def project(k):
        return jnp.dot(h, w_ref[:, _SEG_OFFSETS[k]:_SEG_OFFSETS[k] + _SEG_WIDTHS[k]], preferred_element_type=F32)

    def finish(k, t):
        o_ref = out_refs[k]
        if k in (_SEG_AQ, _SEG_AK, _SEG_CQ):
            t = _rope(t, cos_ref, sin_ref)
        if k in (_SEG_AQ, _SEG_CQ, _SEG_DQ):
            t = t * (QK_SCALE * LOG2E)
        if k == _SEG_AQ:
            o_ref[0] = t.T.astype(o_ref.dtype)
        elif k == _SEG_AV:
            tt = t.T.astype(o_ref.dtype)
            extra = A_VROWS - 2 * HEAD_DIM
            ones_row = (lax.broadcasted_iota(jnp.int32, (extra, tt.shape[1]), 0) == 0).astype(o_ref.dtype)
            for hd in range(A_HEADS):
                o_ref[0, hd * A_VROWS:hd * A_VROWS + 2 * HEAD_DIM] = tt[hd * 2 * HEAD_DIM:(hd + 1) * 2 * HEAD_DIM]
                o_ref[0, hd * A_VROWS + 2 * HEAD_DIM:(hd + 1) * A_VROWS] = ones_row
        elif k == _SEG_CKV:
            kk = _dup_halves(_rope(t[:, :LANES], cos_ref, sin_ref))
            vv = _dup_halves(t[:, LANES:])
            o_ref[...] = jnp.concatenate([kk[0], kk[1], vv[0], vv[1]], axis=1).astype(o_ref.dtype)
        else:
            o_ref[...] = t.astype(o_ref.dtype)

    t = project(0)
    for k in range(N_SEG):
        t_next = project(k + 1) if k + 1 < N_SEG else None
        finish(k, t)
        t = t_next


def _inproj(x, mods_l, shift_k, scale_k, w_in, layer, cos_t, sin_t, nlt):
    rows = x.shape[0]

    def mod_spec(k):
        return pl.BlockSpec((1, 1, D_MODEL), lambda i: (jnp.where(i >= nlt, N_MOD, 0) + k, 0, 0))

    n_tiles = rows // TM
    d = D_MODEL
    assert C_KV_HEADS == 2 and w_in.shape[1:] == (d, D_PROJ)
    widths = [SEG] * N_SEG
    dtypes = [BF16] * N_SEG
    dtypes[_SEG_BU] = F32
    dtypes[_SEG_BV] = F32
    out_specs = [pl.BlockSpec((TM, w), lambda i: (i, 0)) for w in widths]
    out_shape = [jax.ShapeDtypeStruct((rows, w), dt) for w, dt in zip(widths, dtypes)]
    for k in (_SEG_AQ, _SEG_AV):
        slab = SEG if k == _SEG_AQ else A_HEADS * A_VROWS
        out_specs[k] = pl.BlockSpec((1, slab, TM), lambda i: (i, 0, 0))
        out_shape[k] = jax.ShapeDtypeStruct((n_tiles, slab, TM), BF16)
    return pl.pallas_call(
        _inproj_kernel,
        grid=(n_tiles,),
        in_specs=[
            pl.BlockSpec((TM, d), lambda i: (i, 0)),
            mod_spec(shift_k), mod_spec(scale_k),
            pl.BlockSpec((None, d, D_PROJ), lambda i: (layer, 0, 0), pipeline_mode=pl.Buffered(1)),
            pl.BlockSpec((TM, LANES), lambda i: (i, 0)),
            pl.BlockSpec((TM, LANES), lambda i: (i, 0)),
        ],
        out_specs=out_specs,
        out_shape=out_shape,
        compiler_params=_params(("parallel",)),
        name="inproj",
    )(x, mods_l, mods_l, w_in, cos_t, sin_t)


def _attn_a_kernel(lam_ref, qt_ref, k_ref, vt_ref, *rest, n_chunks, tail_k_start, tail_chunk, tail_len, lam_init,
                   aliased):
    if aliased:
        rest = rest[1:]
    o_ref, m_scr, acc_scr, s_scr, mc_scr = rest
    qt = qt_ref[0]
    comp = lax.broadcasted_iota(jnp.int32, qt.shape, 0) < HEAD_DIM
    zero = jnp.zeros_like(qt)
    qts = (jnp.where(comp, qt, zero), jnp.where(comp, zero, qt))
    m_scr[...] = jnp.full(m_scr.shape, NEG_INF, F32)
    acc_scr[...] = jnp.zeros_like(acc_scr)

    def scores(k, slot, size):
        for c in range(2):
            s = jnp.dot(k, qts[c], preferred_element_type=F32)
            s_scr[slot, c, :size] = s
            mc_scr[slot, c] = jnp.max(s, axis=0, keepdims=True)

    def consume(slot, vt, size):
        for c in range(2):
            m_old = m_scr[c]
            m_new = jnp.maximum(m_old, mc_scr[slot, c])
            alpha = jnp.exp2(m_old - m_new)
            p = jnp.exp2(s_scr[slot, c, :size] - m_new).astype(BF16)
            acc_scr[c] = alpha * acc_scr[c] + jnp.dot(vt, p, preferred_element_type=F32)
            m_scr[c] = m_new

    def k_chunk(i):
        i = jnp.minimum(i, n_chunks - 1)
        return k_ref[pl.ds(pl.multiple_of(i * TM, TM), TM), :]

    if n_chunks:
        unroll = math.gcd(n_chunks, A_UNROLL)
        assert unroll % 2 == 0
        scores(k_chunk(0), 0, TM)

        def body(t, carry):
            j = unroll * t
            for u in range(unroll):
                scores(k_chunk(j + u + 1), (u + 1) % 2, TM)
                consume(u % 2, vt_ref[j + u], TM)
            return carry
        lax.fori_loop(0, n_chunks // unroll, body, 0)
    scores(k_ref[pl.ds(tail_k_start, tail_len), :], 0, tail_len)
    consume(0, vt_ref[tail_chunk][:, :tail_len], tail_len)

    lv = lam_ref[...]
    lam = (jnp.exp(jnp.sum(lv[0:1] * lv[1:2], axis=1, keepdims=True))
           - jnp.exp(jnp.sum(lv[2:3] * lv[3:4], axis=1, keepdims=True)) + lam_init)
    nv = 2 * HEAD_DIM
    ot = (acc_scr[0, :nv] / acc_scr[0, nv:nv + 1]
          - lam * (acc_scr[1, :nv] / acc_scr[1, nv:nv + 1]))
    ms = jnp.mean(ot * ot, axis=0, keepdims=True)
    ot = (ot * lax.rsqrt(ms + LN_EPS)) * (1.0 - lam_init)
    o_ref[...] = ot.T.astype(BF16)


def _attn_a(lam_vecs, aqt, ak, avt, n, l_ctx, lam_init, need_ctx):
    rows = ak.shape[0]
    nt = rows // TM
    nlt = n // TM
    scratch = [pltpu.VMEM((2, 1, TM), F32), pltpu.VMEM((2, A_VROWS, TM), F32),
               pltpu.VMEM((2, 2, TM, TM), F32), pltpu.VMEM((2, 2, 1, TM), F32)]
    lam_spec = pl.BlockSpec((4, HEAD_DIM), lambda h, i: (0, 0))
    ya = pl.pallas_call(
        functools.partial(_attn_a_kernel, n_chunks=nlt, tail_k_start=n, tail_chunk=nlt, tail_len=l_ctx,
                          lam_init=lam_init, aliased=False),
        grid=(A_HEADS, nlt),
        in_specs=[
            lam_spec,
            pl.BlockSpec((1, LANES, TM), lambda h, i: (i, h, 0)),
            pl.BlockSpec((rows, LANES), lambda h, i: (0, h)),
            pl.BlockSpec((nt, A_VROWS, TM), lambda h, i: (0, h, 0)),
        ],
        out_specs=pl.BlockSpec((TM, LANES), lambda h, i: (i, h)),
        out_shape=jax.ShapeDtypeStruct((rows, GROUP_WIDTH), BF16),
        scratch_shapes=scratch,
        compiler_params=_params(("parallel", "parallel")),
        name="attn_a",
    )(lam_vecs, aqt, ak, avt)
    if not need_ctx:
        return ya
    cb = n // l_ctx
    return pl.pallas_call(
        functools.partial(_attn_a_kernel, n_chunks=0, tail_k_start=0, tail_chunk=0, tail_len=l_ctx,
                          lam_init=lam_init, aliased=True),
        grid=(A_HEADS, 1),
        in_specs=[
            lam_spec,
            pl.BlockSpec((1, LANES, TM), lambda h, i: (nlt, h, 0)),
            pl.BlockSpec((l_ctx, LANES), lambda h, i: (cb, h)),
            pl.BlockSpec((1, A_VROWS, TM), lambda h, i: (nlt, h, 0)),
            pl.BlockSpec(memory_space=pl.ANY),
        ],
        out_specs=pl.BlockSpec((TM, LANES), lambda h, i: (nlt, h)),
        out_shape=jax.ShapeDtypeStruct((rows, GROUP_WIDTH), BF16),
        scratch_shapes=scratch,
        input_output_aliases={4: 0},
        compiler_params=_params(("parallel", "parallel")),
        name="attn_a_ctx",
    )(lam_vecs, aqt, ak, avt, ya)


def _mixb_kernel(u_ref, v_ref, g_ref, b_ref, ws_ref, bs_ref, o_ref):
    u = jax.nn.gelu(u_ref[...])
    v = _ln(jax.nn.gelu(v_ref[...]), g_ref[...], b_ref[...]).astype(BF16)
    for c in range(u.shape[0] // CHUNK):
        rs = slice(c * CHUNK, (c + 1) * CHUNK)
        for g in range(B_GROUPS):
            cs = slice(g * LANES, (g + 1) * LANES)
            mixed = jnp.dot(ws_ref[g].astype(BF16), v[rs, cs], preferred_element_type=F32) + bs_ref[:, g:g + 1]
            o_ref[rs, cs] = (u[rs, cs] * mixed).astype(BF16)


def _mixb(bu, bv, gn_g, gn_b, w_s, b_s_t):
    rows = bu.shape[0]
    gw = GROUP_WIDTH
    tile = pl.BlockSpec((TM, gw), lambda i: (i, 0))
    return pl.pallas_call(
        _mixb_kernel,
        grid=(rows // TM,),
        in_specs=[
            tile, tile,
            pl.BlockSpec((1, gw), lambda i: (0, 0)),
            pl.BlockSpec((1, gw), lambda i: (0, 0)),
            pl.BlockSpec((B_GROUPS, CHUNK, CHUNK), lambda i: (0, 0, 0)),
            pl.BlockSpec((CHUNK, B_GROUPS), lambda i: (0, 0)),
        ],
        out_specs=tile,
        out_shape=jax.ShapeDtypeStruct((rows, gw), BF16),
        compiler_params=_params(("parallel",)),
        name="mix_b",
    )(bu, bv, gn_g, gn_b, w_s, b_s_t)


def _paired_heads_attention(score_fn, values, n_pairs, sinks=None):
    all_scores = [score_fn(c, half) for c in range(n_pairs) for half in range(2)]
    all_probs = []
    for idx, scores in enumerate(all_scores):
        m = jnp.max(scores[0], axis=1, keepdims=True)
        for s in scores[1:]:
            m = jnp.maximum(m, jnp.max(s, axis=1, keepdims=True))
        if sinks is not None:
            m = jnp.maximum(m, sinks[idx])
        ps = [jnp.exp2(s - m) for s in scores]
        den = jnp.sum(ps[0], axis=1, keepdims=True)
        for p in ps[1:]:
            den = den + jnp.sum(p, axis=1, keepdims=True)
        if sinks is not None:
            den = den + jnp.exp2(sinks[idx] - m)
        all_probs.append(([p.astype(BF16) for p in ps], den))
    outs = []
    for c in range(n_pairs):
        out = None
        for half in range(2):
            ps, den = all_probs[2 * c + half]
            num = None
            for p, v in zip(ps, values[c]):
                first, second = _half_masks(v.shape)
                pv = jnp.dot(p, jnp.where(first if half == 0 else second, v, jnp.zeros_like(v)),
                             preferred_element_type=F32)
                num = pv if num is None else num + pv
            o = num / den
            out = o if out is None else out + o
        outs.append(out)
    return outs


def _attn_c_kernel(sink_ref, q_ref, k_ref, v_ref, *rest, n, l_ctx, local, aliased):
    o_ref = rest[-1]
    i = pl.program_id(0)
    n_sub = q_ref.shape[0] // QBLOCK
    wlen = 3 * QBLOCK
    ctx_start = n if local else 0
    n_pairs = C_HEADS // 2
    grp_pairs = n_pairs // C_KV_HEADS
    kv_block = lambda c: slice((c // grp_pairs) * LANES, (c // grp_pairs + 1) * LANES)
    sinks = [sink_ref[2 * c + half] * LOG2E for _ in range(n_sub) for c in range(n_pairs) for half in range(2)]
    q_rows = [slice(b * QBLOCK, (b + 1) * QBLOCK) for b in range(n_sub)]
    if local:
        starts, valids = [], []
        for b in range(n_sub):
            blk = i * n_sub + b
            start = pl.multiple_of(jnp.clip((blk - 1) * QBLOCK, 0, n - wlen), QBLOCK)
            qpos = blk * QBLOCK + lax.broadcasted_iota(jnp.int32, (QBLOCK, wlen), 0)
            kpos = start + lax.broadcasted_iota(jnp.int32, (QBLOCK, wlen), 1)
            starts.append(start)
            valids.append(jnp.abs(kpos - qpos) <= WINDOW)

    def score_fn(idx, half):
        b, c = divmod(idx, n_pairs)
        qb = q_ref[q_rows[b], c * LANES:(c + 1) * LANES]
        first, second = _half_masks(qb.shape)
        qm = jnp.where(first if half == 0 else second, qb, jnp.zeros_like(qb))
        scores = [_dot_nt(qm, k_ref[pl.ds(ctx_start, l_ctx), kv_block(c)])]
        if local:
            scores.append(jnp.where(valids[b], _dot_nt(qm, k_ref[pl.ds(starts[b], wlen), kv_block(c)]), NEG_INF))
        return scores

    values = []
    for b in range(n_sub):
        for c in range(n_pairs):
            vals = [v_ref[pl.ds(ctx_start, l_ctx), kv_block(c)]]
            if local:
                vals.append(v_ref[pl.ds(starts[b], wlen), kv_block(c)])
            values.append(vals)
    outs = _paired_heads_attention(score_fn, values, n_sub * n_pairs, sinks)
    for idx, out in enumerate(outs):
        b, c = divmod(idx, n_pairs)
        o_ref[q_rows[b], c * LANES:(c + 1) * LANES] = out.astype(BF16)


def _attn_c(sink, cq, ckv, n, l_ctx, need_ctx):
    rows = cq.shape[0]
    gw = GROUP_WIDTH
    kvw = C_KV_HEADS * LANES
    smem = pl.BlockSpec(memory_space=pltpu.SMEM)
    yc = pl.pallas_call(
        functools.partial(_attn_c_kernel, n=n, l_ctx=l_ctx, local=True, aliased=False),
        grid=(n // TM,),
        in_specs=[
            smem,
            pl.BlockSpec((TM, gw), lambda i: (i, 0)),
            pl.BlockSpec((rows, kvw), lambda i: (0, 0)),
            pl.BlockSpec((rows, kvw), lambda i: (0, 1)),
        ],
        out_specs=pl.BlockSpec((TM, gw), lambda i: (i, 0)),
        out_shape=jax.ShapeDtypeStruct((rows, gw), BF16),
        compiler_params=_params(("parallel",)),
        name="attn_c",
    )(sink, cq, ckv, ckv)
    if not need_ctx:
        return yc
    ct = n // TM
    cb = n // l_ctx
    return pl.pallas_call(
        functools.partial(_attn_c_kernel, n=n, l_ctx=l_ctx, local=False, aliased=True),
        grid=(1,),
        in_specs=[
            smem,
            pl.BlockSpec((TM, gw), lambda i: (ct, 0)),
            pl.BlockSpec((l_ctx, kvw), lambda i: (cb, 0)),
            pl.BlockSpec((l_ctx, kvw), lambda i: (cb, 1)),
            pl.BlockSpec(memory_space=pl.ANY),
        ],
        out_specs=pl.BlockSpec((TM, gw), lambda i: (ct, 0)),
        out_shape=jax.ShapeDtypeStruct((rows, gw), BF16),
        input_output_aliases={4: 0},
        compiler_params=_params(("parallel",)),
        name="attn_c_ctx",
    )(sink, cq, ckv, ckv, yc)


def _attn_d_kernel(q_ref, kc_ref, vc_ref, *rest, local, aliased, n_rows):
    o_ref = rest[-1]
    n_pairs = D_HEADS // 2
    lane_block = lambda c: slice(c * LANES, (c + 1) * LANES)
    nk = NA_ROWS * GRID_W
    if local:
        k_ref, v_ref, b_ref = rest[0], rest[1], rest[2]
        g = pl.program_id(0)
        win0 = jnp.clip(g * D_ROWS - NA_ROWS // 2, 0, n_rows - (D_ROWS + NA_ROWS - 1))
        q_rows, offs, shifts = [], [], []
        for i in range(D_ROWS):
            r = g * D_ROWS + i
            rs = jnp.clip(r - NA_ROWS // 2, 0, n_rows - NA_ROWS)
            q_rows.append(slice(i * GRID_W, (i + 1) * GRID_W))
            offs.append(pl.multiple_of((rs - win0) * GRID_W, GRID_W))
            shifts.append(r - rs)
    else:
        q_rows = [slice(0, q_ref.shape[0])]

    def score_fn(idx, half):
        i, c = divmod(idx, n_pairs)
        qb = q_ref[q_rows[i], lane_block(c)]
        first, second = _half_masks(qb.shape)
        qm = jnp.where(first if half == 0 else second, qb, jnp.zeros_like(qb))
        scores = [_dot_nt(qm, kc_ref[:, lane_block(c)])]
        if local:
            scores.append(_dot_nt(qm, k_ref[pl.ds(offs[i], nk), lane_block(c)]) + b_ref[shifts[i], 2 * c + half])
        return scores

    values = []
    for i in range(len(q_rows)):
        for c in range(n_pairs):
            vals = [vc_ref[:, lane_block(c)]]
            if local:
                vals.append(v_ref[pl.ds(offs[i], nk), lane_block(c)])
            values.append(vals)
    outs = _paired_heads_attention(score_fn, values, len(q_rows) * n_pairs)
    for idx, out in enumerate(outs):
        i, c = divmod(idx, n_pairs)
        o_ref[q_rows[i], lane_block(c)] = out.astype(BF16)


def _attn_d(dq, dk, dv, bias, n, l_ctx, need_ctx):
    rows = dq.shape[0]
    gw = GROUP_WIDTH
    n_rows = n // GRID_W
    nk = NA_ROWS * GRID_W
    cb = n // l_ctx
    win = D_ROWS + NA_ROWS - 1
    assert n_rows % D_ROWS == 0 and n_rows >= win

    def window_start(g):
        return jnp.clip(g * D_ROWS - NA_ROWS // 2, 0, n_rows - win) * GRID_W

    once = dict(pipeline_mode=pl.Buffered(1))
    yd = pl.pallas_call(
        functools.partial(_attn_d_kernel, local=True, aliased=False, n_rows=n_rows),
        grid=(n_rows // D_ROWS,),
        in_specs=[
            pl.BlockSpec((D_ROWS * GRID_W, gw), lambda g: (g, 0)),
            pl.BlockSpec((l_ctx, gw), lambda g: (cb, 0), **once),
            pl.BlockSpec((l_ctx, gw), lambda g: (cb, 0), **once),
            pl.BlockSpec((pl.Element(win * GRID_W), pl.Element(gw)), lambda g: (window_start(g), 0)),
            pl.BlockSpec((pl.Element(win * GRID_W), pl.Element(gw)), lambda g: (window_start(g), 0)),
            pl.BlockSpec((NA_ROWS, D_HEADS, GRID_W, nk), lambda g: (0, 0, 0, 0), **once),
        ],
        out_specs=pl.BlockSpec((D_ROWS * GRID_W, gw), lambda g: (g, 0)),
        out_shape=jax.ShapeDtypeStruct((rows, gw), BF16),
        compiler_params=_params(("parallel",)),
        name="attn_d",
    )(dq, dk, dv, dk, dv, bias)
    if not need_ctx:
        return yd
    ct = n // TM
    return pl.pallas_call(
        functools.partial(_attn_d_kernel, local=False, aliased=True, n_rows=n_rows),
        grid=(1,),
        in_specs=[
            pl.BlockSpec((TM, gw), lambda i: (ct, 0)),
            pl.BlockSpec((l_ctx, gw), lambda i: (cb, 0)),
            pl.BlockSpec((l_ctx, gw), lambda i: (cb, 0)),
            pl.BlockSpec(memory_space=pl.ANY),
        ],
        out_specs=pl.BlockSpec((TM, gw), lambda i: (ct, 0)),
        out_shape=jax.ShapeDtypeStruct((rows, gw), BF16),
        input_output_aliases={3: 0},
        compiler_params=_params(("parallel",)),
        name="attn_d_ctx",
    )(dq, dk, dv, yd)


def _outproj_kernel(ya_ref, yb_ref, yc_ref, yd_ref, w_ref, x_ref, gt_ref, lng_ref, lnb_ref, o_ref):
    y_in = jnp.concatenate([ya_ref[...], yb_ref[...], yc_ref[...], yd_ref[...]], axis=1)
    y = jnp.dot(y_in, w_ref[...], preferred_element_type=F32)
    t = DEEPNORM_ALPHA * x_ref[...] + gt_ref[0] * y
    o_ref[...] = _ln(t, lng_ref[...], lnb_ref[...])


def _outproj(ya, yb, yc, yd, w_out, layer, x, mods_l, gate_k, lng, lnb, n_tiles, nlt):
    d = D_MODEL
    gw = GROUP_WIDTH
    ytile = pl.BlockSpec((TM, gw), lambda i: (i, 0))
    vec = pl.BlockSpec((1, d), lambda i: (0, 0))
    return pl.pallas_call(
        _outproj_kernel,
        grid=(n_tiles,),
        in_specs=[
            ytile, ytile, ytile, ytile,
            pl.BlockSpec((None, d, d), lambda i: (layer, 0, 0), pipeline_mode=pl.Buffered(1)),
            pl.BlockSpec((TM, d), lambda i: (i, 0)),
            pl.BlockSpec((1, 1, d), lambda i: (jnp.where(i >= nlt, N_MOD, 0) + gate_k, 0, 0)),
            vec, vec,
        ],
        out_specs=pl.BlockSpec((TM, d), lambda i: (i, 0)),
        out_shape=jax.ShapeDtypeStruct((n_tiles * TM, d), F32),
        compiler_params=_params(("parallel",)),
        name="outproj",
    )(ya, yb, yc, yd, w_out, x, mods_l, lng, lnb)


def _rope_tables(n, rows):
    t = jnp.arange(n, dtype=jnp.int32)
    row = (t // GRID_W).astype(F32)
    col = (t % GRID_W).astype(F32)
    n_freq = HEAD_DIM // 4
    inv = ROPE_THETA ** (-jnp.arange(n_freq, dtype=F32) / n_freq)
    ang = jnp.concatenate([row[:, None] * inv, col[:, None] * inv], axis=-1)
    cos, sin = jnp.cos(ang), jnp.sin(ang)
    cos_t = jnp.tile(jnp.concatenate([cos, cos], axis=-1), (1, LANES // HEAD_DIM))
    sin_t = jnp.tile(jnp.concatenate([-sin, sin], axis=-1), (1, LANES // HEAD_DIM))
    pad = rows - n
    cos_t = jnp.concatenate([cos_t, jnp.ones((pad, LANES), F32)], axis=0)
    sin_t = jnp.concatenate([sin_t, jnp.zeros((pad, LANES), F32)], axis=0)
    return cos_t, sin_t


def _na_bias(rpb):
    w = GRID_W
    cq = jnp.arange(w)
    cs = jnp.clip(cq - NA_COLS // 2, 0, w - NA_COLS)
    col_ok = (cq[None, :] >= cs[:, None]) & (cq[None, :] < cs[:, None] + NA_COLS)
    edge = w - NA_COLS
    ext = jnp.pad(rpb.astype(F32), ((0, 0), (0, 0), (edge, edge)), mode="edge")
    toep = jnp.stack([ext[:, :, w - 1 - q:2 * w - 1 - q] for q in range(w)], axis=2)
    toep = jnp.where(col_ok[None, None], toep, NEG_INF)
    per_shift = []
    for shift in range(NA_ROWS):
        sl = toep[:, NA_ROWS - 1 - shift:2 * NA_ROWS - 1 - shift]
        per_shift.append(jnp.swapaxes(sl, 1, 2).reshape(rpb.shape[0], w, NA_ROWS * w))
    return jnp.stack(per_shift, axis=0) * LOG2E


def kernel(x, c, ctx, c_ctx, w_mod, b_mod, ln_g, ln_b, ffn1_w_in, ffn1_w_out, ffn2_w_in, ffn2_w_out, mix_w_in, mix_w_out, a_lambda, b_norm_g, b_norm_b, b_spatial_w, b_spatial_b, c_sink, d_rpb):
    depth = w_mod.shape[0]
    n = x.shape[1]
    l_ctx = ctx.shape[1]
    d = D_MODEL
    assert x.shape[0] == 1 and n % TM == 0 and n // GRID_W >= NA_ROWS and n % 1024 == 0
    assert l_ctx % CHUNK == 0 and l_ctx <= TM and n % l_ctx == 0
    nlt = n // TM
    rows = n + TM

    xs = jnp.concatenate([x[0], ctx[0], jnp.zeros((rows - n - l_ctx, d), F32)], axis=0)
    cc = jnp.concatenate([c, c_ctx[None], jnp.zeros((6, d), F32)], axis=0)
    mods = _mod_vectors(cc, w_mod, b_mod)[:, :2].reshape(depth, 2 * N_MOD, 1, d)
    cos_t, sin_t = _rope_tables(n, rows)
    w1_in, w1_out, w2_in, w2_out, wm_in, wm_out = (
        w.astype(BF16) for w in (ffn1_w_in, ffn1_w_out, ffn2_w_in, ffn2_w_out, mix_w_in, mix_w_out))

    for l in range(depth):
        last = l == depth - 1
        lam_init = 0.8 - 0.6 * math.exp(-0.3 * l)
        mods_l = mods[l]
        lng = [ln_g[l, k][None] for k in range(3)]
        lnb = [ln_b[l, k][None] for k in range(3)]

        xs = _ffn(xs, mods_l, (0, 1, 2), lng[0], lnb[0], w1_in, w1_out, l, n, True)
        aq, ak, av, bu, bv, cq, ckv, dq, dk, dv = _inproj(xs, mods_l, 3, 4, wm_in, l, cos_t, sin_t, nlt)
        ya = _attn_a(a_lambda[l], aq, ak, av, n, l_ctx, lam_init, not last)
        yb = _mixb(bu, bv, b_norm_g[l][None], b_norm_b[l][None], b_spatial_w[l], b_spatial_b[l].T)
        yc = _attn_c(c_sink[l], cq, ckv, n, l_ctx, not last)
        yd = _attn_d(dq, dk, dv, _na_bias(d_rpb[l]), n, l_ctx, not last)
        n_tiles = nlt if last else nlt + 1
        xs = _outproj(ya, yb, yc, yd, wm_out, l, xs, mods_l, 5, lng[1], lnb[1], n_tiles, nlt)
        xs = _ffn(xs, mods_l, (6, 7, 8), lng[2], lnb[2], w2_in, w2_out, l, n, not last)
    return xs[None]
```

```python
import functools
import math

import jax
import jax.numpy as jnp
from jax import lax
from jax.experimental import pallas as pl
from jax.experimental.pallas import tpu as pltpu

F32 = jnp.float32
BF16 = jnp.bfloat16

D_MODEL = 2048
N_GROUPS = 4
GROUP_WIDTH = D_MODEL // N_GROUPS
HEAD_DIM = 64
GRID_W = 64
CHUNK = 128
B_GROUPS = GROUP_WIDTH // 128
A_HEADS = GROUP_WIDTH // (2 * HEAD_DIM)
C_HEADS = GROUP_WIDTH // HEAD_DIM
C_KV_HEADS = C_HEADS // 4
D_HEADS = GROUP_WIDTH // HEAD_DIM
WINDOW = 128
QBLOCK = 128
NA_ROWS = 8
NA_COLS = 16
D_FF = 256 * math.ceil(8 * D_MODEL / 3 / 256)
N_MOD = 9
ROPE_THETA = 10000.0
LN_EPS = 1e-6
NEG_INF = -1e30
MODEL_DEPTH = 4
DEEPNORM_ALPHA = (2 * MODEL_DEPTH) ** 0.25
QK_SCALE = HEAD_DIM ** -0.5
LOG2E = math.log2(math.e)
A_VROWS = 2 * HEAD_DIM + 16
A_UNROLL = 16
D_ROWS = 8

LANES = 128
TM = 512
TM_FFN = 1024
TF = 512
SEG = 512
N_SEG = 10
VMEM_LIMIT = 60 * 1024 * 1024


def _params(sem):
    return pltpu.CompilerParams(dimension_semantics=sem, vmem_limit_bytes=VMEM_LIMIT)


def _ln(t, g, b):
    mu = jnp.mean(t, axis=-1, keepdims=True)
    tc = t - mu
    var = jnp.mean(tc * tc, axis=-1, keepdims=True)
    return tc * lax.rsqrt(var + LN_EPS) * g + b


def _dot_nt(a, b):
    return lax.dot_general(a, b, (((1,), (1,)), ((), ())), preferred_element_type=F32)


def _half_masks(shape):
    lane = lax.broadcasted_iota(jnp.int32, shape, 1)
    return lane < HEAD_DIM, lane >= HEAD_DIM


def _mod_kernel(cc_ref, w_ref, b_ref, o_ref):
    a = cc_ref[...]
    s = (a * jax.nn.sigmoid(a)).astype(BF16)
    o_ref[0] = jnp.dot(s, w_ref[0].astype(BF16), preferred_element_type=F32) + b_ref[0]


def _mod_vectors(cc, w_mod, b_mod):
    depth, d, nm = w_mod.shape
    tn = 1024
    return pl.pallas_call(
        _mod_kernel,
        grid=(depth, nm // tn),
        in_specs=[
            pl.BlockSpec((8, d), lambda l, j: (0, 0)),
            pl.BlockSpec((1, d, tn), lambda l, j: (l, 0, j)),
            pl.BlockSpec((1, 1, tn), lambda l, j: (l, 0, j)),
        ],
        out_specs=pl.BlockSpec((1, 8, tn), lambda l, j: (l, 0, j)),
        out_shape=jax.ShapeDtypeStruct((depth, 8, nm), F32),
        compiler_params=_params(("parallel", "parallel")),
        name="mod_vectors",
    )(cc, w_mod, b_mod.reshape(depth, 1, nm))


def _ffn_kernel(x_ref, sh_ref, sc_ref, gt_ref, lng_ref, lnb_ref, wg_ref, wu_ref, wo_ref, o_ref):
    j = pl.program_id(1)

    @pl.when(j == 0)
    def _():
        o_ref[...] = jnp.zeros_like(o_ref)

    half = x_ref.shape[0] // 2
    for r in range(2):
        rows = slice(r * half, (r + 1) * half)
        xin = (x_ref[rows] * (1.0 + sc_ref[0]) + sh_ref[0]).astype(BF16)
        g = jnp.dot(xin, wg_ref[...], preferred_element_type=F32)
        u = jnp.dot(xin, wu_ref[...], preferred_element_type=F32)
        hh = ((g * jax.nn.sigmoid(g)) * u).astype(BF16)
        o_ref[rows] += jnp.dot(hh, wo_ref[...], preferred_element_type=F32)

    @pl.when(j == pl.num_programs(1) - 1)
    def _():
        for r in range(2):
            rows = slice(r * half, (r + 1) * half)
            t = DEEPNORM_ALPHA * x_ref[rows] + (0.5 * gt_ref[0]) * o_ref[rows]
            o_ref[rows] = _ln(t, lng_ref[...], lnb_ref[...])


def _ffn_kernel_aliased(*refs):
    _ffn_kernel(*refs[:-2], refs[-1])


def _ffn(x, x_ctx, ctx_tile, mods_l, ks, lng, lnb, w_in, w_out, layer, n):
    d = D_MODEL
    nj = D_FF // TF
    assert n % TM_FFN == 0 and n % TM == 0
    with_ctx = x_ctx is not None
    out_rows = n + TM if with_ctx else n

    def call(src, tm, tile0, out_tile0, n_tiles, mod0, alias):
        mod_spec = lambda k: pl.BlockSpec((1, 1, d), lambda i, j: (mod0 + k, 0, 0))
        vec_spec = pl.BlockSpec((1, d), lambda i, j: (0, 0))
        in_specs = [
            pl.BlockSpec((tm, d), lambda i, j: (tile0 + i, 0)),
            mod_spec(ks[0]), mod_spec(ks[1]), mod_spec(ks[2]),
            vec_spec, vec_spec,
            pl.BlockSpec((None, d, TF), lambda i, j: (layer, 0, j)),
            pl.BlockSpec((None, d, TF), lambda i, j: (layer, 0, nj + j)),
            pl.BlockSpec((None, TF, d), lambda i, j: (layer, j, 0)),
        ]
        args = [src, mods_l, mods_l, mods_l, lng, lnb, w_in, w_in, w_out]
        kernel_fn = _ffn_kernel
        aliases = {}
        if alias is not None:
            in_specs.append(pl.BlockSpec(memory_space=pl.ANY))
            args.append(alias)
            aliases = {len(args) - 1: 0}
            kernel_fn = _ffn_kernel_aliased
        return pl.pallas_call(
            kernel_fn,
            grid=(n_tiles, nj),
            in_specs=in_specs,
            out_specs=pl.BlockSpec((tm, d), lambda i, j: (out_tile0 + i, 0)),
            out_shape=jax.ShapeDtypeStruct((out_rows, d), F32),
            input_output_aliases=aliases,
            compiler_params=_params(("parallel", "arbitrary")),
            name="ffn" if alias is None else "ffn_ctx",
        )(*args)

    y = call(x, TM_FFN, 0, 0, n // TM_FFN, 0, None)
    if with_ctx:
        y = call(x_ctx, TM, ctx_tile, n // TM, 1, N_MOD, y)
    return y


def _rope(t, cos_ref, sin_ref):
    w = t.shape[1]
    lane = lax.broadcasted_iota(jnp.int32, t.shape, 1)
    first = (lane % HEAD_DIM) < (HEAD_DIM // 2)
    rot = jnp.where(first, pltpu.roll(t, w - HEAD_DIM // 2, 1), pltpu.roll(t, HEAD_DIM // 2, 1))
    reps = w // LANES
    cos = jnp.concatenate([cos_ref[...]] * reps, axis=1) if reps > 1 else cos_ref[...]
    sin = jnp.concatenate([sin_ref[...]] * reps, axis=1) if reps > 1 else sin_ref[...]
    return t * cos + rot * sin


_SEG_AQ, _SEG_AK, _SEG_AV, _SEG_BU, _SEG_BV, _SEG_CQ, _SEG_CKV, _SEG_DQ, _SEG_DK, _SEG_DV = range(N_SEG)
_SEG_WIDTHS = tuple(2 * C_KV_HEADS * HEAD_DIM if k == _SEG_CKV else GROUP_WIDTH for k in range(N_SEG))
_SEG_OFFSETS = tuple(sum(_SEG_WIDTHS[:k]) for k in range(N_SEG))
D_PROJ = sum(_SEG_WIDTHS)


def _dup_halves(x):
    swapped = pltpu.roll(x, HEAD_DIM, 1)
    first, _ = _half_masks(x.shape)
    return jnp.where(first, x, swapped), jnp.where(first, swapped, x)


def _inproj_kernel(x_ref, sh_ref, sc_ref, w_ref, cos_ref, sin_ref, *out_refs):
    h = (x_ref[...] * (1.0 + sc_ref[0]) + sh_ref[0]).astype(BF16)

    def project(k):
        return jnp.dot(h, w_ref[:, _SEG_OFFSETS[k]:_SEG_OFFSETS[k] + _SEG_WIDTHS[k]], preferred_element_type=F32)

    def finish(k, t):
        o_ref = out_refs[k]
        if k in (_SEG_AQ, _SEG_AK, _SEG_CQ):
            t = _rope(t, cos_ref, sin_ref)
        if k in (_SEG_AQ, _SEG_CQ, _SEG_DQ):
            t = t * (QK_SCALE * LOG2E)
        if k == _SEG_AQ:
            o_ref[0] = t.T.astype(o_ref.dtype)
        elif k == _SEG_AV:
            tt = t.T.astype(o_ref.dtype)
            extra = A_VROWS - 2 * HEAD_DIM
            ones_row = (lax.broadcasted_iota(jnp.int32, (extra, tt.shape[1]), 0) == 0).astype(o_ref.dtype)
            for hd in range(A_HEADS):
                o_ref[0, hd * A_VROWS:hd * A_VROWS + 2 * HEAD_DIM] = tt[hd * 2 * HEAD_DIM:(hd + 1) * 2 * HEAD_DIM]
                o_ref[0, hd * A_VROWS + 2 * HEAD_DIM:(hd + 1) * A_VROWS] = ones_row
        elif k == _SEG_CKV:
            kk = _dup_halves(_rope(t[:, :LANES], cos_ref, sin_ref))
            vv = _dup_halves(t[:, LANES:])
            o_ref[...] = jnp.concatenate([kk[0], kk[1], vv[0], vv[1]], axis=1).astype(o_ref.dtype)
        else:
            o_ref[...] = t.astype(o_ref.dtype)

    t = project(0)
    for k in range(N_SEG):
        t_next = project(k + 1) if k + 1 < N_SEG else None
        finish(k, t)
        t = t_next


def _inproj(x, mods_l, shift_k, scale_k, w_in, layer, cos_t, sin_t, nlt):
    rows = x.shape[0]

    def mod_spec(k):
        return pl.BlockSpec((1, 1, D_MODEL), lambda i: (jnp.where(i >= nlt, N_MOD, 0) + k, 0, 0))

    n_tiles = rows // TM
    d = D_MODEL
    assert C_KV_HEADS == 2 and w_in.shape[1:] == (d, D_PROJ)
    widths = [SEG] * N_SEG
    dtypes = [BF16] * N_SEG
    dtypes[_SEG_BU] = F32
    dtypes[_SEG_BV] = F32
    out_specs = [pl.BlockSpec((TM, w), lambda i: (i, 0)) for w in widths]
    out_shape = [jax.ShapeDtypeStruct((rows, w), dt) for w, dt in zip(widths, dtypes)]
    for k in (_SEG_AQ, _SEG_AV):
        slab = SEG if k == _SEG_AQ else A_HEADS * A_VROWS
        out_specs[k] = pl.BlockSpec((1, slab, TM), lambda i: (i, 0, 0))
        out_shape[k] = jax.ShapeDtypeStruct((n_tiles, slab, TM), BF16)
    return pl.pallas_call(
        _inproj_kernel,
        grid=(n_tiles,),
        in_specs=[
            pl.BlockSpec((TM, d), lambda i: (i, 0)),
            mod_spec(shift_k), mod_spec(scale_k),
            pl.BlockSpec((None, d, D_PROJ), lambda i: (layer, 0, 0), pipeline_mode=pl.Buffered(1)),
            pl.BlockSpec((TM, LANES), lambda i: (i, 0)),
            pl.BlockSpec((TM, LANES), lambda i: (i, 0)),
        ],
        out_specs=out_specs,
        out_shape=out_shape,
        compiler_params=_params(("parallel",)),
        name="inproj",
    )(x, mods_l, mods_l, w_in, cos_t, sin_t)


def _attn_a_kernel(lam_ref, qt_ref, k_ref, vt_ref, *rest, n_chunks, tail_k_start, tail_chunk, tail_len, lam_init,
                   aliased):
    if aliased:
        rest = rest[1:]
    o_ref, m_scr, acc_scr, s_scr, mc_scr = rest
    qt = qt_ref[0]
    comp = lax.broadcasted_iota(jnp.int32, qt.shape, 0) < HEAD_DIM
    zero = jnp.zeros_like(qt)
    qts = (jnp.where(comp, qt, zero), jnp.where(comp, zero, qt))
    m_scr[...] = jnp.full(m_scr.shape, NEG_INF, F32)
    acc_scr[...] = jnp.zeros_like(acc_scr)

    def scores(k, slot, size):
        for c in range(2):
            s = jnp.dot(k, qts[c], preferred_element_type=F32)
            s_scr[slot, c, :size] = s
            mc_scr[slot, c] = jnp.max(s, axis=0, keepdims=True)

    def consume(slot, vt, size):
        for c in range(2):
            m_old = m_scr[c]
            m_new = jnp.maximum(m_old, mc_scr[slot, c])
            alpha = jnp.exp2(m_old - m_new)
            p = jnp.exp2(s_scr[slot, c, :size] - m_new).astype(BF16)
            acc_scr[c] = alpha * acc_scr[c] + jnp.dot(vt, p, preferred_element_type=F32)
            m_scr[c] = m_new

    def k_chunk(i):
        i = jnp.minimum(i, n_chunks - 1)
        return k_ref[pl.ds(pl.multiple_of(i * TM, TM), TM), :]

    scores(k_ref[pl.ds(tail_k_start, tail_len), :], 1, tail_len)
    if n_chunks:
        unroll = math.gcd(n_chunks, A_UNROLL)
        assert unroll % 2 == 0
        scores(k_chunk(0), 0, TM)
    consume(1, vt_ref[tail_chunk][:, :tail_len], tail_len)
    if n_chunks:
        def body(t, carry):
            j = unroll * t
            for u in range(unroll):
                scores(k_chunk(j + u + 1), (u + 1) % 2, TM)
                consume(u % 2, vt_ref[j + u], TM)
            return carry
        lax.fori_loop(0, n_chunks // unroll, body, 0)

    lv = lam_ref[...]
    lam = (jnp.exp(jnp.sum(lv[0:1] * lv[1:2], axis=1, keepdims=True))
           - jnp.exp(jnp.sum(lv[2:3] * lv[3:4], axis=1, keepdims=True)) + lam_init)
    nv = 2 * HEAD_DIM
    ot = (acc_scr[0, :nv] / acc_scr[0, nv:nv + 1]
          - lam * (acc_scr[1, :nv] / acc_scr[1, nv:nv + 1]))
    ms = jnp.mean(ot * ot, axis=0, keepdims=True)
    ot = (ot * lax.rsqrt(ms + LN_EPS)) * (1.0 - lam_init)
    o_ref[...] = ot.T.astype(BF16)


def _attn_a(lam_vecs, aqt, ak, avt, n, l_ctx, lam_init, need_ctx):
    rows = ak.shape[0]
    nt = rows // TM
    nlt = n // TM
    scratch = [pltpu.VMEM((2, 1, TM), F32), pltpu.VMEM((2, A_VROWS, TM), F32),
               pltpu.VMEM((2, 2, TM, TM), F32), pltpu.VMEM((2, 2, 1, TM), F32)]
    lam_spec = pl.BlockSpec((4, HEAD_DIM), lambda h, i: (0, 0))
    ya = pl.pallas_call(
        functools.partial(_attn_a_kernel, n_chunks=nlt, tail_k_start=n, tail_chunk=nlt, tail_len=l_ctx,
                          lam_init=lam_init, aliased=False),
        grid=(A_HEADS, nlt),
        in_specs=[
            lam_spec,
            pl.BlockSpec((1, LANES, TM), lambda h, i: (i, h, 0)),
            pl.BlockSpec((rows, LANES), lambda h, i: (0, h)),
            pl.BlockSpec((nt, A_VROWS, TM), lambda h, i: (0, h, 0)),
        ],
        out_specs=pl.BlockSpec((TM, LANES), lambda h, i: (i, h)),
        out_shape=jax.ShapeDtypeStruct((rows, GROUP_WIDTH), BF16),
        scratch_shapes=scratch,
        compiler_params=_params(("parallel", "parallel")),
        name="attn_a",
    )(lam_vecs, aqt, ak, avt)
    if not need_ctx:
        return ya
    cb = n // l_ctx
    return pl.pallas_call(
        functools.partial(_attn_a_kernel, n_chunks=0, tail_k_start=0, tail_chunk=0, tail_len=l_ctx,
                          lam_init=lam_init, aliased=True),
        grid=(A_HEADS, 1),
        in_specs=[
            lam_spec,
            pl.BlockSpec((1, LANES, TM), lambda h, i: (nlt, h, 0)),
            pl.BlockSpec((l_ctx, LANES), lambda h, i: (cb, h)),
            pl.BlockSpec((1, A_VROWS, TM), lambda h, i: (nlt, h, 0)),
            pl.BlockSpec(memory_space=pl.ANY),
        ],
        out_specs=pl.BlockSpec((TM, LANES), lambda h, i: (nlt, h)),
        out_shape=jax.ShapeDtypeStruct((rows, GROUP_WIDTH), BF16),
        scratch_shapes=scratch,
        input_output_aliases={4: 0},
        compiler_params=_params(("parallel", "parallel")),
        name="attn_a_ctx",
    )(lam_vecs, aqt, ak, avt, ya)


def _mixb_kernel(u_ref, v_ref, g_ref, b_ref, ws_ref, bs_ref, o_ref):
    u = jax.nn.gelu(u_ref[...])
    v = _ln(jax.nn.gelu(v_ref[...]), g_ref[...], b_ref[...]).astype(BF16)
    for c in range(u.shape[0] // CHUNK):
        rs = slice(c * CHUNK, (c + 1) * CHUNK)
        for g in range(B_GROUPS):
            cs = slice(g * LANES, (g + 1) * LANES)
            mixed = jnp.dot(ws_ref[g].astype(BF16), v[rs, cs], preferred_element_type=F32) + bs_ref[:, g:g + 1]
            o_ref[rs, cs] = (u[rs, cs] * mixed).astype(BF16)


def _mixb(bu, bv, gn_g, gn_b, w_s, b_s_t):
    rows = bu.shape[0]
    gw = GROUP_WIDTH
    tile = pl.BlockSpec((TM, gw), lambda i: (i, 0))
    return pl.pallas_call(
        _mixb_kernel,
        grid=(rows // TM,),
        in_specs=[
            tile, tile,
            pl.BlockSpec((1, gw), lambda i: (0, 0)),
            pl.BlockSpec((1, gw), lambda i: (0, 0)),
            pl.BlockSpec((B_GROUPS, CHUNK, CHUNK), lambda i: (0, 0, 0)),
            pl.BlockSpec((CHUNK, B_GROUPS), lambda i: (0, 0)),
        ],
        out_specs=tile,
        out_shape=jax.ShapeDtypeStruct((rows, gw), BF16),
        compiler_params=_params(("parallel",)),
        name="mix_b",
    )(bu, bv, gn_g, gn_b, w_s, b_s_t)


def _paired_heads_attention(score_fn, values, n_pairs, sinks=None):
    all_scores = [score_fn(c, half) for c in range(n_pairs) for half in range(2)]
    all_probs = []
    for idx, scores in enumerate(all_scores):
        m = jnp.max(scores[0], axis=1, keepdims=True)
        for s in scores[1:]:
            m = jnp.maximum(m, jnp.max(s, axis=1, keepdims=True))
        if sinks is not None:
            m = jnp.maximum(m, sinks[idx])
        ps = [jnp.exp2(s - m) for s in scores]
        den = jnp.sum(ps[0], axis=1, keepdims=True)
        for p in ps[1:]:
            den = den + jnp.sum(p, axis=1, keepdims=True)
        if sinks is not None:
            den = den + jnp.exp2(sinks[idx] - m)
        all_probs.append(([p.astype(BF16) for p in ps], den))
    outs = []
    for c in range(n_pairs):
        out = None
        for half in range(2):
            ps, den = all_probs[2 * c + half]
            num = None
            for p, v in zip(ps, values[c]):
                first, second = _half_masks(v.shape)
                pv = jnp.dot(p, jnp.where(first if half == 0 else second, v, jnp.zeros_like(v)),
                             preferred_element_type=F32)
                num = pv if num is None else num + pv
            o = num / den
            out = o if out is None else out + o
        outs.append(out)
    return outs


def _attn_c_kernel(sink_ref, q_ref, k_ref, v_ref, *rest, n, l_ctx, local, aliased):
    o_ref = rest[-1]
    i = pl.program_id(0)
    n_sub = q_ref.shape[0] // QBLOCK
    wlen = 3 * QBLOCK
    ctx_start = n if local else 0
    n_pairs = C_HEADS // 2
    grp_pairs = n_pairs // C_KV_HEADS
    kv_block = lambda c: slice((c // grp_pairs) * LANES, (c // grp_pairs + 1) * LANES)
    sinks = [sink_ref[2 * c + half] * LOG2E for _ in range(n_sub) for c in range(n_pairs) for half in range(2)]
    q_rows = [slice(b * QBLOCK, (b + 1) * QBLOCK) for b in range(n_sub)]
    if local:
        starts, valids = [], []
        for b in range(n_sub):
            blk = i * n_sub + b
            start = pl.multiple_of(jnp.clip((blk - 1) * QBLOCK, 0, n - wlen), QBLOCK)
            qpos = blk * QBLOCK + lax.broadcasted_iota(jnp.int32, (QBLOCK, wlen), 0)
            kpos = start + lax.broadcasted_iota(jnp.int32, (QBLOCK, wlen), 1)
            starts.append(start)
            valids.append(jnp.abs(kpos - qpos) <= WINDOW)

    def score_fn(idx, half):
        b, c = divmod(idx, n_pairs)
        qb = q_ref[q_rows[b], c * LANES:(c + 1) * LANES]
        first, second = _half_masks(qb.shape)
        qm = jnp.where(first if half == 0 else second, qb, jnp.zeros_like(qb))
        scores = [_dot_nt(qm, k_ref[pl.ds(ctx_start, l_ctx), kv_block(c)])]
        if local:
            scores.append(jnp.where(valids[b], _dot_nt(qm, k_ref[pl.ds(starts[b], wlen), kv_block(c)]), NEG_INF))
        return scores

    values = []
    for b in range(n_sub):
        for c in range(n_pairs):
            vals = [v_ref[pl.ds(ctx_start, l_ctx), kv_block(c)]]
            if local:
                vals.append(v_ref[pl.ds(starts[b], wlen), kv_block(c)])
            values.append(vals)
    outs = _paired_heads_attention(score_fn, values, n_sub * n_pairs, sinks)
    for idx, out in enumerate(outs):
        b, c = divmod(idx, n_pairs)
        o_ref[q_rows[b], c * LANES:(c + 1) * LANES] = out.astype(BF16)


def _attn_c(sink, cq, ckv, n, l_ctx, need_ctx):
    rows = cq.shape[0]
    gw = GROUP_WIDTH
    kvw = C_KV_HEADS * LANES
    smem = pl.BlockSpec(memory_space=pltpu.SMEM)
    yc = pl.pallas_call(
        functools.partial(_attn_c_kernel, n=n, l_ctx=l_ctx, local=True, aliased=False),
        grid=(n // TM,),
        in_specs=[
            smem,
            pl.BlockSpec((TM, gw), lambda i: (i, 0)),
            pl.BlockSpec((rows, kvw), lambda i: (0, 0)),
            pl.BlockSpec((rows, kvw), lambda i: (0, 1)),
        ],
        out_specs=pl.BlockSpec((TM, gw), lambda i: (i, 0)),
        out_shape=jax.ShapeDtypeStruct((rows, gw), BF16),
        compiler_params=_params(("parallel",)),
        name="attn_c",
    )(sink, cq, ckv, ckv)
    if not need_ctx:
        return yc
    ct = n // TM
    cb = n // l_ctx
    return pl.pallas_call(
        functools.partial(_attn_c_kernel, n=n, l_ctx=l_ctx, local=False, aliased=True),
        grid=(1,),
        in_specs=[
            smem,
            pl.BlockSpec((TM, gw), lambda i: (ct, 0)),
            pl.BlockSpec((l_ctx, kvw), lambda i: (cb, 0)),
            pl.BlockSpec((l_ctx, kvw), lambda i: (cb, 1)),
            pl.BlockSpec(memory_space=pl.ANY),
        ],
        out_specs=pl.BlockSpec((TM, gw), lambda i: (ct, 0)),
        out_shape=jax.ShapeDtypeStruct((rows, gw), BF16),
        input_output_aliases={4: 0},
        compiler_params=_params(("parallel",)),
        name="attn_c_ctx",
    )(sink, cq, ckv, ckv, yc)


def _attn_d_kernel(q_ref, kc_ref, vc_ref, *rest, local, aliased, n_rows):
    o_ref = rest[-1]
    n_pairs = D_HEADS // 2
    lane_block = lambda c: slice(c * LANES, (c + 1) * LANES)
    nk = NA_ROWS * GRID_W
    if local:
        k_ref, v_ref, b_ref = rest[0], rest[1], rest[2]
        g = pl.program_id(0)
        win0 = jnp.clip(g * D_ROWS - NA_ROWS // 2, 0, n_rows - (D_ROWS + NA_ROWS - 1))
        q_rows, offs, shifts = [], [], []
        for i in range(D_ROWS):
            r = g * D_ROWS + i
            rs = jnp.clip(r - NA_ROWS // 2, 0, n_rows - NA_ROWS)
            q_rows.append(slice(i * GRID_W, (i + 1) * GRID_W))
            offs.append(pl.multiple_of((rs - win0) * GRID_W, GRID_W))
            shifts.append(r - rs)
    else:
        q_rows = [slice(0, q_ref.shape[0])]

    def score_fn(idx, half):
        i, c = divmod(idx, n_pairs)
        qb = q_ref[q_rows[i], lane_block(c)]
        first, second = _half_masks(qb.shape)
        qm = jnp.where(first if half == 0 else second, qb, jnp.zeros_like(qb))
        scores = [_dot_nt(qm, kc_ref[:, lane_block(c)])]
        if local:
            scores.append(_dot_nt(qm, k_ref[pl.ds(offs[i], nk), lane_block(c)]) + b_ref[shifts[i], 2 * c + half])
        return scores

    values = []
    for i in range(len(q_rows)):
        for c in range(n_pairs):
            vals = [vc_ref[:, lane_block(c)]]
            if local:
                vals.append(v_ref[pl.ds(offs[i], nk), lane_block(c)])
            values.append(vals)
    outs = _paired_heads_attention(score_fn, values, len(q_rows) * n_pairs)
    for idx, out in enumerate(outs):
        i, c = divmod(idx, n_pairs)
        o_ref[q_rows[i], lane_block(c)] = out.astype(BF16)


def _attn_d(dq, dk, dv, bias, n, l_ctx, need_ctx):
    rows = dq.shape[0]
    gw = GROUP_WIDTH
    n_rows = n // GRID_W
    nk = NA_ROWS * GRID_W
    cb = n // l_ctx
    win = D_ROWS + NA_ROWS - 1
    assert n_rows % D_ROWS == 0 and n_rows >= win

    def window_start(g):
        return jnp.clip(g * D_ROWS - NA_ROWS // 2, 0, n_rows - win) * GRID_W

    once = dict(pipeline_mode=pl.Buffered(1))
    yd = pl.pallas_call(
        functools.partial(_attn_d_kernel, local=True, aliased=False, n_rows=n_rows),
        grid=(n_rows // D_ROWS,),
        in_specs=[
            pl.BlockSpec((D_ROWS * GRID_W, gw), lambda g: (g, 0)),
            pl.BlockSpec((l_ctx, gw), lambda g: (cb, 0), **once),
            pl.BlockSpec((l_ctx, gw), lambda g: (cb, 0), **once),
            pl.BlockSpec((pl.Element(win * GRID_W), pl.Element(gw)), lambda g: (window_start(g), 0)),
            pl.BlockSpec((pl.Element(win * GRID_W), pl.Element(gw)), lambda g: (window_start(g), 0)),
            pl.BlockSpec((NA_ROWS, D_HEADS, GRID_W, nk), lambda g: (0, 0, 0, 0), **once),
        ],
        out_specs=pl.BlockSpec((D_ROWS * GRID_W, gw), lambda g: (g, 0)),
        out_shape=jax.ShapeDtypeStruct((rows, gw), BF16),
        compiler_params=_params(("parallel",)),
        name="attn_d",
    )(dq, dk, dv, dk, dv, bias)
    if not need_ctx:
        return yd
    ct = n // TM
    return pl.pallas_call(
        functools.partial(_attn_d_kernel, local=False, aliased=True, n_rows=n_rows),
        grid=(1,),
        in_specs=[
            pl.BlockSpec((TM, gw), lambda i: (ct, 0)),
            pl.BlockSpec((l_ctx, gw), lambda i: (cb, 0)),
            pl.BlockSpec((l_ctx, gw), lambda i: (cb, 0)),
            pl.BlockSpec(memory_space=pl.ANY),
        ],
        out_specs=pl.BlockSpec((TM, gw), lambda i: (ct, 0)),
        out_shape=jax.ShapeDtypeStruct((rows, gw), BF16),
        input_output_aliases={3: 0},
        compiler_params=_params(("parallel",)),
        name="attn_d_ctx",
    )(dq, dk, dv, yd)


def _outproj_kernel(ya_ref, yb_ref, yc_ref, yd_ref, w_ref, x_ref, gt_ref, lng_ref, lnb_ref, o_ref):
    y_in = jnp.concatenate([ya_ref[...], yb_ref[...], yc_ref[...], yd_ref[...]], axis=1)
    y = jnp.dot(y_in, w_ref[...], preferred_element_type=F32)
    t = DEEPNORM_ALPHA * x_ref[...] + gt_ref[0] * y
    o_ref[...] = _ln(t, lng_ref[...], lnb_ref[...])


def _outproj(ya, yb, yc, yd, w_out, layer, x, mods_l, gate_k, lng, lnb, n_tiles, nlt):
    d = D_MODEL
    gw = GROUP_WIDTH
    ytile = pl.BlockSpec((TM, gw), lambda i: (i, 0))
    vec = pl.BlockSpec((1, d), lambda i: (0, 0))
    return pl.pallas_call(
        _outproj_kernel,
        grid=(n_tiles,),
        in_specs=[
            ytile, ytile, ytile, ytile,
            pl.BlockSpec((None, d, d), lambda i: (layer, 0, 0), pipeline_mode=pl.Buffered(1)),
            pl.BlockSpec((TM, d), lambda i: (i, 0)),
            pl.BlockSpec((1, 1, d), lambda i: (jnp.where(i >= nlt, N_MOD, 0) + gate_k, 0, 0)),
            vec, vec,
        ],
        out_specs=pl.BlockSpec((TM, d), lambda i: (i, 0)),
        out_shape=jax.ShapeDtypeStruct((n_tiles * TM, d), F32),
        compiler_params=_params(("parallel",)),
        name="outproj",
    )(ya, yb, yc, yd, w_out, x, mods_l, lng, lnb)


def _rope_tables(n, rows):
    t = jnp.arange(n, dtype=jnp.int32)
    row = (t // GRID_W).astype(F32)
    col = (t % GRID_W).astype(F32)
    n_freq = HEAD_DIM // 4
    inv = ROPE_THETA ** (-jnp.arange(n_freq, dtype=F32) / n_freq)
    ang = jnp.concatenate([row[:, None] * inv, col[:, None] * inv], axis=-1)
    cos, sin = jnp.cos(ang), jnp.sin(ang)
    cos_t = jnp.tile(jnp.concatenate([cos, cos], axis=-1), (1, LANES // HEAD_DIM))
    sin_t = jnp.tile(jnp.concatenate([-sin, sin], axis=-1), (1, LANES // HEAD_DIM))
    pad = rows - n
    cos_t = jnp.concatenate([cos_t, jnp.ones((pad, LANES), F32)], axis=0)
    sin_t = jnp.concatenate([sin_t, jnp.zeros((pad, LANES), F32)], axis=0)
    return cos_t, sin_t


def _na_bias(rpb):
    w = GRID_W
    cq = jnp.arange(w)
    cs = jnp.clip(cq - NA_COLS // 2, 0, w - NA_COLS)
    col_ok = (cq[None, :] >= cs[:, None]) & (cq[None, :] < cs[:, None] + NA_COLS)
    edge = w - NA_COLS
    ext = jnp.pad(rpb.astype(F32), ((0, 0), (0, 0), (edge, edge)), mode="edge")
    toep = jnp.stack([ext[:, :, w - 1 - q:2 * w - 1 - q] for q in range(w)], axis=2)
    toep = jnp.where(col_ok[None, None], toep, NEG_INF)
    per_shift = []
    for shift in range(NA_ROWS):
        sl = toep[:, NA_ROWS - 1 - shift:2 * NA_ROWS - 1 - shift]
        per_shift.append(jnp.swapaxes(sl, 1, 2).reshape(rpb.shape[0], w, NA_ROWS * w))
    return jnp.stack(per_shift, axis=0) * LOG2E


def kernel(x, c, ctx, c_ctx, w_mod, b_mod, ln_g, ln_b, ffn1_w_in, ffn1_w_out, ffn2_w_in, ffn2_w_out, mix_w_in, mix_w_out, a_lambda, b_norm_g, b_norm_b, b_spatial_w, b_spatial_b, c_sink, d_rpb):
    depth = w_mod.shape[0]
    n = x.shape[1]
    l_ctx = ctx.shape[1]
    d = D_MODEL
    assert x.shape[0] == 1 and n % TM == 0 and n // GRID_W >= NA_ROWS and n % 1024 == 0
    assert l_ctx % CHUNK == 0 and l_ctx <= TM and n % l_ctx == 0
    nlt = n // TM
    rows = n + TM

    ctx_rows = jnp.concatenate([ctx[0], jnp.zeros((TM - l_ctx, d), F32)], axis=0)
    cc = jnp.concatenate([c, c_ctx[None], jnp.zeros((6, d), F32)], axis=0)
    mods = _mod_vectors(cc, w_mod, b_mod)[:, :2].reshape(depth, 2 * N_MOD, 1, d)
    cos_t, sin_t = _rope_tables(n, rows)
    w1_in, w1_out, w2_in, w2_out, wm_in, wm_out = (
        w.astype(BF16) for w in (ffn1_w_in, ffn1_w_out, ffn2_w_in, ffn2_w_out, mix_w_in, mix_w_out))

    for l in range(depth):
        last = l == depth - 1
        lam_init = 0.8 - 0.6 * math.exp(-0.3 * l)
        mods_l = mods[l]
        lng = [ln_g[l, k][None] for k in range(3)]
        lnb = [ln_b[l, k][None] for k in range(3)]

        if l == 0:
            xs = _ffn(x[0], ctx_rows, 0, mods_l, (0, 1, 2), lng[0], lnb[0], w1_in, w1_out, l, n)
        else:
            xs = _ffn(xs, xs, nlt, mods_l, (0, 1, 2), lng[0], lnb[0], w1_in, w1_out, l, n)
        aq, ak, av, bu, bv, cq, ckv, dq, dk, dv = _inproj(xs, mods_l, 3, 4, wm_in, l, cos_t, sin_t, nlt)
        ya = _attn_a(a_lambda[l], aq, ak, av, n, l_ctx, lam_init, not last)
        yb = _mixb(bu, bv, b_norm_g[l][None], b_norm_b[l][None], b_spatial_w[l], b_spatial_b[l].T)
        yc = _attn_c(c_sink[l], cq, ckv, n, l_ctx, not last)
        yd = _attn_d(dq, dk, dv, _na_bias(d_rpb[l]), n, l_ctx, not last)
        n_tiles = nlt if last else nlt + 1
        xs = _outproj(ya, yb, yc, yd, wm_out, l, xs, mods_l, 5, lng[1], lnb[1], n_tiles, nlt)
        xs = _ffn(xs, None if last else xs, nlt, mods_l, (6, 7, 8), lng[2], lnb[2], w2_in, w2_out, l, n)
    return xs[None]
```

```python
import functools
import math

import jax
import jax.numpy as jnp
from jax import lax
from jax.experimental import pallas as pl
from jax.experimental.pallas import tpu as pltpu

F32 = jnp.float32
BF16 = jnp.bfloat16

D_MODEL = 2048
N_GROUPS = 4
GROUP_WIDTH = D_MODEL // N_GROUPS
HEAD_DIM = 64
GRID_W = 64
CHUNK = 128
B_GROUPS = GROUP_WIDTH // CHUNK
A_HEADS = GROUP_WIDTH // (2 * HEAD_DIM)
C_HEADS = GROUP_WIDTH // HEAD_DIM
C_KV_HEADS = C_HEADS // 4
D_HEADS = GROUP_WIDTH // HEAD_DIM
WINDOW = 128
QBLOCK = 128
NA_ROWS = 8
NA_COLS = 16
D_FF = 256 * math.ceil(8 * D_MODEL / 3 / 256)
N_MOD = 9
ROPE_THETA = 10000.0
LN_EPS = 1e-6
NEG_INF = -1e30
MODEL_DEPTH = 4
DEEPNORM_ALPHA = (2 * MODEL_DEPTH) ** 0.25
QK_SCALE = HEAD_DIM ** -0.5
LOG2E = math.log2(math.e)

LANES = 128
SUBLANES = 8
BF16_SUBLANES = 16
A_VROWS = 2 * HEAD_DIM + BF16_SUBLANES
A_UNROLL = 16
D_ROWS = 8
TN_MOD = 2048
TM = 512
TM_FFN = 1024
TF = 512
SEG = 512
N_SEG = 10
VMEM_LIMIT = 60 * 1024 * 1024


def _params(sem):
    return pltpu.CompilerParams(dimension_semantics=sem, vmem_limit_bytes=VMEM_LIMIT)


def _ln(t, g, b):
    mu = jnp.mean(t, axis=-1, keepdims=True)
    tc = t - mu
    var = jnp.mean(tc * tc, axis=-1, keepdims=True)
    return tc * lax.rsqrt(var + LN_EPS) * g + b


def _dot_nt(a, b):
    return lax.dot_general(a, b, (((1,), (1,)), ((), ())), preferred_element_type=F32)


def _half_masks(shape):
    lane = lax.broadcasted_iota(jnp.int32, shape, 1)
    return lane < HEAD_DIM, lane >= HEAD_DIM


def _mod_kernel(cc_ref, w_ref, b_ref, o_ref):
    a = cc_ref[...]
    s = (a * jax.nn.sigmoid(a)).astype(BF16)
    o_ref[0] = jnp.dot(s, w_ref[0].astype(BF16), preferred_element_type=F32) + b_ref[0]


def _mod_vectors(cc, w_mod, b_mod):
    depth, d, nm = w_mod.shape
    tn = TN_MOD
    return pl.pallas_call(
        _mod_kernel,
        grid=(depth, nm // tn),
        in_specs=[
            pl.BlockSpec((SUBLANES, d), lambda l, j: (0, 0)),
            pl.BlockSpec((1, d, tn), lambda l, j: (l, 0, j)),
            pl.BlockSpec((1, 1, tn), lambda l, j: (l, 0, j)),
        ],
        out_specs=pl.BlockSpec((1, SUBLANES, tn), lambda l, j: (l, 0, j)),
        out_shape=jax.ShapeDtypeStruct((depth, SUBLANES, nm), F32),
        compiler_params=_params(("parallel", "parallel")),
        name="mod_vectors",
    )(cc, w_mod, b_mod.reshape(depth, 1, nm))


def _ffn_kernel(x_ref, sh_ref, sc_ref, gt_ref, lng_ref, lnb_ref, wg_ref, wu_ref, wo_ref, o_ref, *, n_active=None):
    j = pl.program_id(1)
    last_j = pl.num_programs(1) - 1
    half = x_ref.shape[0] // 2
    halves = [slice(r * half, (r + 1) * half) for r in range(2)]

    @pl.when(j == 0)
    def _():
        o_ref[...] = jnp.zeros_like(o_ref)

    def accumulate():
        for rows in halves:
            xin = (x_ref[rows] * (1.0 + sc_ref[0]) + sh_ref[0]).astype(BF16)
            g = jnp.dot(xin, wg_ref[...], preferred_element_type=F32)
            u = jnp.dot(xin, wu_ref[...], preferred_element_type=F32)
            hh = ((g * jax.nn.sigmoid(g)) * u).astype(BF16)
            o_ref[rows] += jnp.dot(hh, wo_ref[...], preferred_element_type=F32)

    def finalize():
        for rows in halves:
            t = DEEPNORM_ALPHA * x_ref[rows] + (0.5 * gt_ref[0]) * o_ref[rows]
            o_ref[rows] = _ln(t, lng_ref[...], lnb_ref[...])

    if n_active is None:
        accumulate()
        pl.when(j == last_j)(finalize)
    else:
        active = pl.program_id(0) < n_active
        pl.when(active)(accumulate)
        pl.when(jnp.logical_and(active, j == last_j))(finalize)


def _ffn_kernel_aliased(*refs, n_active):
    _ffn_kernel(*refs[:-2], refs[-1], n_active=n_active)


def _ffn(x, x_ctx, ctx_row0, l_ctx, mods_l, ks, lng, lnb, w_in, w_out, layer, n):
    d = D_MODEL
    nj = D_FF // TF
    assert n % TM_FFN == 0 and n % TM == 0
    with_ctx = x_ctx is not None
    assert not with_ctx or (TM % l_ctx == 0 and n % l_ctx == 0 and ctx_row0 % l_ctx == 0)
    out_rows = n + TM if with_ctx else n

    def call(src, tm, tile0, out_tile0, n_tiles, mod0, alias, n_active=None):
        mod_spec = lambda k: pl.BlockSpec((1, 1, d), lambda i, j: (mod0 + k, 0, 0))
        vec_spec = pl.BlockSpec((1, d), lambda i, j: (0, 0))
        in_specs = [
            pl.BlockSpec((tm, d), lambda i, j: (tile0 + i, 0)),
            mod_spec(ks[0]), mod_spec(ks[1]), mod_spec(ks[2]),
            vec_spec, vec_spec,
            pl.BlockSpec((None, d, TF), lambda i, j: (layer, 0, j)),
            pl.BlockSpec((None, d, TF), lambda i, j: (layer, 0, nj + j)),
            pl.BlockSpec((None, TF, d), lambda i, j: (layer, j, 0)),
        ]
        args = [src, mods_l, mods_l, mods_l, lng, lnb, w_in, w_in, w_out]
        kernel_fn = _ffn_kernel
        aliases = {}
        if alias is not None:
            in_specs.append(pl.BlockSpec(memory_space=pl.ANY))
            args.append(alias)
            aliases = {len(args) - 1: 0}
            kernel_fn = functools.partial(_ffn_kernel_aliased, n_active=n_active)
        return pl.pallas_call(
            kernel_fn,
            grid=(n_tiles, nj),
            in_specs=in_specs,
            out_specs=pl.BlockSpec((tm, d), lambda i, j: (out_tile0 + i, 0)),
            out_shape=jax.ShapeDtypeStruct((out_rows, d), F32),
            input_output_aliases=aliases,
            compiler_params=_params(("parallel", "arbitrary")),
            name="ffn" if alias is None else "ffn_ctx",
        )(*args)

    y = call(x, TM_FFN, 0, 0, n // TM_FFN, 0, None)
    if with_ctx:
        y = call(x_ctx, l_ctx, ctx_row0 // l_ctx, n // l_ctx, TM // l_ctx, N_MOD, y, n_active=1)
    return y


def _rope(t, cos_ref, sin_ref):
    w = t.shape[1]
    lane = lax.broadcasted_iota(jnp.int32, t.shape, 1)
    first = (lane % HEAD_DIM) < (HEAD_DIM // 2)
    rot = jnp.where(first, pltpu.roll(t, w - HEAD_DIM // 2, 1), pltpu.roll(t, HEAD_DIM // 2, 1))
    reps = w // LANES
    cos = jnp.concatenate([cos_ref[...]] * reps, axis=1) if reps > 1 else cos_ref[...]
    sin = jnp.concatenate([sin_ref[...]] * reps, axis=1) if reps > 1 else sin_ref[...]
    return t * cos + rot * sin


_SEG_AQ, _SEG_AK, _SEG_AV, _SEG_BU, _SEG_BV, _SEG_CQ, _SEG_CKV, _SEG_DQ, _SEG_DK, _SEG_DV = range(N_SEG)
_SEG_WIDTHS = tuple(2 * C_KV_HEADS * HEAD_DIM if k == _SEG_CKV else GROUP_WIDTH for k in range(N_SEG))
_SEG_OFFSETS = tuple(sum(_SEG_WIDTHS[:k]) for k in range(N_SEG))
D_PROJ = sum(_SEG_WIDTHS)


def _dup_halves(x):
    swapped = pltpu.roll(x, HEAD_DIM, 1)
    first, _ = _half_masks(x.shape)
    return jnp.where(first, x, swapped), jnp.where(first, swapped, x)


def _inproj_kernel(x_ref, sh_ref, sc_ref, w_ref, cos_ref, sin_ref, *out_refs):
    h = (x_ref[...] * (1.0 + sc_ref[0]) + sh_ref[0]).astype(BF16)

    def project(k):
        return jnp.dot(h, w_ref[:, _SEG_OFFSETS[k]:_SEG_OFFSETS[k] + _SEG_WIDTHS[k]], preferred_element_type=F32)

    def finish(k, t):
        o_ref = out_refs[k]
        if k in (_SEG_AQ, _SEG_AK, _SEG_CQ):
            t = _rope(t, cos_ref, sin_ref)
        if k in (_SEG_AQ, _SEG_CQ, _SEG_DQ):
            t = t * (QK_SCALE * LOG2E)
        if k == _SEG_AQ:
            o_ref[0] = t.T.astype(o_ref.dtype)
        elif k == _SEG_AV:
            tt = t.T.astype(o_ref.dtype)
            extra = A_VROWS - 2 * HEAD_DIM
            ones_row = (lax.broadcasted_iota(jnp.int32, (extra, tt.shape[1]), 0) == 0).astype(o_ref.dtype)
            for hd in range(A_HEADS):
                o_ref[0, hd * A_VROWS:hd * A_VROWS + 2 * HEAD_DIM] = tt[hd * 2 * HEAD_DIM:(hd + 1) * 2 * HEAD_DIM]
                o_ref[0, hd * A_VROWS + 2 * HEAD_DIM:(hd + 1) * A_VROWS] = ones_row
        elif k == _SEG_CKV:
            kk = _dup_halves(_rope(t[:, :LANES], cos_ref, sin_ref))
            vv = _dup_halves(t[:, LANES:])
            o_ref[...] = jnp.concatenate([kk[0], kk[1], vv[0], vv[1]], axis=1).astype(o_ref.dtype)
        else:
            o_ref[...] = t.astype(o_ref.dtype)

    t = project(0)
    for k in range(N_SEG):
        t_next = project(k + 1) if k + 1 < N_SEG else None
        finish(k, t)
        t = t_next


def _inproj(x, mods_l, shift_k, scale_k, w_in, layer, cos_t, sin_t, nlt):
    rows = x.shape[0]

    def mod_spec(k):
        return pl.BlockSpec((1, 1, D_MODEL), lambda i: (jnp.where(i >= nlt, N_MOD, 0) + k, 0, 0))

    n_tiles = rows // TM
    d = D_MODEL
    assert C_KV_HEADS == 2 and w_in.shape[1:] == (d, D_PROJ)
    widths = [SEG] * N_SEG
    dtypes = [BF16] * N_SEG
    dtypes[_SEG_BU] = F32
    dtypes[_SEG_BV] = F32
    out_specs = [pl.BlockSpec((TM, w), lambda i: (i, 0)) for w in widths]
    out_shape = [jax.ShapeDtypeStruct((rows, w), dt) for w, dt in zip(widths, dtypes)]
    for k in (_SEG_AQ, _SEG_AV):
        slab = SEG if k == _SEG_AQ else A_HEADS * A_VROWS
        out_specs[k] = pl.BlockSpec((1, slab, TM), lambda i: (i, 0, 0))
        out_shape[k] = jax.ShapeDtypeStruct((n_tiles, slab, TM), BF16)
    return pl.pallas_call(
        _inproj_kernel,
        grid=(n_tiles,),
        in_specs=[
            pl.BlockSpec((TM, d), lambda i: (i, 0)),
            mod_spec(shift_k), mod_spec(scale_k),
            pl.BlockSpec((None, d, D_PROJ), lambda i: (layer, 0, 0), pipeline_mode=pl.Buffered(1)),
            pl.BlockSpec((TM, LANES), lambda i: (i, 0)),
            pl.BlockSpec((TM, LANES), lambda i: (i, 0)),
        ],
        out_specs=out_specs,
        out_shape=out_shape,
        compiler_params=_params(("parallel",)),
        name="inproj",
    )(x, mods_l, mods_l, w_in, cos_t, sin_t)


def _attn_a_kernel(lam_ref, qt_ref, k_ref, vt_ref, *rest, n_chunks, tail_k_start, tail_chunk, tail_len, lam_init,
                   aliased):
    if aliased:
        rest = rest[1:]
    o_ref, m_scr, acc_scr, s_scr, mc_scr = rest
    qt = qt_ref[0]
    comp = lax.broadcasted_iota(jnp.int32, qt.shape, 0) < HEAD_DIM
    zero = jnp.zeros_like(qt)
    qts = (jnp.where(comp, qt, zero), jnp.where(comp, zero, qt))
    m_scr[...] = jnp.full(m_scr.shape, NEG_INF, F32)
    acc_scr[...] = jnp.zeros_like(acc_scr)

    def scores(k, slot, size):
        for c in range(2):
            s = jnp.dot(k, qts[c], preferred_element_type=F32)
            s_scr[slot, c, :size] = s
            mc_scr[slot, c] = jnp.max(s, axis=0, keepdims=True)

    def consume(slot, vt, size):
        for c in range(2):
            m_old = m_scr[c]
            m_new = jnp.maximum(m_old, mc_scr[slot, c])
            alpha = jnp.exp2(m_old - m_new)
            p = jnp.exp2(s_scr[slot, c, :size] - m_new).astype(BF16)
            acc_scr[c] = alpha * acc_scr[c] + jnp.dot(vt, p, preferred_element_type=F32)
            m_scr[c] = m_new

    def k_chunk(i):
        i = jnp.minimum(i, n_chunks - 1)
        return k_ref[pl.ds(pl.multiple_of(i * TM, TM), TM), :]

    scores(k_ref[pl.ds(tail_k_start, tail_len), :], 1, tail_len)
    if n_chunks:
        unroll = math.gcd(n_chunks, A_UNROLL)
        assert unroll % 2 == 0
        scores(k_chunk(0), 0, TM)
    consume(1, vt_ref[tail_chunk][:, :tail_len], tail_len)
    if n_chunks:
        def body(t, carry):
            j = unroll * t
            for u in range(unroll):
                scores(k_chunk(j + u + 1), (u + 1) % 2, TM)
                consume(u % 2, vt_ref[j + u], TM)
            return carry
        lax.fori_loop(0, n_chunks // unroll, body, 0)

    lv = lam_ref[...]
    lam = (jnp.exp(jnp.sum(lv[0:1] * lv[1:2], axis=1, keepdims=True))
           - jnp.exp(jnp.sum(lv[2:3] * lv[3:4], axis=1, keepdims=True)) + lam_init)
    nv = 2 * HEAD_DIM
    ot = (acc_scr[0, :nv] / acc_scr[0, nv:nv + 1]
          - lam * (acc_scr[1, :nv] / acc_scr[1, nv:nv + 1]))
    ms = jnp.mean(ot * ot, axis=0, keepdims=True)
    ot = (ot * lax.rsqrt(ms + LN_EPS)) * (1.0 - lam_init)
    o_ref[...] = ot.T.astype(BF16)


def _attn_a(lam_vecs, aqt, ak, avt, n, l_ctx, lam_init, need_ctx):
    rows = ak.shape[0]
    nt = rows // TM
    nlt = n // TM
    scratch = [pltpu.VMEM((2, 1, TM), F32), pltpu.VMEM((2, A_VROWS, TM), F32),
               pltpu.VMEM((2, 2, TM, TM), F32), pltpu.VMEM((2, 2, 1, TM), F32)]
    lam_spec = pl.BlockSpec((4, HEAD_DIM), lambda h, i: (0, 0))
    ya = pl.pallas_call(
        functools.partial(_attn_a_kernel, n_chunks=nlt, tail_k_start=n, tail_chunk=nlt, tail_len=l_ctx,
                          lam_init=lam_init, aliased=False),
        grid=(A_HEADS, nlt),
        in_specs=[
            lam_spec,
            pl.BlockSpec((1, LANES, TM), lambda h, i: (i, h, 0)),
            pl.BlockSpec((rows, LANES), lambda h, i: (0, h)),
            pl.BlockSpec((nt, A_VROWS, TM), lambda h, i: (0, h, 0)),
        ],
        out_specs=pl.BlockSpec((TM, LANES), lambda h, i: (i, h)),
        out_shape=jax.ShapeDtypeStruct((rows, GROUP_WIDTH), BF16),
        scratch_shapes=scratch,
        compiler_params=_params(("parallel", "parallel")),
        name="attn_a",
    )(lam_vecs, aqt, ak, avt)
    if not need_ctx:
        return ya
    cb = n // l_ctx
    return pl.pallas_call(
        functools.partial(_attn_a_kernel, n_chunks=0, tail_k_start=0, tail_chunk=0, tail_len=l_ctx,
                          lam_init=lam_init, aliased=True),
        grid=(A_HEADS, 1),
        in_specs=[
            lam_spec,
            pl.BlockSpec((1, LANES, TM), lambda h, i: (nlt, h, 0)),
            pl.BlockSpec((l_ctx, LANES), lambda h, i: (cb, h)),
            pl.BlockSpec((1, A_VROWS, TM), lambda h, i: (nlt, h, 0)),
            pl.BlockSpec(memory_space=pl.ANY),
        ],
        out_specs=pl.BlockSpec((TM, LANES), lambda h, i: (nlt, h)),
        out_shape=jax.ShapeDtypeStruct((rows, GROUP_WIDTH), BF16),
        scratch_shapes=scratch,
        input_output_aliases={4: 0},
        compiler_params=_params(("parallel", "parallel")),
        name="attn_a_ctx",
    )(lam_vecs, aqt, ak, avt, ya)


def _mixb_kernel(u_ref, v_ref, g_ref, b_ref, ws_ref, bs_ref, o_ref):
    u = jax.nn.gelu(u_ref[...])
    v = _ln(jax.nn.gelu(v_ref[...]), g_ref[...], b_ref[...]).astype(BF16)
    for c in range(u.shape[0] // CHUNK):
        rs = slice(c * CHUNK, (c + 1) * CHUNK)
        for g in range(B_GROUPS):
            cs = slice(g * LANES, (g + 1) * LANES)
            mixed = jnp.dot(ws_ref[g].astype(BF16), v[rs, cs], preferred_element_type=F32) + bs_ref[:, g:g + 1]
            o_ref[rs, cs] = (u[rs, cs] * mixed).astype(BF16)


def _mixb(bu, bv, gn_g, gn_b, w_s, b_s_t):
    rows = bu.shape[0]
    gw = GROUP_WIDTH
    tile = pl.BlockSpec((TM, gw), lambda i: (i, 0))
    return pl.pallas_call(
        _mixb_kernel,
        grid=(rows // TM,),
        in_specs=[
            tile, tile,
            pl.BlockSpec((1, gw), lambda i: (0, 0)),
            pl.BlockSpec((1, gw), lambda i: (0, 0)),
            pl.BlockSpec((B_GROUPS, CHUNK, CHUNK), lambda i: (0, 0, 0)),
            pl.BlockSpec((CHUNK, B_GROUPS), lambda i: (0, 0)),
        ],
        out_specs=tile,
        out_shape=jax.ShapeDtypeStruct((rows, gw), BF16),
        compiler_params=_params(("parallel",)),
        name="mix_b",
    )(bu, bv, gn_g, gn_b, w_s, b_s_t)


def _paired_heads_attention(score_fn, values, n_pairs, sinks=None):
    all_scores = [score_fn(c, half) for c in range(n_pairs) for half in range(2)]
    all_probs = []
    for idx, scores in enumerate(all_scores):
        m = jnp.max(scores[0], axis=1, keepdims=True)
        for s in scores[1:]:
            m = jnp.maximum(m, jnp.max(s, axis=1, keepdims=True))
        if sinks is not None:
            m = jnp.maximum(m, sinks[idx])
        ps = [jnp.exp2(s - m) for s in scores]
        den = jnp.sum(ps[0], axis=1, keepdims=True)
        for p in ps[1:]:
            den = den + jnp.sum(p, axis=1, keepdims=True)
        if sinks is not None:
            den = den + jnp.exp2(sinks[idx] - m)
        all_probs.append(([p.astype(BF16) for p in ps], den))
    outs = []
    for c in range(n_pairs):
        out = None
        for half in range(2):
            ps, den = all_probs[2 * c + half]
            num = None
            for p, v in zip(ps, values[c]):
                first, second = _half_masks(v.shape)
                pv = jnp.dot(p, jnp.where(first if half == 0 else second, v, jnp.zeros_like(v)),
                             preferred_element_type=F32)
                num = pv if num is None else num + pv
            o = num / den
            out = o if out is None else out + o
        outs.append(out)
    return outs


def _attn_c_kernel(sink_ref, q_ref, k_ref, v_ref, *rest, n, l_ctx, local, aliased):
    o_ref = rest[-1]
    i = pl.program_id(0)
    n_sub = q_ref.shape[0] // QBLOCK
    wlen = 3 * QBLOCK
    ctx_start = n if local else 0
    n_pairs = C_HEADS // 2
    grp_pairs = n_pairs // C_KV_HEADS
    kv_block = lambda c: slice((c // grp_pairs) * LANES, (c // grp_pairs + 1) * LANES)
    sinks = [sink_ref[2 * c + half] * LOG2E for _ in range(n_sub) for c in range(n_pairs) for half in range(2)]
    q_rows = [slice(b * QBLOCK, (b + 1) * QBLOCK) for b in range(n_sub)]
    if local:
        starts, valids = [], []
        for b in range(n_sub):
            blk = i * n_sub + b
            start = pl.multiple_of(jnp.clip((blk - 1) * QBLOCK, 0, n - wlen), QBLOCK)
            qpos = blk * QBLOCK + lax.broadcasted_iota(jnp.int32, (QBLOCK, wlen), 0)
            kpos = start + lax.broadcasted_iota(jnp.int32, (QBLOCK, wlen), 1)
            starts.append(start)
            valids.append(jnp.abs(kpos - qpos) <= WINDOW)

    def score_fn(idx, half):
        b, c = divmod(idx, n_pairs)
        qb = q_ref[q_rows[b], c * LANES:(c + 1) * LANES]
        first, second = _half_masks(qb.shape)
        qm = jnp.where(first if half == 0 else second, qb, jnp.zeros_like(qb))
        scores = [_dot_nt(qm, k_ref[pl.ds(ctx_start, l_ctx), kv_block(c)])]
        if local:
            scores.append(jnp.where(valids[b], _dot_nt(qm, k_ref[pl.ds(starts[b], wlen), kv_block(c)]), NEG_INF))
        return scores

    values = []
    for b in range(n_sub):
        for c in range(n_pairs):
            vals = [v_ref[pl.ds(ctx_start, l_ctx), kv_block(c)]]
            if local:
                vals.append(v_ref[pl.ds(starts[b], wlen), kv_block(c)])
            values.append(vals)
    outs = _paired_heads_attention(score_fn, values, n_sub * n_pairs, sinks)
    for idx, out in enumerate(outs):
        b, c = divmod(idx, n_pairs)
        o_ref[q_rows[b], c * LANES:(c + 1) * LANES] = out.astype(BF16)


def _attn_c(sink, cq, ckv, n, l_ctx, need_ctx):
    rows = cq.shape[0]
    gw = GROUP_WIDTH
    kvw = C_KV_HEADS * LANES
    smem = pl.BlockSpec(memory_space=pltpu.SMEM)
    yc = pl.pallas_call(
        functools.partial(_attn_c_kernel, n=n, l_ctx=l_ctx, local=True, aliased=False),
        grid=(n // TM,),
        in_specs=[
            smem,
            pl.BlockSpec((TM, gw), lambda i: (i, 0)),
            pl.BlockSpec((rows, kvw), lambda i: (0, 0)),
            pl.BlockSpec((rows, kvw), lambda i: (0, 1)),
        ],
        out_specs=pl.BlockSpec((TM, gw), lambda i: (i, 0)),
        out_shape=jax.ShapeDtypeStruct((rows, gw), BF16),
        compiler_params=_params(("parallel",)),
        name="attn_c",
    )(sink, cq, ckv, ckv)
    if not need_ctx:
        return yc
    ct = n // TM
    cb = n // l_ctx
    return pl.pallas_call(
        functools.partial(_attn_c_kernel, n=n, l_ctx=l_ctx, local=False, aliased=True),
        grid=(1,),
        in_specs=[
            smem,
            pl.BlockSpec((TM, gw), lambda i: (ct, 0)),
            pl.BlockSpec((l_ctx, kvw), lambda i: (cb, 0)),
            pl.BlockSpec((l_ctx, kvw), lambda i: (cb, 1)),
            pl.BlockSpec(memory_space=pl.ANY),
        ],
        out_specs=pl.BlockSpec((TM, gw), lambda i: (ct, 0)),
        out_shape=jax.ShapeDtypeStruct((rows, gw), BF16),
        input_output_aliases={4: 0},
        compiler_params=_params(("parallel",)),
        name="attn_c_ctx",
    )(sink, cq, ckv, ckv, yc)


def _attn_d_kernel(q_ref, kc_ref, vc_ref, *rest, local, aliased, n_rows):
    o_ref = rest[-1]
    n_pairs = D_HEADS // 2
    lane_block = lambda c: slice(c * LANES, (c + 1) * LANES)
    nk = NA_ROWS * GRID_W
    if local:
        k_ref, v_ref, b_ref = rest[0], rest[1], rest[2]
        g = pl.program_id(0)
        win0 = jnp.clip(g * D_ROWS - NA_ROWS // 2, 0, n_rows - (D_ROWS + NA_ROWS - 1))
        q_rows, offs, shifts = [], [], []
        for i in range(D_ROWS):
            r = g * D_ROWS + i
            rs = jnp.clip(r - NA_ROWS // 2, 0, n_rows - NA_ROWS)
            q_rows.append(slice(i * GRID_W, (i + 1) * GRID_W))
            offs.append(pl.multiple_of((rs - win0) * GRID_W, GRID_W))
            shifts.append(r - rs)
    else:
        q_rows = [slice(0, q_ref.shape[0])]

    def score_fn(idx, half):
        i, c = divmod(idx, n_pairs)
        qb = q_ref[q_rows[i], lane_block(c)]
        first, second = _half_masks(qb.shape)
        qm = jnp.where(first if half == 0 else second, qb, jnp.zeros_like(qb))
        scores = [_dot_nt(qm, kc_ref[:, lane_block(c)])]
        if local:
            scores.append(_dot_nt(qm, k_ref[pl.ds(offs[i], nk), lane_block(c)]) + b_ref[shifts[i], 2 * c + half])
        return scores

    values = []
    for i in range(len(q_rows)):
        for c in range(n_pairs):
            vals = [vc_ref[:, lane_block(c)]]
            if local:
                vals.append(v_ref[pl.ds(offs[i], nk), lane_block(c)])
            values.append(vals)
    outs = _paired_heads_attention(score_fn, values, len(q_rows) * n_pairs)
    for idx, out in enumerate(outs):
        i, c = divmod(idx, n_pairs)
        o_ref[q_rows[i], lane_block(c)] = out.astype(BF16)


def _attn_d(dq, dk, dv, bias, n, l_ctx, need_ctx):
    rows = dq.shape[0]
    gw = GROUP_WIDTH
    n_rows = n // GRID_W
    nk = NA_ROWS * GRID_W
    cb = n // l_ctx
    win = D_ROWS + NA_ROWS - 1
    assert n_rows % D_ROWS == 0 and n_rows >= win

    def window_start(g):
        return jnp.clip(g * D_ROWS - NA_ROWS // 2, 0, n_rows - win) * GRID_W

    once = dict(pipeline_mode=pl.Buffered(1))
    yd = pl.pallas_call(
        functools.partial(_attn_d_kernel, local=True, aliased=False, n_rows=n_rows),
        grid=(n_rows // D_ROWS,),
        in_specs=[
            pl.BlockSpec((D_ROWS * GRID_W, gw), lambda g: (g, 0)),
            pl.BlockSpec((l_ctx, gw), lambda g: (cb, 0), **once),
            pl.BlockSpec((l_ctx, gw), lambda g: (cb, 0), **once),
            pl.BlockSpec((pl.Element(win * GRID_W), pl.Element(gw)), lambda g: (window_start(g), 0)),
            pl.BlockSpec((pl.Element(win * GRID_W), pl.Element(gw)), lambda g: (window_start(g), 0)),
            pl.BlockSpec((NA_ROWS, D_HEADS, GRID_W, nk), lambda g: (0, 0, 0, 0), **once),
        ],
        out_specs=pl.BlockSpec((D_ROWS * GRID_W, gw), lambda g: (g, 0)),
        out_shape=jax.ShapeDtypeStruct((rows, gw), BF16),
        compiler_params=_params(("parallel",)),
        name="attn_d",
    )(dq, dk, dv, dk, dv, bias)
    if not need_ctx:
        return yd
    ct = n // TM
    return pl.pallas_call(
        functools.partial(_attn_d_kernel, local=False, aliased=True, n_rows=n_rows),
        grid=(1,),
        in_specs=[
            pl.BlockSpec((TM, gw), lambda i: (ct, 0)),
            pl.BlockSpec((l_ctx, gw), lambda i: (cb, 0)),
            pl.BlockSpec((l_ctx, gw), lambda i: (cb, 0)),
            pl.BlockSpec(memory_space=pl.ANY),
        ],
        out_specs=pl.BlockSpec((TM, gw), lambda i: (ct, 0)),
        out_shape=jax.ShapeDtypeStruct((rows, gw), BF16),
        input_output_aliases={3: 0},
        compiler_params=_params(("parallel",)),
        name="attn_d_ctx",
    )(dq, dk, dv, yd)


def _outproj_kernel(ya_ref, yb_ref, yc_ref, yd_ref, w_ref, x_ref, gt_ref, lng_ref, lnb_ref, o_ref):
    y_in = jnp.concatenate([ya_ref[...], yb_ref[...], yc_ref[...], yd_ref[...]], axis=1)
    y = jnp.dot(y_in, w_ref[...], preferred_element_type=F32)
    t = DEEPNORM_ALPHA * x_ref[...] + gt_ref[0] * y
    o_ref[...] = _ln(t, lng_ref[...], lnb_ref[...])


def _outproj(ya, yb, yc, yd, w_out, layer, x, mods_l, gate_k, lng, lnb, n_tiles, nlt):
    d = D_MODEL
    gw = GROUP_WIDTH
    ytile = pl.BlockSpec((TM, gw), lambda i: (i, 0))
    vec = pl.BlockSpec((1, d), lambda i: (0, 0))
    return pl.pallas_call(
        _outproj_kernel,
        grid=(n_tiles,),
        in_specs=[
            ytile, ytile, ytile, ytile,
            pl.BlockSpec((None, d, d), lambda i: (layer, 0, 0), pipeline_mode=pl.Buffered(1)),
            pl.BlockSpec((TM, d), lambda i: (i, 0)),
            pl.BlockSpec((1, 1, d), lambda i: (jnp.where(i >= nlt, N_MOD, 0) + gate_k, 0, 0)),
            vec, vec,
        ],
        out_specs=pl.BlockSpec((TM, d), lambda i: (i, 0)),
        out_shape=jax.ShapeDtypeStruct((n_tiles * TM, d), F32),
        compiler_params=_params(("parallel",)),
        name="outproj",
    )(ya, yb, yc, yd, w_out, x, mods_l, lng, lnb)


def _rope_tables(n, rows):
    t = jnp.arange(n, dtype=jnp.int32)
    row = (t // GRID_W).astype(F32)
    col = (t % GRID_W).astype(F32)
    n_freq = HEAD_DIM // 4
    inv = ROPE_THETA ** (-jnp.arange(n_freq, dtype=F32) / n_freq)
    ang = jnp.concatenate([row[:, None] * inv, col[:, None] * inv], axis=-1)
    cos, sin = jnp.cos(ang), jnp.sin(ang)
    cos_t = jnp.tile(jnp.concatenate([cos, cos], axis=-1), (1, LANES // HEAD_DIM))
    sin_t = jnp.tile(jnp.concatenate([-sin, sin], axis=-1), (1, LANES // HEAD_DIM))
    pad = rows - n
    cos_t = jnp.concatenate([cos_t, jnp.ones((pad, LANES), F32)], axis=0)
    sin_t = jnp.concatenate([sin_t, jnp.zeros((pad, LANES), F32)], axis=0)
    return cos_t, sin_t


def _na_bias(rpb):
    w = GRID_W
    cq = jnp.arange(w)
    cs = jnp.clip(cq - NA_COLS // 2, 0, w - NA_COLS)
    col_ok = (cq[None, :] >= cs[:, None]) & (cq[None, :] < cs[:, None] + NA_COLS)
    edge = w - NA_COLS
    ext = jnp.pad(rpb.astype(F32), ((0, 0), (0, 0), (edge, edge)), mode="edge")
    toep = jnp.stack([ext[:, :, w - 1 - q:2 * w - 1 - q] for q in range(w)], axis=2)
    toep = jnp.where(col_ok[None, None], toep, NEG_INF)
    per_shift = []
    for shift in range(NA_ROWS):
        sl = toep[:, NA_ROWS - 1 - shift:2 * NA_ROWS - 1 - shift]
        per_shift.append(jnp.swapaxes(sl, 1, 2).reshape(rpb.shape[0], w, NA_ROWS * w))
    return jnp.stack(per_shift, axis=0) * LOG2E


def kernel(x, c, ctx, c_ctx, w_mod, b_mod, ln_g, ln_b, ffn1_w_in, ffn1_w_out, ffn2_w_in, ffn2_w_out, mix_w_in, mix_w_out, a_lambda, b_norm_g, b_norm_b, b_spatial_w, b_spatial_b, c_sink, d_rpb):
    depth = w_mod.shape[0]
    n = x.shape[1]
    l_ctx = ctx.shape[1]
    d = D_MODEL
    assert x.shape[0] == 1 and n % TM == 0 and n // GRID_W >= NA_ROWS and n % TM_FFN == 0
    assert l_ctx % CHUNK == 0 and l_ctx <= TM and n % l_ctx == 0
    nlt = n // TM
    rows = n + TM

    ctx_rows = jnp.concatenate([ctx[0], jnp.zeros((TM - l_ctx, d), F32)], axis=0)
    cc = jnp.concatenate([c, c_ctx[None], jnp.zeros((SUBLANES - 2, d), F32)], axis=0)
    mods = _mod_vectors(cc, w_mod, b_mod)[:, :2].reshape(depth, 2 * N_MOD, 1, d)
    cos_t, sin_t = _rope_tables(n, rows)
    w1_in, w1_out, w2_in, w2_out, wm_in, wm_out = (
        w.astype(BF16) for w in (ffn1_w_in, ffn1_w_out, ffn2_w_in, ffn2_w_out, mix_w_in, mix_w_out))

    for l in range(depth):
        last = l == depth - 1
        lam_init = 0.8 - 0.6 * math.exp(-0.3 * l)
        mods_l = mods[l]
        lng = [ln_g[l, k][None] for k in range(3)]
        lnb = [ln_b[l, k][None] for k in range(3)]

        if l == 0:
            xs = _ffn(x[0], ctx_rows, 0, l_ctx, mods_l, (0, 1, 2), lng[0], lnb[0], w1_in, w1_out, l, n)
        else:
            xs = _ffn(xs, xs, n, l_ctx, mods_l, (0, 1, 2), lng[0], lnb[0], w1_in, w1_out, l, n)
        aq, ak, av, bu, bv, cq, ckv, dq, dk, dv = _inproj(xs, mods_l, 3, 4, wm_in, l, cos_t, sin_t, nlt)
        ya = _attn_a(a_lambda[l], aq, ak, av, n, l_ctx, lam_init, not last)
        yb = _mixb(bu, bv, b_norm_g[l][None], b_norm_b[l][None], b_spatial_w[l], b_spatial_b[l].T)
        yc = _attn_c(c_sink[l], cq, ckv, n, l_ctx, not last)
        yd = _attn_d(dq, dk, dv, _na_bias(d_rpb[l]), n, l_ctx, not last)
        n_tiles = nlt if last else nlt + 1
        xs = _outproj(ya, yb, yc, yd, wm_out, l, xs, mods_l, 5, lng[1], lnb[1], n_tiles, nlt)
        xs = _ffn(xs, None if last else xs, n, l_ctx, mods_l, (6, 7, 8), lng[2], lnb[2], w2_in, w2_out, l, n)
    return xs[None]
```

```python
import functools
import math

import jax
import jax.numpy as jnp
from jax import lax
from jax.experimental import pallas as pl
from jax.experimental.pallas import tpu as pltpu

F32 = jnp.float32
BF16 = jnp.bfloat16

D_MODEL = 2048
N_GROUPS = 4
GROUP_WIDTH = D_MODEL // N_GROUPS
HEAD_DIM = 64
GRID_W = 64
CHUNK = 128
B_GROUPS = GROUP_WIDTH // CHUNK
A_HEADS = GROUP_WIDTH // (2 * HEAD_DIM)
C_HEADS = GROUP_WIDTH // HEAD_DIM
C_KV_HEADS = C_HEADS // 4
D_HEADS = GROUP_WIDTH // HEAD_DIM
WINDOW = 128
QBLOCK = 128
NA_ROWS = 8
NA_COLS = 16
D_FF = 256 * math.ceil(8 * D_MODEL / 3 / 256)
N_MOD = 9
ROPE_THETA = 10000.0
LN_EPS = 1e-6
NEG_INF = -1e30
MODEL_DEPTH = 4
DEEPNORM_ALPHA = (2 * MODEL_DEPTH) ** 0.25
QK_SCALE = HEAD_DIM ** -0.5
LOG2E = math.log2(math.e)

LANES = 128
SUBLANES = 8
BF16_SUBLANES = 16
A_VROWS = 2 * HEAD_DIM + BF16_SUBLANES
A_UNROLL = 16
D_ROWS = 8
TN_MOD = 2048
TM = 512
TM_FFN = 1024
TF = 512
SEG = 512
N_SEG = 10
VMEM_LIMIT = 60 * 1024 * 1024


def _params(sem):
    return pltpu.CompilerParams(dimension_semantics=sem, vmem_limit_bytes=VMEM_LIMIT)


def _ln(t, g, b):
    mu = jnp.mean(t, axis=-1, keepdims=True)
    tc = t - mu
    var = jnp.mean(tc * tc, axis=-1, keepdims=True)
    return tc * lax.rsqrt(var + LN_EPS) * g + b


def _dot_nt(a, b):
    return lax.dot_general(a, b, (((1,), (1,)), ((), ())), preferred_element_type=F32)


def _half_masks(shape):
    lane = lax.broadcasted_iota(jnp.int32, shape, 1)
    return lane < HEAD_DIM, lane >= HEAD_DIM


def _mod_kernel(cc_ref, w_ref, b_ref, o_ref):
    a = cc_ref[...]
    s = (a * jax.nn.sigmoid(a)).astype(BF16)
    o_ref[0] = jnp.dot(s, w_ref[0].astype(BF16), preferred_element_type=F32) + b_ref[0]


def _mod_vectors(cc, w_mod, b_mod):
    depth, d, nm = w_mod.shape
    tn = TN_MOD
    return pl.pallas_call(
        _mod_kernel,
        grid=(depth, nm // tn),
        in_specs=[
            pl.BlockSpec((SUBLANES, d), lambda l, j: (0, 0)),
            pl.BlockSpec((1, d, tn), lambda l, j: (l, 0, j)),
            pl.BlockSpec((1, 1, tn), lambda l, j: (l, 0, j)),
        ],
        out_specs=pl.BlockSpec((1, SUBLANES, tn), lambda l, j: (l, 0, j)),
        out_shape=jax.ShapeDtypeStruct((depth, SUBLANES, nm), F32),
        compiler_params=_params(("parallel", "parallel")),
        name="mod_vectors",
    )(cc, w_mod, b_mod.reshape(depth, 1, nm))


def _ffn_kernel(x_ref, sh_ref, sc_ref, gt_ref, lng_ref, lnb_ref, wg_ref, wu_ref, wo_ref, o_ref):
    j = pl.program_id(1)
    last_j = pl.num_programs(1) - 1
    half = x_ref.shape[0] // 2
    halves = [slice(r * half, (r + 1) * half) for r in range(2)]

    @pl.when(j == 0)
    def _():
        o_ref[...] = jnp.zeros_like(o_ref)

    def accumulate():
        for rows in halves:
            xin = (x_ref[rows] * (1.0 + sc_ref[0]) + sh_ref[0]).astype(BF16)
            g = jnp.dot(xin, wg_ref[...], preferred_element_type=F32)
            u = jnp.dot(xin, wu_ref[...], preferred_element_type=F32)
            hh = ((g * jax.nn.sigmoid(g)) * u).astype(BF16)
            o_ref[rows] += jnp.dot(hh, wo_ref[...], preferred_element_type=F32)

    def finalize():
        for rows in halves:
            t = DEEPNORM_ALPHA * x_ref[rows] + (0.5 * gt_ref[0]) * o_ref[rows]
            o_ref[rows] = _ln(t, lng_ref[...], lnb_ref[...])

    accumulate()
    pl.when(j == last_j)(finalize)


def _ffn_kernel_aliased(*refs):
    _ffn_kernel(*refs[:-2], refs[-1])


def _ffn(x, x_ctx, ctx_row0, l_ctx, mods_l, ks, lng, lnb, w_in, w_out, layer, n):
    d = D_MODEL
    nj = D_FF // TF
    assert n % TM_FFN == 0 and n % TM == 0
    with_ctx = x_ctx is not None
    assert not with_ctx or (TM % l_ctx == 0 and n % l_ctx == 0 and ctx_row0 % l_ctx == 0)
    out_rows = n + TM if with_ctx else n

    def call(src, tm, tile0, tm_out, out_tile0, n_tiles, mod0, alias):
        mod_spec = lambda k: pl.BlockSpec((1, 1, d), lambda i, j: (mod0 + k, 0, 0))
        vec_spec = pl.BlockSpec((1, d), lambda i, j: (0, 0))
        in_specs = [
            pl.BlockSpec((tm, d), lambda i, j: (tile0 + i, 0)),
            mod_spec(ks[0]), mod_spec(ks[1]), mod_spec(ks[2]),
            vec_spec, vec_spec,
            pl.BlockSpec((None, d, TF), lambda i, j: (layer, 0, j)),
            pl.BlockSpec((None, d, TF), lambda i, j: (layer, 0, nj + j)),
            pl.BlockSpec((None, TF, d), lambda i, j: (layer, j, 0)),
        ]
        args = [src, mods_l, mods_l, mods_l, lng, lnb, w_in, w_in, w_out]
        kernel_fn = _ffn_kernel
        aliases = {}
        if alias is not None:
            in_specs.append(pl.BlockSpec(memory_space=pl.ANY))
            args.append(alias)
            aliases = {len(args) - 1: 0}
            kernel_fn = _ffn_kernel_aliased
        return pl.pallas_call(
            kernel_fn,
            grid=(n_tiles, nj),
            in_specs=in_specs,
            out_specs=pl.BlockSpec((tm_out, d), lambda i, j: (out_tile0 + i, 0)),
            out_shape=jax.ShapeDtypeStruct((out_rows, d), F32),
            input_output_aliases=aliases,
            compiler_params=_params(("parallel", "arbitrary")),
            name="ffn" if alias is None else "ffn_ctx",
        )(*args)

    y = call(x, TM_FFN, 0, TM_FFN, 0, n // TM_FFN, 0, None)
    if with_ctx:
        y = call(x_ctx, l_ctx, ctx_row0 // l_ctx, TM, n // TM, 1, N_MOD, y)
    return y


def _rope(t, cos_ref, sin_ref):
    w = t.shape[1]
    lane = lax.broadcasted_iota(jnp.int32, t.shape, 1)
    first = (lane % HEAD_DIM) < (HEAD_DIM // 2)
    rot = jnp.where(first, pltpu.roll(t, w - HEAD_DIM // 2, 1), pltpu.roll(t, HEAD_DIM // 2, 1))
    reps = w // LANES
    cos = jnp.concatenate([cos_ref[...]] * reps, axis=1) if reps > 1 else cos_ref[...]
    sin = jnp.concatenate([sin_ref[...]] * reps, axis=1) if reps > 1 else sin_ref[...]
    return t * cos + rot * sin


_SEG_AQ, _SEG_AK, _SEG_AV, _SEG_BU, _SEG_BV, _SEG_CQ, _SEG_CKV, _SEG_DQ, _SEG_DK, _SEG_DV = range(N_SEG)
_SEG_WIDTHS = tuple(2 * C_KV_HEADS * HEAD_DIM if k == _SEG_CKV else GROUP_WIDTH for k in range(N_SEG))
_SEG_OFFSETS = tuple(sum(_SEG_WIDTHS[:k]) for k in range(N_SEG))
D_PROJ = sum(_SEG_WIDTHS)


def _dup_halves(x):
    swapped = pltpu.roll(x, HEAD_DIM, 1)
    first, _ = _half_masks(x.shape)
    return jnp.where(first, x, swapped), jnp.where(first, swapped, x)


def _inproj_kernel(x_ref, sh_ref, sc_ref, w_ref, cos_ref, sin_ref, *out_refs):
    h = (x_ref[...] * (1.0 + sc_ref[0]) + sh_ref[0]).astype(BF16)

    def project(k):
        return jnp.dot(h, w_ref[:, _SEG_OFFSETS[k]:_SEG_OFFSETS[k] + _SEG_WIDTHS[k]], preferred_element_type=F32)

    def finish(k, t):
        o_ref = out_refs[k]
        if k in (_SEG_AQ, _SEG_AK, _SEG_CQ):
            t = _rope(t, cos_ref, sin_ref)
        if k in (_SEG_AQ, _SEG_CQ, _SEG_DQ):
            t = t * (QK_SCALE * LOG2E)
        if k == _SEG_AQ:
            o_ref[0] = t.T.astype(o_ref.dtype)
        elif k == _SEG_AV:
            tt = t.T.astype(o_ref.dtype)
            extra = A_VROWS - 2 * HEAD_DIM
            ones_row = (lax.broadcasted_iota(jnp.int32, (extra, tt.shape[1]), 0) == 0).astype(o_ref.dtype)
            for hd in range(A_HEADS):
                o_ref[0, hd * A_VROWS:hd * A_VROWS + 2 * HEAD_DIM] = tt[hd * 2 * HEAD_DIM:(hd + 1) * 2 * HEAD_DIM]
                o_ref[0, hd * A_VROWS + 2 * HEAD_DIM:(hd + 1) * A_VROWS] = ones_row
        elif k == _SEG_CKV:
            kk = _dup_halves(_rope(t[:, :LANES], cos_ref, sin_ref))
            vv = _dup_halves(t[:, LANES:])
            o_ref[...] = jnp.concatenate([kk[0], kk[1], vv[0], vv[1]], axis=1).astype(o_ref.dtype)
        else:
            o_ref[...] = t.astype(o_ref.dtype)

    t = project(0)
    for k in range(N_SEG):
        t_next = project(k + 1) if k + 1 < N_SEG else None
        finish(k, t)
        t = t_next


def _inproj(x, mods_l, shift_k, scale_k, w_in, layer, cos_t, sin_t, nlt):
    rows = x.shape[0]

    def mod_spec(k):
        return pl.BlockSpec((1, 1, D_MODEL), lambda i: (jnp.where(i >= nlt, N_MOD, 0) + k, 0, 0))

    n_tiles = rows // TM
    d = D_MODEL
    assert C_KV_HEADS == 2 and w_in.shape[1:] == (d, D_PROJ)
    widths = [SEG] * N_SEG
    dtypes = [BF16] * N_SEG
    dtypes[_SEG_BU] = F32
    dtypes[_SEG_BV] = F32
    out_specs = [pl.BlockSpec((TM, w), lambda i: (i, 0)) for w in widths]
    out_shape = [jax.ShapeDtypeStruct((rows, w), dt) for w, dt in zip(widths, dtypes)]
    for k in (_SEG_AQ, _SEG_AV):
        slab = SEG if k == _SEG_AQ else A_HEADS * A_VROWS
        out_specs[k] = pl.BlockSpec((1, slab, TM), lambda i: (i, 0, 0))
        out_shape[k] = jax.ShapeDtypeStruct((n_tiles, slab, TM), BF16)
    return pl.pallas_call(
        _inproj_kernel,
        grid=(n_tiles,),
        in_specs=[
            pl.BlockSpec((TM, d), lambda i: (i, 0)),
            mod_spec(shift_k), mod_spec(scale_k),
            pl.BlockSpec((None, d, D_PROJ), lambda i: (layer, 0, 0), pipeline_mode=pl.Buffered(1)),
            pl.BlockSpec((TM, LANES), lambda i: (i, 0)),
            pl.BlockSpec((TM, LANES), lambda i: (i, 0)),
        ],
        out_specs=out_specs,
        out_shape=out_shape,
        compiler_params=_params(("parallel",)),
        name="inproj",
    )(x, mods_l, mods_l, w_in, cos_t, sin_t)


def _attn_a_kernel(lam_ref, qt_ref, k_ref, vt_ref, *rest, n_chunks, tail_k_start, tail_chunk, tail_len, lam_init,
                   aliased):
    if aliased:
        rest = rest[1:]
    o_ref, m_scr, acc_scr, s_scr, mc_scr = rest
    qt = qt_ref[0]
    comp = lax.broadcasted_iota(jnp.int32, qt.shape, 0) < HEAD_DIM
    zero = jnp.zeros_like(qt)
    qts = (jnp.where(comp, qt, zero), jnp.where(comp, zero, qt))
    m_scr[...] = jnp.full(m_scr.shape, NEG_INF, F32)
    acc_scr[...] = jnp.zeros_like(acc_scr)

    def scores(k, slot, size):
        for c in range(2):
            s = jnp.dot(k, qts[c], preferred_element_type=F32)
            s_scr[slot, c, :size] = s
            mc_scr[slot, c] = jnp.max(s, axis=0, keepdims=True)

    def consume(slot, vt, size):
        for c in range(2):
            m_old = m_scr[c]
            m_new = jnp.maximum(m_old, mc_scr[slot, c])
            alpha = jnp.exp2(m_old - m_new)
            p = jnp.exp2(s_scr[slot, c, :size] - m_new).astype(BF16)
            acc_scr[c] = alpha * acc_scr[c] + jnp.dot(vt, p, preferred_element_type=F32)
            m_scr[c] = m_new

    def k_chunk(i):
        i = jnp.minimum(i, n_chunks - 1)
        return k_ref[pl.ds(pl.multiple_of(i * TM, TM), TM), :]

    scores(k_ref[pl.ds(tail_k_start, tail_len), :], 1, tail_len)
    if n_chunks:
        unroll = math.gcd(n_chunks, A_UNROLL)
        assert unroll % 2 == 0
        scores(k_chunk(0), 0, TM)
    consume(1, vt_ref[tail_chunk][:, :tail_len], tail_len)
    if n_chunks:
        def body(t, carry):
            j = unroll * t
            for u in range(unroll):
                scores(k_chunk(j + u + 1), (u + 1) % 2, TM)
                consume(u % 2, vt_ref[j + u], TM)
            return carry
        lax.fori_loop(0, n_chunks // unroll, body, 0)

    lv = lam_ref[...]
    lam = (jnp.exp(jnp.sum(lv[0:1] * lv[1:2], axis=1, keepdims=True))
           - jnp.exp(jnp.sum(lv[2:3] * lv[3:4], axis=1, keepdims=True)) + lam_init)
    nv = 2 * HEAD_DIM
    ot = (acc_scr[0, :nv] / acc_scr[0, nv:nv + 1]
          - lam * (acc_scr[1, :nv] / acc_scr[1, nv:nv + 1]))
    ms = jnp.mean(ot * ot, axis=0, keepdims=True)
    ot = (ot * lax.rsqrt(ms + LN_EPS)) * (1.0 - lam_init)
    o_ref[...] = ot.T.astype(BF16)


def _attn_a(lam_vecs, aqt, ak, avt, n, l_ctx, lam_init, need_ctx):
    rows = ak.shape[0]
    nt = rows // TM
    nlt = n // TM
    scratch = [pltpu.VMEM((2, 1, TM), F32), pltpu.VMEM((2, A_VROWS, TM), F32),
               pltpu.VMEM((2, 2, TM, TM), F32), pltpu.VMEM((2, 2, 1, TM), F32)]
    lam_spec = pl.BlockSpec((4, HEAD_DIM), lambda h, i: (0, 0))
    ya = pl.pallas_call(
        functools.partial(_attn_a_kernel, n_chunks=nlt, tail_k_start=n, tail_chunk=nlt, tail_len=l_ctx,
                          lam_init=lam_init, aliased=False),
        grid=(A_HEADS, nlt),
        in_specs=[
            lam_spec,
            pl.BlockSpec((1, LANES, TM), lambda h, i: (i, h, 0)),
            pl.BlockSpec((rows, LANES), lambda h, i: (0, h)),
            pl.BlockSpec((nt, A_VROWS, TM), lambda h, i: (0, h, 0)),
        ],
        out_specs=pl.BlockSpec((TM, LANES), lambda h, i: (i, h)),
        out_shape=jax.ShapeDtypeStruct((rows, GROUP_WIDTH), BF16),
        scratch_shapes=scratch,
        compiler_params=_params(("parallel", "parallel")),
        name="attn_a",
    )(lam_vecs, aqt, ak, avt)
    if not need_ctx:
        return ya
    cb = n // l_ctx
    return pl.pallas_call(
        functools.partial(_attn_a_kernel, n_chunks=0, tail_k_start=0, tail_chunk=0, tail_len=l_ctx,
                          lam_init=lam_init, aliased=True),
        grid=(A_HEADS, 1),
        in_specs=[
            lam_spec,
            pl.BlockSpec((1, LANES, TM), lambda h, i: (nlt, h, 0)),
            pl.BlockSpec((l_ctx, LANES), lambda h, i: (cb, h)),
            pl.BlockSpec((1, A_VROWS, TM), lambda h, i: (nlt, h, 0)),
            pl.BlockSpec(memory_space=pl.ANY),
        ],
        out_specs=pl.BlockSpec((TM, LANES), lambda h, i: (nlt, h)),
        out_shape=jax.ShapeDtypeStruct((rows, GROUP_WIDTH), BF16),
        scratch_shapes=scratch,
        input_output_aliases={4: 0},
        compiler_params=_params(("parallel", "parallel")),
        name="attn_a_ctx",
    )(lam_vecs, aqt, ak, avt, ya)


def _mixb_kernel(u_ref, v_ref, g_ref, b_ref, ws_ref, bs_ref, o_ref):
    u = jax.nn.gelu(u_ref[...])
    v = _ln(jax.nn.gelu(v_ref[...]), g_ref[...], b_ref[...]).astype(BF16)
    for c in range(u.shape[0] // CHUNK):
        rs = slice(c * CHUNK, (c + 1) * CHUNK)
        for g in range(B_GROUPS):
            cs = slice(g * LANES, (g + 1) * LANES)
            mixed = jnp.dot(ws_ref[g].astype(BF16), v[rs, cs], preferred_element_type=F32) + bs_ref[:, g:g + 1]
            o_ref[rs, cs] = (u[rs, cs] * mixed).astype(BF16)


def _mixb(bu, bv, gn_g, gn_b, w_s, b_s_t):
    rows = bu.shape[0]
    gw = GROUP_WIDTH
    tile = pl.BlockSpec((TM, gw), lambda i: (i, 0))
    return pl.pallas_call(
        _mixb_kernel,
        grid=(rows // TM,),
        in_specs=[
            tile, tile,
            pl.BlockSpec((1, gw), lambda i: (0, 0)),
            pl.BlockSpec((1, gw), lambda i: (0, 0)),
            pl.BlockSpec((B_GROUPS, CHUNK, CHUNK), lambda i: (0, 0, 0)),
            pl.BlockSpec((CHUNK, B_GROUPS), lambda i: (0, 0)),
        ],
        out_specs=tile,
        out_shape=jax.ShapeDtypeStruct((rows, gw), BF16),
        compiler_params=_params(("parallel",)),
        name="mix_b",
    )(bu, bv, gn_g, gn_b, w_s, b_s_t)


def _paired_heads_attention(score_fn, values, n_pairs, sinks=None):
    all_scores = [score_fn(c, half) for c in range(n_pairs) for half in range(2)]
    all_probs = []
    for idx, scores in enumerate(all_scores):
        m = jnp.max(scores[0], axis=1, keepdims=True)
        for s in scores[1:]:
            m = jnp.maximum(m, jnp.max(s, axis=1, keepdims=True))
        if sinks is not None:
            m = jnp.maximum(m, sinks[idx])
        ps = [jnp.exp2(s - m) for s in scores]
        den = jnp.sum(ps[0], axis=1, keepdims=True)
        for p in ps[1:]:
            den = den + jnp.sum(p, axis=1, keepdims=True)
        if sinks is not None:
            den = den + jnp.exp2(sinks[idx] - m)
        all_probs.append(([p.astype(BF16) for p in ps], den))
    outs = []
    for c in range(n_pairs):
        out = None
        for half in range(2):
            ps, den = all_probs[2 * c + half]
            num = None
            for p, v in zip(ps, values[c]):
                first, second = _half_masks(v.shape)
                pv = jnp.dot(p, jnp.where(first if half == 0 else second, v, jnp.zeros_like(v)),
                             preferred_element_type=F32)
                num = pv if num is None else num + pv
            o = num / den
            out = o if out is None else out + o
        outs.append(out)
    return outs


def _attn_c_kernel(sink_ref, q_ref, k_ref, v_ref, *rest, n, l_ctx, local, aliased):
    o_ref = rest[-1]
    i = pl.program_id(0)
    n_sub = q_ref.shape[0] // QBLOCK
    wlen = 3 * QBLOCK
    ctx_start = n if local else 0
    n_pairs = C_HEADS // 2
    grp_pairs = n_pairs // C_KV_HEADS
    kv_block = lambda c: slice((c // grp_pairs) * LANES, (c // grp_pairs + 1) * LANES)
    sinks = [sink_ref[2 * c + half] * LOG2E for _ in range(n_sub) for c in range(n_pairs) for half in range(2)]
    q_rows = [slice(b * QBLOCK, (b + 1) * QBLOCK) for b in range(n_sub)]
    if local:
        starts, valids = [], []
        for b in range(n_sub):
            blk = i * n_sub + b
            start = pl.multiple_of(jnp.clip((blk - 1) * QBLOCK, 0, n - wlen), QBLOCK)
            qpos = blk * QBLOCK + lax.broadcasted_iota(jnp.int32, (QBLOCK, wlen), 0)
            kpos = start + lax.broadcasted_iota(jnp.int32, (QBLOCK, wlen), 1)
            starts.append(start)
            valids.append(jnp.abs(kpos - qpos) <= WINDOW)

    def score_fn(idx, half):
        b, c = divmod(idx, n_pairs)
        qb = q_ref[q_rows[b], c * LANES:(c + 1) * LANES]
        first, second = _half_masks(qb.shape)
        qm = jnp.where(first if half == 0 else second, qb, jnp.zeros_like(qb))
        scores = [_dot_nt(qm, k_ref[pl.ds(ctx_start, l_ctx), kv_block(c)])]
        if local:
            scores.append(jnp.where(valids[b], _dot_nt(qm, k_ref[pl.ds(starts[b], wlen), kv_block(c)]), NEG_INF))
        return scores

    values = []
    for b in range(n_sub):
        for c in range(n_pairs):
            vals = [v_ref[pl.ds(ctx_start, l_ctx), kv_block(c)]]
            if local:
                vals.append(v_ref[pl.ds(starts[b], wlen), kv_block(c)])
            values.append(vals)
    outs = _paired_heads_attention(score_fn, values, n_sub * n_pairs, sinks)
    for idx, out in enumerate(outs):
        b, c = divmod(idx, n_pairs)
        o_ref[q_rows[b], c * LANES:(c + 1) * LANES] = out.astype(BF16)


def _attn_c(sink, cq, ckv, n, l_ctx, need_ctx):
    rows = cq.shape[0]
    gw = GROUP_WIDTH
    kvw = C_KV_HEADS * LANES
    smem = pl.BlockSpec(memory_space=pltpu.SMEM)
    yc = pl.pallas_call(
        functools.partial(_attn_c_kernel, n=n, l_ctx=l_ctx, local=True, aliased=False),
        grid=(n // TM,),
        in_specs=[
            smem,
            pl.BlockSpec((TM, gw), lambda i: (i, 0)),
            pl.BlockSpec((rows, kvw), lambda i: (0, 0)),
            pl.BlockSpec((rows, kvw), lambda i: (0, 1)),
        ],
        out_specs=pl.BlockSpec((TM, gw), lambda i: (i, 0)),
        out_shape=jax.ShapeDtypeStruct((rows, gw), BF16),
        compiler_params=_params(("parallel",)),
        name="attn_c",
    )(sink, cq, ckv, ckv)
    if not need_ctx:
        return yc
    ct = n // TM
    cb = n // l_ctx
    return pl.pallas_call(
        functools.partial(_attn_c_kernel, n=n, l_ctx=l_ctx, local=False, aliased=True),
        grid=(1,),
        in_specs=[
            smem,
            pl.BlockSpec((TM, gw), lambda i: (ct, 0)),
            pl.BlockSpec((l_ctx, kvw), lambda i: (cb, 0)),
            pl.BlockSpec((l_ctx, kvw), lambda i: (cb, 1)),
            pl.BlockSpec(memory_space=pl.ANY),
        ],
        out_specs=pl.BlockSpec((TM, gw), lambda i: (ct, 0)),
        out_shape=jax.ShapeDtypeStruct((rows, gw), BF16),
        input_output_aliases={4: 0},
        compiler_params=_params(("parallel",)),
        name="attn_c_ctx",
    )(sink, cq, ckv, ckv, yc)


def _attn_d_kernel(q_ref, kc_ref, vc_ref, *rest, local, aliased, n_rows):
    o_ref = rest[-1]
    n_pairs = D_HEADS // 2
    lane_block = lambda c: slice(c * LANES, (c + 1) * LANES)
    nk = NA_ROWS * GRID_W
    if local:
        k_ref, v_ref, b_ref = rest[0], rest[1], rest[2]
        g = pl.program_id(0)
        win0 = jnp.clip(g * D_ROWS - NA_ROWS // 2, 0, n_rows - (D_ROWS + NA_ROWS - 1))
        q_rows, offs, shifts = [], [], []
        for i in range(D_ROWS):
            r = g * D_ROWS + i
            rs = jnp.clip(r - NA_ROWS // 2, 0, n_rows - NA_ROWS)
            q_rows.append(slice(i * GRID_W, (i + 1) * GRID_W))
            offs.append(pl.multiple_of((rs - win0) * GRID_W, GRID_W))
            shifts.append(r - rs)
    else:
        q_rows = [slice(0, q_ref.shape[0])]

    def score_fn(idx, half):
        i, c = divmod(idx, n_pairs)
        qb = q_ref[q_rows[i], lane_block(c)]
        first, second = _half_masks(qb.shape)
        qm = jnp.where(first if half == 0 else second, qb, jnp.zeros_like(qb))
        scores = [_dot_nt(qm, kc_ref[:, lane_block(c)])]
        if local:
            scores.append(_dot_nt(qm, k_ref[pl.ds(offs[i], nk), lane_block(c)]) + b_ref[shifts[i], 2 * c + half])
        return scores

    values = []
    for i in range(len(q_rows)):
        for c in range(n_pairs):
            vals = [vc_ref[:, lane_block(c)]]
            if local:
                vals.append(v_ref[pl.ds(offs[i], nk), lane_block(c)])
            values.append(vals)
    outs = _paired_heads_attention(score_fn, values, len(q_rows) * n_pairs)
    for idx, out in enumerate(outs):
        i, c = divmod(idx, n_pairs)
        o_ref[q_rows[i], lane_block(c)] = out.astype(BF16)


def _attn_d(dq, dk, dv, bias, n, l_ctx, need_ctx):
    rows = dq.shape[0]
    gw = GROUP_WIDTH
    n_rows = n // GRID_W
    nk = NA_ROWS * GRID_W
    cb = n // l_ctx
    win = D_ROWS + NA_ROWS - 1
    assert n_rows % D_ROWS == 0 and n_rows >= win

    def window_start(g):
        return jnp.clip(g * D_ROWS - NA_ROWS // 2, 0, n_rows - win) * GRID_W

    once = dict(pipeline_mode=pl.Buffered(1))
    yd = pl.pallas_call(
        functools.partial(_attn_d_kernel, local=True, aliased=False, n_rows=n_rows),
        grid=(n_rows // D_ROWS,),
        in_specs=[
            pl.BlockSpec((D_ROWS * GRID_W, gw), lambda g: (g, 0)),
            pl.BlockSpec((l_ctx, gw), lambda g: (cb, 0), **once),
            pl.BlockSpec((l_ctx, gw), lambda g: (cb, 0), **once),
            pl.BlockSpec((pl.Element(win * GRID_W), pl.Element(gw)), lambda g: (window_start(g), 0)),
            pl.BlockSpec((pl.Element(win * GRID_W), pl.Element(gw)), lambda g: (window_start(g), 0)),
            pl.BlockSpec((NA_ROWS, D_HEADS, GRID_W, nk), lambda g: (0, 0, 0, 0), **once),
        ],
        out_specs=pl.BlockSpec((D_ROWS * GRID_W, gw), lambda g: (g, 0)),
        out_shape=jax.ShapeDtypeStruct((rows, gw), BF16),
        compiler_params=_params(("parallel",)),
        name="attn_d",
    )(dq, dk, dv, dk, dv, bias)
    if not need_ctx:
        return yd
    ct = n // TM
    return pl.pallas_call(
        functools.partial(_attn_d_kernel, local=False, aliased=True, n_rows=n_rows),
        grid=(1,),
        in_specs=[
            pl.BlockSpec((TM, gw), lambda i: (ct, 0)),
            pl.BlockSpec((l_ctx, gw), lambda i: (cb, 0)),
            pl.BlockSpec((l_ctx, gw), lambda i: (cb, 0)),
            pl.BlockSpec(memory_space=pl.ANY),
        ],
        out_specs=pl.BlockSpec((TM, gw), lambda i: (ct, 0)),
        out_shape=jax.ShapeDtypeStruct((rows, gw), BF16),
        input_output_aliases={3: 0},
        compiler_params=_params(("parallel",)),
        name="attn_d_ctx",
    )(dq, dk, dv, yd)


def _outproj_kernel(ya_ref, yb_ref, yc_ref, yd_ref, w_ref, x_ref, gt_ref, lng_ref, lnb_ref, o_ref):
    y_in = jnp.concatenate([ya_ref[...], yb_ref[...], yc_ref[...], yd_ref[...]], axis=1)
    y = jnp.dot(y_in, w_ref[...], preferred_element_type=F32)
    t = DEEPNORM_ALPHA * x_ref[...] + gt_ref[0] * y
    o_ref[...] = _ln(t, lng_ref[...], lnb_ref[...])


def _outproj(ya, yb, yc, yd, w_out, layer, x, mods_l, gate_k, lng, lnb, n_tiles, nlt):
    d = D_MODEL
    gw = GROUP_WIDTH
    ytile = pl.BlockSpec((TM, gw), lambda i: (i, 0))
    vec = pl.BlockSpec((1, d), lambda i: (0, 0))
    return pl.pallas_call(
        _outproj_kernel,
        grid=(n_tiles,),
        in_specs=[
            ytile, ytile, ytile, ytile,
            pl.BlockSpec((None, d, d), lambda i: (layer, 0, 0), pipeline_mode=pl.Buffered(1)),
            pl.BlockSpec((TM, d), lambda i: (i, 0)),
            pl.BlockSpec((1, 1, d), lambda i: (jnp.where(i >= nlt, N_MOD, 0) + gate_k, 0, 0)),
            vec, vec,
        ],
        out_specs=pl.BlockSpec((TM, d), lambda i: (i, 0)),
        out_shape=jax.ShapeDtypeStruct((n_tiles * TM, d), F32),
        compiler_params=_params(("parallel",)),
        name="outproj",
    )(ya, yb, yc, yd, w_out, x, mods_l, lng, lnb)


def _rope_tables(n, rows):
    t = jnp.arange(n, dtype=jnp.int32)
    row = (t // GRID_W).astype(F32)
    col = (t % GRID_W).astype(F32)
    n_freq = HEAD_DIM // 4
    inv = ROPE_THETA ** (-jnp.arange(n_freq, dtype=F32) / n_freq)
    ang = jnp.concatenate([row[:, None] * inv, col[:, None] * inv], axis=-1)
    cos, sin = jnp.cos(ang), jnp.sin(ang)
    cos_t = jnp.tile(jnp.concatenate([cos, cos], axis=-1), (1, LANES // HEAD_DIM))
    sin_t = jnp.tile(jnp.concatenate([-sin, sin], axis=-1), (1, LANES // HEAD_DIM))
    pad = rows - n
    cos_t = jnp.concatenate([cos_t, jnp.ones((pad, LANES), F32)], axis=0)
    sin_t = jnp.concatenate([sin_t, jnp.zeros((pad, LANES), F32)], axis=0)
    return cos_t, sin_t


def _na_bias(rpb):
    w = GRID_W
    cq = jnp.arange(w)
    cs = jnp.clip(cq - NA_COLS // 2, 0, w - NA_COLS)
    col_ok = (cq[None, :] >= cs[:, None]) & (cq[None, :] < cs[:, None] + NA_COLS)
    edge = w - NA_COLS
    ext = jnp.pad(rpb.astype(F32), ((0, 0), (0, 0), (edge, edge)), mode="edge")
    toep = jnp.stack([ext[:, :, w - 1 - q:2 * w - 1 - q] for q in range(w)], axis=2)
    toep = jnp.where(col_ok[None, None], toep, NEG_INF)
    per_shift = []
    for shift in range(NA_ROWS):
        sl = toep[:, NA_ROWS - 1 - shift:2 * NA_ROWS - 1 - shift]
        per_shift.append(jnp.swapaxes(sl, 1, 2).reshape(rpb.shape[0], w, NA_ROWS * w))
    return jnp.stack(per_shift, axis=0) * LOG2E


def kernel(x, c, ctx, c_ctx, w_mod, b_mod, ln_g, ln_b, ffn1_w_in, ffn1_w_out, ffn2_w_in, ffn2_w_out, mix_w_in, mix_w_out, a_lambda, b_norm_g, b_norm_b, b_spatial_w, b_spatial_b, c_sink, d_rpb):
    depth = w_mod.shape[0]
    n = x.shape[1]
    l_ctx = ctx.shape[1]
    d = D_MODEL
    assert x.shape[0] == 1 and n % TM == 0 and n // GRID_W >= NA_ROWS and n % TM_FFN == 0
    assert l_ctx % CHUNK == 0 and l_ctx <= TM and n % l_ctx == 0
    nlt = n // TM
    rows = n + TM

    cc = jnp.concatenate([c, c_ctx[None], jnp.zeros((SUBLANES - 2, d), F32)], axis=0)
    mods = _mod_vectors(cc, w_mod, b_mod)[:, :2].reshape(depth, 2 * N_MOD, 1, d)
    cos_t, sin_t = _rope_tables(n, rows)
    w1_in, w1_out, w2_in, w2_out, wm_in, wm_out = (
        w.astype(BF16) for w in (ffn1_w_in, ffn1_w_out, ffn2_w_in, ffn2_w_out, mix_w_in, mix_w_out))

    for l in range(depth):
        last = l == depth - 1
        lam_init = 0.8 - 0.6 * math.exp(-0.3 * l)
        mods_l = mods[l]
        lng = [ln_g[l, k][None] for k in range(3)]
        lnb = [ln_b[l, k][None] for k in range(3)]

        if l == 0:
            xs = _ffn(x[0], ctx[0], 0, l_ctx, mods_l, (0, 1, 2), lng[0], lnb[0], w1_in, w1_out, l, n)
        else:
            xs = _ffn(xs, xs, n, l_ctx, mods_l, (0, 1, 2), lng[0], lnb[0], w1_in, w1_out, l, n)
        aq, ak, av, bu, bv, cq, ckv, dq, dk, dv = _inproj(xs, mods_l, 3, 4, wm_in, l, cos_t, sin_t, nlt)
        ya = _attn_a(a_lambda[l], aq, ak, av, n, l_ctx, lam_init, not last)
        yb = _mixb(bu, bv, b_norm_g[l][None], b_norm_b[l][None], b_spatial_w[l], b_spatial_b[l].T)
        yc = _attn_c(c_sink[l], cq, ckv, n, l_ctx, not last)
        yd = _attn_d(dq, dk, dv, _na_bias(d_rpb[l]), n, l_ctx, not last)
        n_tiles = nlt if last else nlt + 1
        xs = _outproj(ya, yb, yc, yd, wm_out, l, xs, mods_l, 5, lng[1], lnb[1], n_tiles, nlt)
        xs = _ffn(xs, None if last else xs, n, l_ctx, mods_l, (6, 7, 8), lng[2], lnb[2], w2_in, w2_out, l, n)
    return xs[None]
```

```python
import functools
import math

import jax
import jax.numpy as jnp
from jax import lax
from jax.experimental import pallas as pl
from jax.experimental.pallas import tpu as pltpu

F32 = jnp.float32
BF16 = jnp.bfloat16

D_MODEL = 2048
N_GROUPS = 4
GROUP_WIDTH = D_MODEL // N_GROUPS
HEAD_DIM = 64
GRID_W = 64
CHUNK = 128
B_GROUPS = GROUP_WIDTH // CHUNK
A_HEADS = GROUP_WIDTH // (2 * HEAD_DIM)
C_HEADS = GROUP_WIDTH // HEAD_DIM
C_KV_HEADS = C_HEADS // 4
D_HEADS = GROUP_WIDTH // HEAD_DIM
WINDOW = 128
QBLOCK = 128
NA_ROWS = 8
NA_COLS = 16
D_FF = 256 * math.ceil(8 * D_MODEL / 3 / 256)
N_MOD = 9
ROPE_THETA = 10000.0
LN_EPS = 1e-6
NEG_INF = -1e30
MODEL_DEPTH = 4
DEEPNORM_ALPHA = (2 * MODEL_DEPTH) ** 0.25
QK_SCALE = HEAD_DIM ** -0.5
LOG2E = math.log2(math.e)

LANES = 128
SUBLANES = 8
BF16_SUBLANES = 16
A_VROWS = 2 * HEAD_DIM + BF16_SUBLANES
A_UNROLL = 16
D_ROWS = 8
TN_MOD = 2048
TM = 512
TM_FFN = 1024
TF = 512
SEG = 512
N_SEG = 10
VMEM_LIMIT = 60 * 1024 * 1024


def _params(sem):
    return pltpu.CompilerParams(dimension_semantics=sem, vmem_limit_bytes=VMEM_LIMIT)


def _ln(t, g, b):
    mu = jnp.mean(t, axis=-1, keepdims=True)
    tc = t - mu
    var = jnp.mean(tc * tc, axis=-1, keepdims=True)
    return tc * lax.rsqrt(var + LN_EPS) * g + b


def _dot_nt(a, b):
    return lax.dot_general(a, b, (((1,), (1,)), ((), ())), preferred_element_type=F32)


def _half_masks(shape):
    lane = lax.broadcasted_iota(jnp.int32, shape, 1)
    return lane < HEAD_DIM, lane >= HEAD_DIM


def _mod_kernel(cc_ref, w_ref, b_ref, o_ref):
    a = cc_ref[...]
    s = (a * jax.nn.sigmoid(a)).astype(BF16)
    o_ref[0] = jnp.dot(s, w_ref[0].astype(BF16), preferred_element_type=F32) + b_ref[0]


def _mod_vectors(cc, w_mod, b_mod):
    depth, d, nm = w_mod.shape
    tn = TN_MOD
    return pl.pallas_call(
        _mod_kernel,
        grid=(depth, nm // tn),
        in_specs=[
            pl.BlockSpec((SUBLANES, d), lambda l, j: (0, 0)),
            pl.BlockSpec((1, d, tn), lambda l, j: (l, 0, j)),
            pl.BlockSpec((1, 1, tn), lambda l, j: (l, 0, j)),
        ],
        out_specs=pl.BlockSpec((1, SUBLANES, tn), lambda l, j: (l, 0, j)),
        out_shape=jax.ShapeDtypeStruct((depth, SUBLANES, nm), F32),
        compiler_params=_params(("parallel", "parallel")),
        name="mod_vectors",
    )(cc, w_mod, b_mod.reshape(depth, 1, nm))


def _ffn_kernel(x_ref, sh_ref, sc_ref, gt_ref, lng_ref, lnb_ref, wg_ref, wu_ref, wo_ref, o_ref):
    j = pl.program_id(1)
    last_j = pl.num_programs(1) - 1
    half = x_ref.shape[0] // 2
    halves = [slice(r * half, (r + 1) * half) for r in range(2)]

    @pl.when(j == 0)
    def _():
        o_ref[...] = jnp.zeros_like(o_ref)

    def accumulate():
        for rows in halves:
            xin = (x_ref[rows] * (1.0 + sc_ref[0]) + sh_ref[0]).astype(BF16)
            g = jnp.dot(xin, wg_ref[...], preferred_element_type=F32)
            u = jnp.dot(xin, wu_ref[...], preferred_element_type=F32)
            hh = ((g * jax.nn.sigmoid(g)) * u).astype(BF16)
            o_ref[rows] += jnp.dot(hh, wo_ref[...], preferred_element_type=F32)

    def finalize():
        for rows in halves:
            t = DEEPNORM_ALPHA * x_ref[rows] + (0.5 * gt_ref[0]) * o_ref[rows]
            o_ref[rows] = _ln(t, lng_ref[...], lnb_ref[...])

    accumulate()
    pl.when(j == last_j)(finalize)


def _ffn_kernel_aliased(*refs):
    _ffn_kernel(*refs[:-2], refs[-1])


def _ffn(x, x_ctx, ctx_row0, l_ctx, mods_l, ks, lng, lnb, w_in, w_out, layer, n):
    d = D_MODEL
    nj = D_FF // TF
    assert n % TM_FFN == 0 and n % TM == 0
    with_ctx = x_ctx is not None
    assert not with_ctx or (TM % l_ctx == 0 and n % l_ctx == 0 and ctx_row0 % l_ctx == 0)
    out_rows = n + TM if with_ctx else n

    def call(src, tm, tile0, tm_out, out_tile0, n_tiles, mod0, alias):
        mod_spec = lambda k: pl.BlockSpec((1, 1, d), lambda i, j: (mod0 + k, 0, 0))
        vec_spec = pl.BlockSpec((1, d), lambda i, j: (0, 0))
        in_specs = [
            pl.BlockSpec((tm, d), lambda i, j: (tile0 + i, 0)),
            mod_spec(ks[0]), mod_spec(ks[1]), mod_spec(ks[2]),
            vec_spec, vec_spec,
            pl.BlockSpec((None, d, TF), lambda i, j: (layer, 0, j)),
            pl.BlockSpec((None, d, TF), lambda i, j: (layer, 0, nj + j)),
            pl.BlockSpec((None, TF, d), lambda i, j: (layer, j, 0)),
        ]
        args = [src, mods_l, mods_l, mods_l, lng, lnb, w_in, w_in, w_out]
        kernel_fn = _ffn_kernel
        aliases = {}
        if alias is not None:
            in_specs.append(pl.BlockSpec(memory_space=pl.ANY))
            args.append(alias)
            aliases = {len(args) - 1: 0}
            kernel_fn = _ffn_kernel_aliased
        return pl.pallas_call(
            kernel_fn,
            grid=(n_tiles, nj),
            in_specs=in_specs,
            out_specs=pl.BlockSpec((tm_out, d), lambda i, j: (out_tile0 + i, 0)),
            out_shape=jax.ShapeDtypeStruct((out_rows, d), F32),
            input_output_aliases=aliases,
            compiler_params=_params(("parallel", "arbitrary")),
            name="ffn" if alias is None else "ffn_ctx",
        )(*args)

    y = call(x, TM_FFN, 0, TM_FFN, 0, n // TM_FFN, 0, None)
    if with_ctx:
        y = call(x_ctx, l_ctx, ctx_row0 // l_ctx, TM, n // TM, 1, N_MOD, y)
    return y


def _rope(t, cos_ref, sin_ref):
    w = t.shape[1]
    lane = lax.broadcasted_iota(jnp.int32, t.shape, 1)
    first = (lane % HEAD_DIM) < (HEAD_DIM // 2)
    rot = jnp.where(first, pltpu.roll(t, w - HEAD_DIM // 2, 1), pltpu.roll(t, HEAD_DIM // 2, 1))
    reps = w // LANES
    cos = jnp.concatenate([cos_ref[...]] * reps, axis=1) if reps > 1 else cos_ref[...]
    sin = jnp.concatenate([sin_ref[...]] * reps, axis=1) if reps > 1 else sin_ref[...]
    return t * cos + rot * sin


_SEG_AQ, _SEG_AK, _SEG_AV, _SEG_BU, _SEG_BV, _SEG_CQ, _SEG_CKV, _SEG_DQ, _SEG_DK, _SEG_DV = range(N_SEG)
_SEG_WIDTHS = tuple(2 * C_KV_HEADS * HEAD_DIM if k == _SEG_CKV else GROUP_WIDTH for k in range(N_SEG))
_SEG_OFFSETS = tuple(sum(_SEG_WIDTHS[:k]) for k in range(N_SEG))
D_PROJ = sum(_SEG_WIDTHS)


def _dup_halves(x):
    swapped = pltpu.roll(x, HEAD_DIM, 1)
    first, _ = _half_masks(x.shape)
    return jnp.where(first, x, swapped), jnp.where(first, swapped, x)


def _inproj_kernel(x_ref, sh_ref, sc_ref, w_ref, cos_ref, sin_ref, *out_refs):
    h = (x_ref[...] * (1.0 + sc_ref[0]) + sh_ref[0]).astype(BF16)

    def project(k):
        return jnp.dot(h, w_ref[:, _SEG_OFFSETS[k]:_SEG_OFFSETS[k] + _SEG_WIDTHS[k]], preferred_element_type=F32)

    def finish(k, t):
        o_ref = out_refs[k]
        if k in (_SEG_AQ, _SEG_AK, _SEG_CQ):
            t = _rope(t, cos_ref, sin_ref)
        if k in (_SEG_AQ, _SEG_CQ, _SEG_DQ):
            t = t * (QK_SCALE * LOG2E)
        if k == _SEG_AQ:
            o_ref[0] = t.T.astype(o_ref.dtype)
        elif k == _SEG_AV:
            tt = t.T.astype(o_ref.dtype)
            extra = A_VROWS - 2 * HEAD_DIM
            ones_row = (lax.broadcasted_iota(jnp.int32, (extra, tt.shape[1]), 0) == 0).astype(o_ref.dtype)
            for hd in range(A_HEADS):
                o_ref[0, hd * A_VROWS:hd * A_VROWS + 2 * HEAD_DIM] = tt[hd * 2 * HEAD_DIM:(hd + 1) * 2 * HEAD_DIM]
                o_ref[0, hd * A_VROWS + 2 * HEAD_DIM:(hd + 1) * A_VROWS] = ones_row
        elif k == _SEG_CKV:
            kk = _dup_halves(_rope(t[:, :LANES], cos_ref, sin_ref))
            vv = _dup_halves(t[:, LANES:])
            o_ref[...] = jnp.concatenate([kk[0], kk[1], vv[0], vv[1]], axis=1).astype(o_ref.dtype)
        else:
            o_ref[...] = t.astype(o_ref.dtype)

    t = project(0)
    for k in range(N_SEG):
        t_next = project(k + 1) if k + 1 < N_SEG else None
        finish(k, t)
        t = t_next


def _inproj(x, mods_l, shift_k, scale_k, w_in, layer, cos_t, sin_t, nlt):
    rows = x.shape[0]

    def mod_spec(k):
        return pl.BlockSpec((1, 1, D_MODEL), lambda i: (jnp.where(i >= nlt, N_MOD, 0) + k, 0, 0))

    n_tiles = rows // TM
    d = D_MODEL
    assert C_KV_HEADS == 2 and w_in.shape[1:] == (d, D_PROJ)
    widths = [SEG] * N_SEG
    dtypes = [BF16] * N_SEG
    dtypes[_SEG_BU] = F32
    dtypes[_SEG_BV] = F32
    out_specs = [pl.BlockSpec((TM, w), lambda i: (i, 0)) for w in widths]
    out_shape = [jax.ShapeDtypeStruct((rows, w), dt) for w, dt in zip(widths, dtypes)]
    for k in (_SEG_AQ, _SEG_AV):
        slab = SEG if k == _SEG_AQ else A_HEADS * A_VROWS
        out_specs[k] = pl.BlockSpec((1, slab, TM), lambda i: (i, 0, 0))
        out_shape[k] = jax.ShapeDtypeStruct((n_tiles, slab, TM), BF16)
    return pl.pallas_call(
        _inproj_kernel,
        grid=(n_tiles,),
        in_specs=[
            pl.BlockSpec((TM, d), lambda i: (i, 0)),
            mod_spec(shift_k), mod_spec(scale_k),
            pl.BlockSpec((None, d, D_PROJ), lambda i: (layer, 0, 0), pipeline_mode=pl.Buffered(1)),
            pl.BlockSpec((TM, LANES), lambda i: (i, 0)),
            pl.BlockSpec((TM, LANES), lambda i: (i, 0)),
        ],
        out_specs=out_specs,
        out_shape=out_shape,
        compiler_params=_params(("parallel",)),
        name="inproj",
    )(x, mods_l, mods_l, w_in, cos_t, sin_t)


def _attn_a_kernel(lam_ref, qt_ref, k_ref, vt_ref, *rest, n_chunks, tail_k_start, tail_chunk, tail_len, lam_init,
                   aliased):
    if aliased:
        rest = rest[1:]
    o_ref, m_scr, acc_scr, s_scr, mc_scr = rest
    qt = qt_ref[0]
    comp = lax.broadcasted_iota(jnp.int32, qt.shape, 0) < HEAD_DIM
    zero = jnp.zeros_like(qt)
    qts = (jnp.where(comp, qt, zero), jnp.where(comp, zero, qt))
    m_scr[...] = jnp.full(m_scr.shape, NEG_INF, F32)
    acc_scr[...] = jnp.zeros_like(acc_scr)

    def scores(k, slot, size):
        for c in range(2):
            s = jnp.dot(k, qts[c], preferred_element_type=F32)
            s_scr[slot, c, :size] = s
            mc_scr[slot, c] = jnp.max(s, axis=0, keepdims=True)

    def consume(slot, vt, size):
        for c in range(2):
            m_old = m_scr[c]
            m_new = jnp.maximum(m_old, mc_scr[slot, c])
            alpha = jnp.exp2(m_old - m_new)
            p = jnp.exp2(s_scr[slot, c, :size] - m_new).astype(BF16)
            acc_scr[c] = alpha * acc_scr[c] + jnp.dot(vt, p, preferred_element_type=F32)
            m_scr[c] = m_new

    def k_chunk(i):
        i = jnp.minimum(i, n_chunks - 1)
        return k_ref[pl.ds(pl.multiple_of(i * TM, TM), TM), :]

    scores(k_ref[pl.ds(tail_k_start, tail_len), :], 1, tail_len)
    if n_chunks:
        unroll = math.gcd(n_chunks, A_UNROLL)
        assert unroll % 2 == 0
        scores(k_chunk(0), 0, TM)
    consume(1, vt_ref[tail_chunk][:, :tail_len], tail_len)
    if n_chunks:
        def body(t, carry):
            j = unroll * t
            for u in range(unroll):
                scores(k_chunk(j + u + 1), (u + 1) % 2, TM)
                consume(u % 2, vt_ref[j + u], TM)
            return carry
        lax.fori_loop(0, n_chunks // unroll, body, 0)

    lv = lam_ref[...]
    lam = (jnp.exp(jnp.sum(lv[0:1] * lv[1:2], axis=1, keepdims=True))
           - jnp.exp(jnp.sum(lv[2:3] * lv[3:4], axis=1, keepdims=True)) + lam_init)
    nv = 2 * HEAD_DIM
    ot = (acc_scr[0, :nv] / acc_scr[0, nv:nv + 1]
          - lam * (acc_scr[1, :nv] / acc_scr[1, nv:nv + 1]))
    ms = jnp.mean(ot * ot, axis=0, keepdims=True)
    ot = (ot * lax.rsqrt(ms + LN_EPS)) * (1.0 - lam_init)
    o_ref[...] = ot.T.astype(BF16)


def _attn_a(lam_vecs, aqt, ak, avt, n, l_ctx, lam_init, need_ctx):
    rows = ak.shape[0]
    nt = rows // TM
    nlt = n // TM
    scratch = [pltpu.VMEM((2, 1, TM), F32), pltpu.VMEM((2, A_VROWS, TM), F32),
               pltpu.VMEM((2, 2, TM, TM), F32), pltpu.VMEM((2, 2, 1, TM), F32)]
    lam_spec = pl.BlockSpec((4, HEAD_DIM), lambda h, i: (0, 0))
    ya = pl.pallas_call(
        functools.partial(_attn_a_kernel, n_chunks=nlt, tail_k_start=n, tail_chunk=nlt, tail_len=l_ctx,
                          lam_init=lam_init, aliased=False),
        grid=(A_HEADS, nlt),
        in_specs=[
            lam_spec,
            pl.BlockSpec((1, LANES, TM), lambda h, i: (i, h, 0)),
            pl.BlockSpec((rows, LANES), lambda h, i: (0, h)),
            pl.BlockSpec((nt, A_VROWS, TM), lambda h, i: (0, h, 0)),
        ],
        out_specs=pl.BlockSpec((TM, LANES), lambda h, i: (i, h)),
        out_shape=jax.ShapeDtypeStruct((rows, GROUP_WIDTH), BF16),
        scratch_shapes=scratch,
        compiler_params=_params(("parallel", "parallel")),
        name="attn_a",
    )(lam_vecs, aqt, ak, avt)
    if not need_ctx:
        return ya
    cb = n // l_ctx
    return pl.pallas_call(
        functools.partial(_attn_a_kernel, n_chunks=0, tail_k_start=0, tail_chunk=0, tail_len=l_ctx,
                          lam_init=lam_init, aliased=True),
        grid=(A_HEADS, 1),
        in_specs=[
            lam_spec,
            pl.BlockSpec((1, LANES, TM), lambda h, i: (nlt, h, 0)),
            pl.BlockSpec((l_ctx, LANES), lambda h, i: (cb, h)),
            pl.BlockSpec((1, A_VROWS, TM), lambda h, i: (nlt, h, 0)),
            pl.BlockSpec(memory_space=pl.ANY),
        ],
        out_specs=pl.BlockSpec((TM, LANES), lambda h, i: (nlt, h)),
        out_shape=jax.ShapeDtypeStruct((rows, GROUP_WIDTH), BF16),
        scratch_shapes=scratch,
        input_output_aliases={4: 0},
        compiler_params=_params(("parallel", "parallel")),
        name="attn_a_ctx",
    )(lam_vecs, aqt, ak, avt, ya)


def _mixb_kernel(u_ref, v_ref, g_ref, b_ref, ws_ref, bs_ref, o_ref):
    u = jax.nn.gelu(u_ref[...])
    v = _ln(jax.nn.gelu(v_ref[...]), g_ref[...], b_ref[...]).astype(BF16)
    for c in range(u.shape[0] // CHUNK):
        rs = slice(c * CHUNK, (c + 1) * CHUNK)
        for g in range(B_GROUPS):
            cs = slice(g * LANES, (g + 1) * LANES)
            mixed = jnp.dot(ws_ref[g].astype(BF16), v[rs, cs], preferred_element_type=F32) + bs_ref[:, g:g + 1]
            o_ref[rs, cs] = (u[rs, cs] * mixed).astype(BF16)


def _mixb(bu, bv, gn_g, gn_b, w_s, b_s_t):
    rows = bu.shape[0]
    gw = GROUP_WIDTH
    tile = pl.BlockSpec((TM, gw), lambda i: (i, 0))
    return pl.pallas_call(
        _mixb_kernel,
        grid=(rows // TM,),
        in_specs=[
            tile, tile,
            pl.BlockSpec((1, gw), lambda i: (0, 0)),
            pl.BlockSpec((1, gw), lambda i: (0, 0)),
            pl.BlockSpec((B_GROUPS, CHUNK, CHUNK), lambda i: (0, 0, 0)),
            pl.BlockSpec((CHUNK, B_GROUPS), lambda i: (0, 0)),
        ],
        out_specs=tile,
        out_shape=jax.ShapeDtypeStruct((rows, gw), BF16),
        compiler_params=_params(("parallel",)),
        name="mix_b",
    )(bu, bv, gn_g, gn_b, w_s, b_s_t)


def _stack_pair(qb):
    first, second = _half_masks(qb.shape)
    zero = jnp.zeros_like(qb)
    return jnp.concatenate([jnp.where(first, qb, zero), jnp.where(second, qb, zero)], axis=0)


def _paired_heads_attention(score_fn, values, n_pairs, sinks=None):
    all_scores = [score_fn(c) for c in range(n_pairs)]
    all_probs = []
    for c, scores in enumerate(all_scores):
        m = jnp.max(scores[0], axis=1, keepdims=True)
        for s in scores[1:]:
            m = jnp.maximum(m, jnp.max(s, axis=1, keepdims=True))
        if sinks is not None:
            m = jnp.maximum(m, sinks[c])
        ps = [jnp.exp2(s - m) for s in scores]
        den = jnp.sum(ps[0], axis=1, keepdims=True)
        for p in ps[1:]:
            den = den + jnp.sum(p, axis=1, keepdims=True)
        if sinks is not None:
            den = den + jnp.exp2(sinks[c] - m)
        all_probs.append(([p.astype(BF16) for p in ps], den))
    outs = []
    for c in range(n_pairs):
        ps, den = all_probs[c]
        num = None
        for p, v in zip(ps, values[c]):
            pv = jnp.dot(p, v, preferred_element_type=F32)
            num = pv if num is None else num + pv
        o = num / den
        r = o.shape[0] // 2
        first, _ = _half_masks((r, LANES))
        outs.append(jnp.where(first, o[:r], o[r:]))
    return outs


def _attn_c_kernel(sink_ref, q_ref, k_ref, v_ref, *rest, n, l_ctx, local, aliased):
    o_ref = rest[-1]
    i = pl.program_id(0)
    n_sub = q_ref.shape[0] // QBLOCK
    wlen = 3 * QBLOCK
    ctx_start = n if local else 0
    n_pairs = C_HEADS // 2
    grp_pairs = n_pairs // C_KV_HEADS
    kv_block = lambda c: slice((c // grp_pairs) * LANES, (c // grp_pairs + 1) * LANES)
    top = lax.broadcasted_iota(jnp.int32, (2 * QBLOCK, 1), 0) < QBLOCK
    pair_sinks = [jnp.where(top, sink_ref[2 * c] * LOG2E, sink_ref[2 * c + 1] * LOG2E) for c in range(n_pairs)]
    sinks = [pair_sinks[c] for _ in range(n_sub) for c in range(n_pairs)]
    q_rows = [slice(b * QBLOCK, (b + 1) * QBLOCK) for b in range(n_sub)]
    if local:
        starts, valids = [], []
        for b in range(n_sub):
            blk = i * n_sub + b
            start = pl.multiple_of(jnp.clip((blk - 1) * QBLOCK, 0, n - wlen), QBLOCK)
            qpos = blk * QBLOCK + lax.broadcasted_iota(jnp.int32, (QBLOCK, wlen), 0)
            kpos = start + lax.broadcasted_iota(jnp.int32, (QBLOCK, wlen), 1)
            starts.append(start)
            valid = jnp.abs(kpos - qpos) <= WINDOW
            valids.append(jnp.concatenate([valid, valid], axis=0))

    def score_fn(idx):
        b, c = divmod(idx, n_pairs)
        q2 = _stack_pair(q_ref[q_rows[b], c * LANES:(c + 1) * LANES])
        scores = [_dot_nt(q2, k_ref[pl.ds(ctx_start, l_ctx), kv_block(c)])]
        if local:
            scores.append(jnp.where(valids[b], _dot_nt(q2, k_ref[pl.ds(starts[b], wlen), kv_block(c)]), NEG_INF))
        return scores

    values = []
    for b in range(n_sub):
        for c in range(n_pairs):
            vals = [v_ref[pl.ds(ctx_start, l_ctx), kv_block(c)]]
            if local:
                vals.append(v_ref[pl.ds(starts[b], wlen), kv_block(c)])
            values.append(vals)
    outs = _paired_heads_attention(score_fn, values, n_sub * n_pairs, sinks)
    for idx, out in enumerate(outs):
        b, c = divmod(idx, n_pairs)
        o_ref[q_rows[b], c * LANES:(c + 1) * LANES] = out.astype(BF16)


def _attn_c(sink, cq, ckv, n, l_ctx, need_ctx):
    rows = cq.shape[0]
    gw = GROUP_WIDTH
    kvw = C_KV_HEADS * LANES
    smem = pl.BlockSpec(memory_space=pltpu.SMEM)
    yc = pl.pallas_call(
        functools.partial(_attn_c_kernel, n=n, l_ctx=l_ctx, local=True, aliased=False),
        grid=(n // TM,),
        in_specs=[
            smem,
            pl.BlockSpec((TM, gw), lambda i: (i, 0)),
            pl.BlockSpec((rows, kvw), lambda i: (0, 0)),
            pl.BlockSpec((rows, kvw), lambda i: (0, 1)),
        ],
        out_specs=pl.BlockSpec((TM, gw), lambda i: (i, 0)),
        out_shape=jax.ShapeDtypeStruct((rows, gw), BF16),
        compiler_params=_params(("parallel",)),
        name="attn_c",
    )(sink, cq, ckv, ckv)
    if not need_ctx:
        return yc
    ct = n // TM
    cb = n // l_ctx
    return pl.pallas_call(
        functools.partial(_attn_c_kernel, n=n, l_ctx=l_ctx, local=False, aliased=True),
        grid=(1,),
        in_specs=[
            smem,
            pl.BlockSpec((TM, gw), lambda i: (ct, 0)),
            pl.BlockSpec((l_ctx, kvw), lambda i: (cb, 0)),
            pl.BlockSpec((l_ctx, kvw), lambda i: (cb, 1)),
            pl.BlockSpec(memory_space=pl.ANY),
        ],
        out_specs=pl.BlockSpec((TM, gw), lambda i: (ct, 0)),
        out_shape=jax.ShapeDtypeStruct((rows, gw), BF16),
        input_output_aliases={4: 0},
        compiler_params=_params(("parallel",)),
        name="attn_c_ctx",
    )(sink, cq, ckv, ckv, yc)


def _attn_d_kernel(q_ref, kc_ref, vc_ref, *rest, local, aliased, n_rows):
    o_ref = rest[-1]
    n_pairs = D_HEADS // 2
    lane_block = lambda c: slice(c * LANES, (c + 1) * LANES)
    nk = NA_ROWS * GRID_W
    if local:
        k_ref, v_ref, b_ref = rest[0], rest[1], rest[2]
        g = pl.program_id(0)
        win0 = jnp.clip(g * D_ROWS - NA_ROWS // 2, 0, n_rows - (D_ROWS + NA_ROWS - 1))
        q_rows, offs, shifts = [], [], []
        for i in range(D_ROWS):
            r = g * D_ROWS + i
            rs = jnp.clip(r - NA_ROWS // 2, 0, n_rows - NA_ROWS)
            q_rows.append(slice(i * GRID_W, (i + 1) * GRID_W))
            offs.append(pl.multiple_of((rs - win0) * GRID_W, GRID_W))
            shifts.append(r - rs)
    else:
        q_rows = [slice(0, q_ref.shape[0])]

    def score_fn(idx):
        i, c = divmod(idx, n_pairs)
        q2 = _stack_pair(q_ref[q_rows[i], lane_block(c)])
        scores = [_dot_nt(q2, kc_ref[:, lane_block(c)])]
        if local:
            bias = jnp.concatenate([b_ref[shifts[i], 2 * c], b_ref[shifts[i], 2 * c + 1]], axis=0)
            scores.append(_dot_nt(q2, k_ref[pl.ds(offs[i], nk), lane_block(c)]) + bias)
        return scores

    values = []
    for i in range(len(q_rows)):
        for c in range(n_pairs):
            vals = [vc_ref[:, lane_block(c)]]
            if local:
                vals.append(v_ref[pl.ds(offs[i], nk), lane_block(c)])
            values.append(vals)
    outs = _paired_heads_attention(score_fn, values, len(q_rows) * n_pairs)
    for idx, out in enumerate(outs):
        i, c = divmod(idx, n_pairs)
        o_ref[q_rows[i], lane_block(c)] = out.astype(BF16)


def _attn_d(dq, dk, dv, bias, n, l_ctx, need_ctx):
    rows = dq.shape[0]
    gw = GROUP_WIDTH
    n_rows = n // GRID_W
    nk = NA_ROWS * GRID_W
    cb = n // l_ctx
    win = D_ROWS + NA_ROWS - 1
    assert n_rows % D_ROWS == 0 and n_rows >= win

    def window_start(g):
        return jnp.clip(g * D_ROWS - NA_ROWS // 2, 0, n_rows - win) * GRID_W

    once = dict(pipeline_mode=pl.Buffered(1))
    yd = pl.pallas_call(
        functools.partial(_attn_d_kernel, local=True, aliased=False, n_rows=n_rows),
        grid=(n_rows // D_ROWS,),
        in_specs=[
            pl.BlockSpec((D_ROWS * GRID_W, gw), lambda g: (g, 0)),
            pl.BlockSpec((l_ctx, gw), lambda g: (cb, 0), **once),
            pl.BlockSpec((l_ctx, gw), lambda g: (cb, 0), **once),
            pl.BlockSpec((pl.Element(win * GRID_W), pl.Element(gw)), lambda g: (window_start(g), 0)),
            pl.BlockSpec((pl.Element(win * GRID_W), pl.Element(gw)), lambda g: (window_start(g), 0)),
            pl.BlockSpec((NA_ROWS, D_HEADS, GRID_W, nk), lambda g: (0, 0, 0, 0), **once),
        ],
        out_specs=pl.BlockSpec((D_ROWS * GRID_W, gw), lambda g: (g, 0)),
        out_shape=jax.ShapeDtypeStruct((rows, gw), BF16),
        compiler_params=_params(("parallel",)),
        name="attn_d",
    )(dq, dk, dv, dk, dv, bias)
    if not need_ctx:
        return yd
    ct = n // TM
    return pl.pallas_call(
        functools.partial(_attn_d_kernel, local=False, aliased=True, n_rows=n_rows),
        grid=(1,),
        in_specs=[
            pl.BlockSpec((TM, gw), lambda i: (ct, 0)),
            pl.BlockSpec((l_ctx, gw), lambda i: (cb, 0)),
            pl.BlockSpec((l_ctx, gw), lambda i: (cb, 0)),
            pl.BlockSpec(memory_space=pl.ANY),
        ],
        out_specs=pl.BlockSpec((TM, gw), lambda i: (ct, 0)),
        out_shape=jax.ShapeDtypeStruct((rows, gw), BF16),
        input_output_aliases={3: 0},
        compiler_params=_params(("parallel",)),
        name="attn_d_ctx",
    )(dq, dk, dv, yd)


def _outproj_kernel(ya_ref, yb_ref, yc_ref, yd_ref, w_ref, x_ref, gt_ref, lng_ref, lnb_ref, o_ref):
    y_in = jnp.concatenate([ya_ref[...], yb_ref[...], yc_ref[...], yd_ref[...]], axis=1)
    y = jnp.dot(y_in, w_ref[...], preferred_element_type=F32)
    t = DEEPNORM_ALPHA * x_ref[...] + gt_ref[0] * y
    o_ref[...] = _ln(t, lng_ref[...], lnb_ref[...])


def _outproj(ya, yb, yc, yd, w_out, layer, x, mods_l, gate_k, lng, lnb, n_tiles, nlt):
    d = D_MODEL
    gw = GROUP_WIDTH
    ytile = pl.BlockSpec((TM, gw), lambda i: (i, 0))
    vec = pl.BlockSpec((1, d), lambda i: (0, 0))
    return pl.pallas_call(
        _outproj_kernel,
        grid=(n_tiles,),
        in_specs=[
            ytile, ytile, ytile, ytile,
            pl.BlockSpec((None, d, d), lambda i: (layer, 0, 0), pipeline_mode=pl.Buffered(1)),
            pl.BlockSpec((TM, d), lambda i: (i, 0)),
            pl.BlockSpec((1, 1, d), lambda i: (jnp.where(i >= nlt, N_MOD, 0) + gate_k, 0, 0)),
            vec, vec,
        ],
        out_specs=pl.BlockSpec((TM, d), lambda i: (i, 0)),
        out_shape=jax.ShapeDtypeStruct((n_tiles * TM, d), F32),
        compiler_params=_params(("parallel",)),
        name="outproj",
    )(ya, yb, yc, yd, w_out, x, mods_l, lng, lnb)


def _rope_tables(n, rows):
    t = jnp.arange(n, dtype=jnp.int32)
    row = (t // GRID_W).astype(F32)
    col = (t % GRID_W).astype(F32)
    n_freq = HEAD_DIM // 4
    inv = ROPE_THETA ** (-jnp.arange(n_freq, dtype=F32) / n_freq)
    ang = jnp.concatenate([row[:, None] * inv, col[:, None] * inv], axis=-1)
    cos, sin = jnp.cos(ang), jnp.sin(ang)
    cos_t = jnp.tile(jnp.concatenate([cos, cos], axis=-1), (1, LANES // HEAD_DIM))
    sin_t = jnp.tile(jnp.concatenate([-sin, sin], axis=-1), (1, LANES // HEAD_DIM))
    pad = rows - n
    cos_t = jnp.concatenate([cos_t, jnp.ones((pad, LANES), F32)], axis=0)
    sin_t = jnp.concatenate([sin_t, jnp.zeros((pad, LANES), F32)], axis=0)
    return cos_t, sin_t


def _na_bias(rpb):
    w = GRID_W
    cq = jnp.arange(w)
    cs = jnp.clip(cq - NA_COLS // 2, 0, w - NA_COLS)
    col_ok = (cq[None, :] >= cs[:, None]) & (cq[None, :] < cs[:, None] + NA_COLS)
    edge = w - NA_COLS
    ext = jnp.pad(rpb.astype(F32), ((0, 0), (0, 0), (edge, edge)), mode="edge")
    toep = jnp.stack([ext[:, :, w - 1 - q:2 * w - 1 - q] for q in range(w)], axis=2)
    toep = jnp.where(col_ok[None, None], toep, NEG_INF)
    per_shift = []
    for shift in range(NA_ROWS):
        sl = toep[:, NA_ROWS - 1 - shift:2 * NA_ROWS - 1 - shift]
        per_shift.append(jnp.swapaxes(sl, 1, 2).reshape(rpb.shape[0], w, NA_ROWS * w))
    return jnp.stack(per_shift, axis=0) * LOG2E


def kernel(x, c, ctx, c_ctx, w_mod, b_mod, ln_g, ln_b, ffn1_w_in, ffn1_w_out, ffn2_w_in, ffn2_w_out, mix_w_in, mix_w_out, a_lambda, b_norm_g, b_norm_b, b_spatial_w, b_spatial_b, c_sink, d_rpb):
    depth = w_mod.shape[0]
    n = x.shape[1]
    l_ctx = ctx.shape[1]
    d = D_MODEL
    assert x.shape[0] == 1 and n % TM == 0 and n // GRID_W >= NA_ROWS and n % TM_FFN == 0
    assert l_ctx % CHUNK == 0 and l_ctx <= TM and n % l_ctx == 0
    nlt = n // TM
    rows = n + TM

    cc = jnp.concatenate([c, c_ctx[None], jnp.zeros((SUBLANES - 2, d), F32)], axis=0)
    mods = _mod_vectors(cc, w_mod, b_mod)[:, :2].reshape(depth, 2 * N_MOD, 1, d)
    cos_t, sin_t = _rope_tables(n, rows)
    w1_in, w1_out, w2_in, w2_out, wm_in, wm_out = (
        w.astype(BF16) for w in (ffn1_w_in, ffn1_w_out, ffn2_w_in, ffn2_w_out, mix_w_in, mix_w_out))

    for l in range(depth):
        last = l == depth - 1
        lam_init = 0.8 - 0.6 * math.exp(-0.3 * l)
        mods_l = mods[l]
        lng = [ln_g[l, k][None] for k in range(3)]
        lnb = [ln_b[l, k][None] for k in range(3)]

        if l == 0:
            xs = _ffn(x[0], ctx[0], 0, l_ctx, mods_l, (0, 1, 2), lng[0], lnb[0], w1_in, w1_out, l, n)
        else:
            xs = _ffn(xs, xs, n, l_ctx, mods_l, (0, 1, 2), lng[0], lnb[0], w1_in, w1_out, l, n)
        aq, ak, av, bu, bv, cq, ckv, dq, dk, dv = _inproj(xs, mods_l, 3, 4, wm_in, l, cos_t, sin_t, nlt)
        ya = _attn_a(a_lambda[l], aq, ak, av, n, l_ctx, lam_init, not last)
        yb = _mixb(bu, bv, b_norm_g[l][None], b_norm_b[l][None], b_spatial_w[l], b_spatial_b[l].T)
        yc = _attn_c(c_sink[l], cq, ckv, n, l_ctx, not last)
        yd = _attn_d(dq, dk, dv, _na_bias(d_rpb[l]), n, l_ctx, not last)
        n_tiles = nlt if last else nlt + 1
        xs = _outproj(ya, yb, yc, yd, wm_out, l, xs, mods_l, 5, lng[1], lnb[1], n_tiles, nlt)
        xs = _ffn(xs, None if last else xs, n, l_ctx, mods_l, (6, 7, 8), lng[2], lnb[2], w2_in, w2_out, l, n)
    return xs[None]
```

```python
import functools
import math

import jax
import jax.numpy as jnp
from jax import lax
from jax.experimental import pallas as pl
from jax.experimental.pallas import tpu as pltpu

F32 = jnp.float32
BF16 = jnp.bfloat16

D_MODEL = 2048
N_GROUPS = 4
GROUP_WIDTH = D_MODEL // N_GROUPS
HEAD_DIM = 64
GRID_W = 64
CHUNK = 128
B_GROUPS = GROUP_WIDTH // CHUNK
A_HEADS = GROUP_WIDTH // (2 * HEAD_DIM)
C_HEADS = GROUP_WIDTH // HEAD_DIM
C_KV_HEADS = C_HEADS // 4
D_HEADS = GROUP_WIDTH // HEAD_DIM
WINDOW = 128
QBLOCK = 128
NA_ROWS = 8
NA_COLS = 16
D_FF = 256 * math.ceil(8 * D_MODEL / 3 / 256)
N_MOD = 9
ROPE_THETA = 10000.0
LN_EPS = 1e-6
NEG_INF = -1e30
MODEL_DEPTH = 4
DEEPNORM_ALPHA = (2 * MODEL_DEPTH) ** 0.25
QK_SCALE = HEAD_DIM ** -0.5
LOG2E = math.log2(math.e)

LANES = 128
SUBLANES = 8
BF16_SUBLANES = 16
A_VROWS = 2 * HEAD_DIM + BF16_SUBLANES
A_UNROLL = 16
D_ROWS = 8
TN_MOD = 2048
TM = 512
TM_FFN = 1024
TF = 512
SEG = 512
N_SEG = 10
VMEM_LIMIT = 60 * 1024 * 1024


def _params(sem):
    return pltpu.CompilerParams(dimension_semantics=sem, vmem_limit_bytes=VMEM_LIMIT)


def _ln(t, g, b):
    mu = jnp.mean(t, axis=-1, keepdims=True)
    tc = t - mu
    var = jnp.mean(tc * tc, axis=-1, keepdims=True)
    return tc * lax.rsqrt(var + LN_EPS) * g + b


def _dot_nt(a, b):
    return lax.dot_general(a, b, (((1,), (1,)), ((), ())), preferred_element_type=F32)


def _half_masks(shape):
    lane = lax.broadcasted_iota(jnp.int32, shape, 1)
    return lane < HEAD_DIM, lane >= HEAD_DIM


def _mod_kernel(cc_ref, w_ref, b_ref, o_ref):
    a = cc_ref[...]
    s = (a * jax.nn.sigmoid(a)).astype(BF16)
    o_ref[0] = jnp.dot(s, w_ref[0].astype(BF16), preferred_element_type=F32) + b_ref[0]


def _mod_vectors(cc, w_mod, b_mod):
    depth, d, nm = w_mod.shape
    tn = TN_MOD
    return pl.pallas_call(
        _mod_kernel,
        grid=(depth, nm // tn),
        in_specs=[
            pl.BlockSpec((SUBLANES, d), lambda l, j: (0, 0)),
            pl.BlockSpec((1, d, tn), lambda l, j: (l, 0, j)),
            pl.BlockSpec((1, 1, tn), lambda l, j: (l, 0, j)),
        ],
        out_specs=pl.BlockSpec((1, SUBLANES, tn), lambda l, j: (l, 0, j)),
        out_shape=jax.ShapeDtypeStruct((depth, SUBLANES, nm), F32),
        compiler_params=_params(("parallel", "parallel")),
        name="mod_vectors",
    )(cc, w_mod, b_mod.reshape(depth, 1, nm))


def _ffn_kernel(x_ref, sh_ref, sc_ref, gt_ref, lng_ref, lnb_ref, wg_ref, wu_ref, wo_ref, o_ref):
    j = pl.program_id(1)
    last_j = pl.num_programs(1) - 1
    half = x_ref.shape[0] // 2
    halves = [slice(r * half, (r + 1) * half) for r in range(2)]

    @pl.when(j == 0)
    def _():
        o_ref[...] = jnp.zeros_like(o_ref)

    def accumulate(rows):
        xin = (x_ref[rows] * (1.0 + sc_ref[0]) + sh_ref[0]).astype(BF16)
        g = jnp.dot(xin, wg_ref[...], preferred_element_type=F32)
        u = jnp.dot(xin, wu_ref[...], preferred_element_type=F32)
        hh = ((g * jax.nn.sigmoid(g)) * u).astype(BF16)
        return o_ref[rows] + jnp.dot(hh, wo_ref[...], preferred_element_type=F32)

    @pl.when(j < last_j)
    def _():
        for rows in halves:
            o_ref[rows] = accumulate(rows)

    @pl.when(j == last_j)
    def _():
        acc = [accumulate(rows) for rows in halves]
        for rows, a in zip(halves, acc):
            t = DEEPNORM_ALPHA * x_ref[rows] + (0.5 * gt_ref[0]) * a
            o_ref[rows] = _ln(t, lng_ref[...], lnb_ref[...])


def _ffn_kernel_aliased(*refs):
    _ffn_kernel(*refs[:-2], refs[-1])


def _ffn(x, x_ctx, ctx_row0, l_ctx, mods_l, ks, lng, lnb, w_in, w_out, layer, n):
    d = D_MODEL
    nj = D_FF // TF
    assert n % TM_FFN == 0 and n % TM == 0
    with_ctx = x_ctx is not None
    assert not with_ctx or (TM % l_ctx == 0 and n % l_ctx == 0 and ctx_row0 % l_ctx == 0)
    out_rows = n + TM if with_ctx else n

    def call(src, tm, tile0, tm_out, out_tile0, n_tiles, mod0, alias):
        mod_spec = lambda k: pl.BlockSpec((1, 1, d), lambda i, j: (mod0 + k, 0, 0))
        vec_spec = pl.BlockSpec((1, d), lambda i, j: (0, 0))
        in_specs = [
            pl.BlockSpec((tm, d), lambda i, j: (tile0 + i, 0)),
            mod_spec(ks[0]), mod_spec(ks[1]), mod_spec(ks[2]),
            vec_spec, vec_spec,
            pl.BlockSpec((None, d, TF), lambda i, j: (layer, 0, j)),
            pl.BlockSpec((None, d, TF), lambda i, j: (layer, 0, nj + j)),
            pl.BlockSpec((None, TF, d), lambda i, j: (layer, j, 0)),
        ]
        args = [src, mods_l, mods_l, mods_l, lng, lnb, w_in, w_in, w_out]
        kernel_fn = _ffn_kernel
        aliases = {}
        if alias is not None:
            in_specs.append(pl.BlockSpec(memory_space=pl.ANY))
            args.append(alias)
            aliases = {len(args) - 1: 0}
            kernel_fn = _ffn_kernel_aliased
        return pl.pallas_call(
            kernel_fn,
            grid=(n_tiles, nj),
            in_specs=in_specs,
            out_specs=pl.BlockSpec((tm_out, d), lambda i, j: (out_tile0 + i, 0)),
            out_shape=jax.ShapeDtypeStruct((out_rows, d), F32),
            input_output_aliases=aliases,
            compiler_params=_params(("parallel", "arbitrary")),
            name="ffn" if alias is None else "ffn_ctx",
        )(*args)

    y = call(x, TM_FFN, 0, TM_FFN, 0, n // TM_FFN, 0, None)
    if with_ctx:
        y = call(x_ctx, l_ctx, ctx_row0 // l_ctx, TM, n // TM, 1, N_MOD, y)
    return y


def _rope(t, cos_ref, sin_ref):
    w = t.shape[1]
    lane = lax.broadcasted_iota(jnp.int32, t.shape, 1)
    first = (lane % HEAD_DIM) < (HEAD_DIM // 2)
    rot = jnp.where(first, pltpu.roll(t, w - HEAD_DIM // 2, 1), pltpu.roll(t, HEAD_DIM // 2, 1))
    reps = w // LANES
    cos = jnp.concatenate([cos_ref[...]] * reps, axis=1) if reps > 1 else cos_ref[...]
    sin = jnp.concatenate([sin_ref[...]] * reps, axis=1) if reps > 1 else sin_ref[...]
    return t * cos + rot * sin


_SEG_AQ, _SEG_AK, _SEG_AV, _SEG_BU, _SEG_BV, _SEG_CQ, _SEG_CKV, _SEG_DQ, _SEG_DK, _SEG_DV = range(N_SEG)
_SEG_WIDTHS = tuple(2 * C_KV_HEADS * HEAD_DIM if k == _SEG_CKV else GROUP_WIDTH for k in range(N_SEG))
_SEG_OFFSETS = tuple(sum(_SEG_WIDTHS[:k]) for k in range(N_SEG))
D_PROJ = sum(_SEG_WIDTHS)


def _dup_halves(x):
    swapped = pltpu.roll(x, HEAD_DIM, 1)
    first, _ = _half_masks(x.shape)
    return jnp.where(first, x, swapped), jnp.where(first, swapped, x)


def _inproj_kernel(x_ref, sh_ref, sc_ref, w_ref, cos_ref, sin_ref, *out_refs):
    h = (x_ref[...] * (1.0 + sc_ref[0]) + sh_ref[0]).astype(BF16)

    def project(k):
        return jnp.dot(h, w_ref[:, _SEG_OFFSETS[k]:_SEG_OFFSETS[k] + _SEG_WIDTHS[k]], preferred_element_type=F32)

    def finish(k, t):
        o_ref = out_refs[k]
        if k in (_SEG_AQ, _SEG_AK, _SEG_CQ):
            t = _rope(t, cos_ref, sin_ref)
        if k in (_SEG_AQ, _SEG_CQ, _SEG_DQ):
            t = t * (QK_SCALE * LOG2E)
        if k == _SEG_AQ:
            o_ref[0] = t.T.astype(o_ref.dtype)
        elif k == _SEG_AV:
            tt = t.T.astype(o_ref.dtype)
            extra = A_VROWS - 2 * HEAD_DIM
            ones_row = (lax.broadcasted_iota(jnp.int32, (extra, tt.shape[1]), 0) == 0).astype(o_ref.dtype)
            for hd in range(A_HEADS):
                o_ref[0, hd * A_VROWS:hd * A_VROWS + 2 * HEAD_DIM] = tt[hd * 2 * HEAD_DIM:(hd + 1) * 2 * HEAD_DIM]
                o_ref[0, hd * A_VROWS + 2 * HEAD_DIM:(hd + 1) * A_VROWS] = ones_row
        elif k == _SEG_CKV:
            kk = _dup_halves(_rope(t[:, :LANES], cos_ref, sin_ref))
            vv = _dup_halves(t[:, LANES:])
            o_ref[...] = jnp.concatenate([kk[0], kk[1], vv[0], vv[1]], axis=1).astype(o_ref.dtype)
        else:
            o_ref[...] = t.astype(o_ref.dtype)

    t = project(0)
    for k in range(N_SEG):
        t_next = project(k + 1) if k + 1 < N_SEG else None
        finish(k, t)
        t = t_next


def _inproj(x, mods_l, shift_k, scale_k, w_in, layer, cos_t, sin_t, nlt):
    rows = x.shape[0]

    def mod_spec(k):
        return pl.BlockSpec((1, 1, D_MODEL), lambda i: (jnp.where(i >= nlt, N_MOD, 0) + k, 0, 0))

    n_tiles = rows // TM
    d = D_MODEL
    assert C_KV_HEADS == 2 and w_in.shape[1:] == (d, D_PROJ)
    widths = [SEG] * N_SEG
    dtypes = [BF16] * N_SEG
    dtypes[_SEG_BU] = F32
    dtypes[_SEG_BV] = F32
    out_specs = [pl.BlockSpec((TM, w), lambda i: (i, 0)) for w in widths]
    out_shape = [jax.ShapeDtypeStruct((rows, w), dt) for w, dt in zip(widths, dtypes)]
    for k in (_SEG_AQ, _SEG_AV):
        slab = SEG if k == _SEG_AQ else A_HEADS * A_VROWS
        out_specs[k] = pl.BlockSpec((1, slab, TM), lambda i: (i, 0, 0))
        out_shape[k] = jax.ShapeDtypeStruct((n_tiles, slab, TM), BF16)
    return pl.pallas_call(
        _inproj_kernel,
        grid=(n_tiles,),
        in_specs=[
            pl.BlockSpec((TM, d), lambda i: (i, 0)),
            mod_spec(shift_k), mod_spec(scale_k),
            pl.BlockSpec((None, d, D_PROJ), lambda i: (layer, 0, 0), pipeline_mode=pl.Buffered(1)),
            pl.BlockSpec((TM, LANES), lambda i: (i, 0)),
            pl.BlockSpec((TM, LANES), lambda i: (i, 0)),
        ],
        out_specs=out_specs,
        out_shape=out_shape,
        compiler_params=_params(("parallel",)),
        name="inproj",
    )(x, mods_l, mods_l, w_in, cos_t, sin_t)


def _attn_a_kernel(lam_ref, qt_ref, k_ref, vt_ref, *rest, n_chunks, tail_k_start, tail_chunk, tail_len, lam_init,
                   aliased):
    if aliased:
        rest = rest[1:]
    o_ref, m_scr, acc_scr, s_scr, mc_scr = rest
    qt = qt_ref[0]
    comp = lax.broadcasted_iota(jnp.int32, qt.shape, 0) < HEAD_DIM
    zero = jnp.zeros_like(qt)
    qts = (jnp.where(comp, qt, zero), jnp.where(comp, zero, qt))
    m_scr[...] = jnp.full(m_scr.shape, NEG_INF, F32)
    acc_scr[...] = jnp.zeros_like(acc_scr)

    def scores(k, slot, size):
        for c in range(2):
            s = jnp.dot(k, qts[c], preferred_element_type=F32)
            s_scr[slot, c, :size] = s
            mc_scr[slot, c] = jnp.max(s, axis=0, keepdims=True)

    def consume(slot, vt, size):
        for c in range(2):
            m_old = m_scr[c]
            m_new = jnp.maximum(m_old, mc_scr[slot, c])
            alpha = jnp.exp2(m_old - m_new)
            p = jnp.exp2(s_scr[slot, c, :size] - m_new).astype(BF16)
            acc_scr[c] = alpha * acc_scr[c] + jnp.dot(vt, p, preferred_element_type=F32)
            m_scr[c] = m_new

    def k_chunk(i):
        i = jnp.minimum(i, n_chunks - 1)
        return k_ref[pl.ds(pl.multiple_of(i * TM, TM), TM), :]

    scores(k_ref[pl.ds(tail_k_start, tail_len), :], 1, tail_len)
    if n_chunks:
        unroll = math.gcd(n_chunks, A_UNROLL)
        assert unroll % 2 == 0
        scores(k_chunk(0), 0, TM)
    consume(1, vt_ref[tail_chunk][:, :tail_len], tail_len)
    if n_chunks:
        def body(t, carry):
            j = unroll * t
            for u in range(unroll):
                scores(k_chunk(j + u + 1), (u + 1) % 2, TM)
                consume(u % 2, vt_ref[j + u], TM)
            return carry
        lax.fori_loop(0, n_chunks // unroll, body, 0)

    lv = lam_ref[...]
    lam = (jnp.exp(jnp.sum(lv[0:1] * lv[1:2], axis=1, keepdims=True))
           - jnp.exp(jnp.sum(lv[2:3] * lv[3:4], axis=1, keepdims=True)) + lam_init)
    nv = 2 * HEAD_DIM
    ot = (acc_scr[0, :nv] / acc_scr[0, nv:nv + 1]
          - lam * (acc_scr[1, :nv] / acc_scr[1, nv:nv + 1]))
    ms = jnp.mean(ot * ot, axis=0, keepdims=True)
    ot = (ot * lax.rsqrt(ms + LN_EPS)) * (1.0 - lam_init)
    o_ref[...] = ot.T.astype(BF16)


def _attn_a(lam_vecs, aqt, ak, avt, n, l_ctx, lam_init, need_ctx):
    rows = ak.shape[0]
    nt = rows // TM
    nlt = n // TM
    scratch = [pltpu.VMEM((2, 1, TM), F32), pltpu.VMEM((2, A_VROWS, TM), F32),
               pltpu.VMEM((2, 2, TM, TM), F32), pltpu.VMEM((2, 2, 1, TM), F32)]
    lam_spec = pl.BlockSpec((4, HEAD_DIM), lambda h, i: (0, 0))
    ya = pl.pallas_call(
        functools.partial(_attn_a_kernel, n_chunks=nlt, tail_k_start=n, tail_chunk=nlt, tail_len=l_ctx,
                          lam_init=lam_init, aliased=False),
        grid=(A_HEADS, nlt),
        in_specs=[
            lam_spec,
            pl.BlockSpec((1, LANES, TM), lambda h, i: (i, h, 0)),
            pl.BlockSpec((rows, LANES), lambda h, i: (0, h)),
            pl.BlockSpec((nt, A_VROWS, TM), lambda h, i: (0, h, 0)),
        ],
        out_specs=pl.BlockSpec((TM, LANES), lambda h, i: (i, h)),
        out_shape=jax.ShapeDtypeStruct((rows, GROUP_WIDTH), BF16),
        scratch_shapes=scratch,
        compiler_params=_params(("parallel", "parallel")),
        name="attn_a",
    )(lam_vecs, aqt, ak, avt)
    if not need_ctx:
        return ya
    cb = n // l_ctx
    return pl.pallas_call(
        functools.partial(_attn_a_kernel, n_chunks=0, tail_k_start=0, tail_chunk=0, tail_len=l_ctx,
                          lam_init=lam_init, aliased=True),
        grid=(A_HEADS, 1),
        in_specs=[
            lam_spec,
            pl.BlockSpec((1, LANES, TM), lambda h, i: (nlt, h, 0)),
            pl.BlockSpec((l_ctx, LANES), lambda h, i: (cb, h)),
            pl.BlockSpec((1, A_VROWS, TM), lambda h, i: (nlt, h, 0)),
            pl.BlockSpec(memory_space=pl.ANY),
        ],
        out_specs=pl.BlockSpec((TM, LANES), lambda h, i: (nlt, h)),
        out_shape=jax.ShapeDtypeStruct((rows, GROUP_WIDTH), BF16),
        scratch_shapes=scratch,
        input_output_aliases={4: 0},
        compiler_params=_params(("parallel", "parallel")),
        name="attn_a_ctx",
    )(lam_vecs, aqt, ak, avt, ya)


def _mixb_kernel(u_ref, v_ref, g_ref, b_ref, ws_ref, bs_ref, o_ref):
    u = jax.nn.gelu(u_ref[...])
    v = _ln(jax.nn.gelu(v_ref[...]), g_ref[...], b_ref[...]).astype(BF16)
    for c in range(u.shape[0] // CHUNK):
        rs = slice(c * CHUNK, (c + 1) * CHUNK)
        for g in range(B_GROUPS):
            cs = slice(g * LANES, (g + 1) * LANES)
            mixed = jnp.dot(ws_ref[g].astype(BF16), v[rs, cs], preferred_element_type=F32) + bs_ref[:, g:g + 1]
            o_ref[rs, cs] = (u[rs, cs] * mixed).astype(BF16)


def _mixb(bu, bv, gn_g, gn_b, w_s, b_s_t):
    rows = bu.shape[0]
    gw = GROUP_WIDTH
    tile = pl.BlockSpec((TM, gw), lambda i: (i, 0))
    return pl.pallas_call(
        _mixb_kernel,
        grid=(rows // TM,),
        in_specs=[
            tile, tile,
            pl.BlockSpec((1, gw), lambda i: (0, 0)),
            pl.BlockSpec((1, gw), lambda i: (0, 0)),
            pl.BlockSpec((B_GROUPS, CHUNK, CHUNK), lambda i: (0, 0, 0)),
            pl.BlockSpec((CHUNK, B_GROUPS), lambda i: (0, 0)),
        ],
        out_specs=tile,
        out_shape=jax.ShapeDtypeStruct((rows, gw), BF16),
        compiler_params=_params(("parallel",)),
        name="mix_b",
    )(bu, bv, gn_g, gn_b, w_s, b_s_t)


def _stack_pair(qb):
    first, second = _half_masks(qb.shape)
    zero = jnp.zeros_like(qb)
    return jnp.concatenate([jnp.where(first, qb, zero), jnp.where(second, qb, zero)], axis=0)


def _paired_heads_attention(score_fn, values, n_pairs, sinks=None):
    all_scores = [score_fn(c) for c in range(n_pairs)]
    all_probs = []
    for c, scores in enumerate(all_scores):
        m = jnp.max(scores[0], axis=1, keepdims=True)
        for s in scores[1:]:
            m = jnp.maximum(m, jnp.max(s, axis=1, keepdims=True))
        if sinks is not None:
            m = jnp.maximum(m, sinks[c])
        ps = [jnp.exp2(s - m) for s in scores]
        den = jnp.sum(ps[0], axis=1, keepdims=True)
        for p in ps[1:]:
            den = den + jnp.sum(p, axis=1, keepdims=True)
        if sinks is not None:
            den = den + jnp.exp2(sinks[c] - m)
        all_probs.append(([p.astype(BF16) for p in ps], den))
    outs = []
    for c in range(n_pairs):
        ps, den = all_probs[c]
        num = None
        for p, v in zip(ps, values[c]):
            pv = jnp.dot(p, v, preferred_element_type=F32)
            num = pv if num is None else num + pv
        o = num / den
        r = o.shape[0] // 2
        first, _ = _half_masks((r, LANES))
        outs.append(jnp.where(first, o[:r], o[r:]))
    return outs


def _attn_c_kernel(sink_ref, q_ref, k_ref, v_ref, *rest, n, l_ctx, local, aliased):
    o_ref = rest[-1]
    i = pl.program_id(0)
    n_sub = q_ref.shape[0] // QBLOCK
    wlen = 3 * QBLOCK
    ctx_start = n if local else 0
    n_pairs = C_HEADS // 2
    grp_pairs = n_pairs // C_KV_HEADS
    kv_block = lambda c: slice((c // grp_pairs) * LANES, (c // grp_pairs + 1) * LANES)
    top = lax.broadcasted_iota(jnp.int32, (2 * QBLOCK, 1), 0) < QBLOCK
    pair_sinks = [jnp.where(top, sink_ref[2 * c] * LOG2E, sink_ref[2 * c + 1] * LOG2E) for c in range(n_pairs)]
    sinks = [pair_sinks[c] for _ in range(n_sub) for c in range(n_pairs)]
    q_rows = [slice(b * QBLOCK, (b + 1) * QBLOCK) for b in range(n_sub)]
    if local:
        starts, valids = [], []
        for b in range(n_sub):
            blk = i * n_sub + b
            start = pl.multiple_of(jnp.clip((blk - 1) * QBLOCK, 0, n - wlen), QBLOCK)
            qpos = blk * QBLOCK + lax.broadcasted_iota(jnp.int32, (QBLOCK, wlen), 0)
            kpos = start + lax.broadcasted_iota(jnp.int32, (QBLOCK, wlen), 1)
            starts.append(start)
            valid = jnp.abs(kpos - qpos) <= WINDOW
            valids.append(jnp.concatenate([valid, valid], axis=0))

    def score_fn(idx):
        b, c = divmod(idx, n_pairs)
        q2 = _stack_pair(q_ref[q_rows[b], c * LANES:(c + 1) * LANES])
        scores = [_dot_nt(q2, k_ref[pl.ds(ctx_start, l_ctx), kv_block(c)])]
        if local:
            scores.append(jnp.where(valids[b], _dot_nt(q2, k_ref[pl.ds(starts[b], wlen), kv_block(c)]), NEG_INF))
        return scores

    values = []
    for b in range(n_sub):
        for c in range(n_pairs):
            vals = [v_ref[pl.ds(ctx_start, l_ctx), kv_block(c)]]
            if local:
                vals.append(v_ref[pl.ds(starts[b], wlen), kv_block(c)])
            values.append(vals)
    outs = _paired_heads_attention(score_fn, values, n_sub * n_pairs, sinks)
    for idx, out in enumerate(outs):
        b, c = divmod(idx, n_pairs)
        o_ref[q_rows[b], c * LANES:(c + 1) * LANES] = out.astype(BF16)


def _attn_c(sink, cq, ckv, n, l_ctx, need_ctx):
    rows = cq.shape[0]
    gw = GROUP_WIDTH
    kvw = C_KV_HEADS * LANES
    smem = pl.BlockSpec(memory_space=pltpu.SMEM)
    yc = pl.pallas_call(
        functools.partial(_attn_c_kernel, n=n, l_ctx=l_ctx, local=True, aliased=False),
        grid=(n // TM,),
        in_specs=[
            smem,
            pl.BlockSpec((TM, gw), lambda i: (i, 0)),
            pl.BlockSpec((rows, kvw), lambda i: (0, 0)),
            pl.BlockSpec((rows, kvw), lambda i: (0, 1)),
        ],
        out_specs=pl.BlockSpec((TM, gw), lambda i: (i, 0)),
        out_shape=jax.ShapeDtypeStruct((rows, gw), BF16),
        compiler_params=_params(("parallel",)),
        name="attn_c",
    )(sink, cq, ckv, ckv)
    if not need_ctx:
        return yc
    ct = n // TM
    cb = n // l_ctx
    return pl.pallas_call(
        functools.partial(_attn_c_kernel, n=n, l_ctx=l_ctx, local=False, aliased=True),
        grid=(1,),
        in_specs=[
            smem,
            pl.BlockSpec((TM, gw), lambda i: (ct, 0)),
            pl.BlockSpec((l_ctx, kvw), lambda i: (cb, 0)),
            pl.BlockSpec((l_ctx, kvw), lambda i: (cb, 1)),
            pl.BlockSpec(memory_space=pl.ANY),
        ],
        out_specs=pl.BlockSpec((TM, gw), lambda i: (ct, 0)),
        out_shape=jax.ShapeDtypeStruct((rows, gw), BF16),
        input_output_aliases={4: 0},
        compiler_params=_params(("parallel",)),
        name="attn_c_ctx",
    )(sink, cq, ckv, ckv, yc)


def _attn_d_kernel(q_ref, kc_ref, vc_ref, *rest, local, aliased, n_rows):
    o_ref = rest[-1]
    n_pairs = D_HEADS // 2
    lane_block = lambda c: slice(c * LANES, (c + 1) * LANES)
    nk = NA_ROWS * GRID_W
    if local:
        k_ref, v_ref, b_ref = rest[0], rest[1], rest[2]
        g = pl.program_id(0)
        win0 = jnp.clip(g * D_ROWS - NA_ROWS // 2, 0, n_rows - (D_ROWS + NA_ROWS - 1))
        q_rows, offs, shifts = [], [], []
        for i in range(D_ROWS):
            r = g * D_ROWS + i
            rs = jnp.clip(r - NA_ROWS // 2, 0, n_rows - NA_ROWS)
            q_rows.append(slice(i * GRID_W, (i + 1) * GRID_W))
            offs.append(pl.multiple_of((rs - win0) * GRID_W, GRID_W))
            shifts.append(r - rs)
    else:
        q_rows = [slice(0, q_ref.shape[0])]

    def score_fn(idx):
        i, c = divmod(idx, n_pairs)
        q2 = _stack_pair(q_ref[q_rows[i], lane_block(c)])
        scores = [_dot_nt(q2, kc_ref[:, lane_block(c)])]
        if local:
            bias = jnp.concatenate([b_ref[shifts[i], 2 * c], b_ref[shifts[i], 2 * c + 1]], axis=0)
            scores.append(_dot_nt(q2, k_ref[pl.ds(offs[i], nk), lane_block(c)]) + bias)
        return scores

    values = []
    for i in range(len(q_rows)):
        for c in range(n_pairs):
            vals = [vc_ref[:, lane_block(c)]]
            if local:
                vals.append(v_ref[pl.ds(offs[i], nk), lane_block(c)])
            values.append(vals)
    outs = _paired_heads_attention(score_fn, values, len(q_rows) * n_pairs)
    for idx, out in enumerate(outs):
        i, c = divmod(idx, n_pairs)
        o_ref[q_rows[i], lane_block(c)] = out.astype(BF16)


def _attn_d(dq, dk, dv, bias, n, l_ctx, need_ctx):
    rows = dq.shape[0]
    gw = GROUP_WIDTH
    n_rows = n // GRID_W
    nk = NA_ROWS * GRID_W
    cb = n // l_ctx
    win = D_ROWS + NA_ROWS - 1
    assert n_rows % D_ROWS == 0 and n_rows >= win

    def window_start(g):
        return jnp.clip(g * D_ROWS - NA_ROWS // 2, 0, n_rows - win) * GRID_W

    once = dict(pipeline_mode=pl.Buffered(1))
    yd = pl.pallas_call(
        functools.partial(_attn_d_kernel, local=True, aliased=False, n_rows=n_rows),
        grid=(n_rows // D_ROWS,),
        in_specs=[
            pl.BlockSpec((D_ROWS * GRID_W, gw), lambda g: (g, 0)),
            pl.BlockSpec((l_ctx, gw), lambda g: (cb, 0), **once),
            pl.BlockSpec((l_ctx, gw), lambda g: (cb, 0), **once),
            pl.BlockSpec((pl.Element(win * GRID_W), pl.Element(gw)), lambda g: (window_start(g), 0)),
            pl.BlockSpec((pl.Element(win * GRID_W), pl.Element(gw)), lambda g: (window_start(g), 0)),
            pl.BlockSpec((NA_ROWS, D_HEADS, GRID_W, nk), lambda g: (0, 0, 0, 0), **once),
        ],
        out_specs=pl.BlockSpec((D_ROWS * GRID_W, gw), lambda g: (g, 0)),
        out_shape=jax.ShapeDtypeStruct((rows, gw), BF16),
        compiler_params=_params(("parallel",)),
        name="attn_d",
    )(dq, dk, dv, dk, dv, bias)
    if not need_ctx:
        return yd
    ct = n // TM
    return pl.pallas_call(
        functools.partial(_attn_d_kernel, local=False, aliased=True, n_rows=n_rows),
        grid=(1,),
        in_specs=[
            pl.BlockSpec((TM, gw), lambda i: (ct, 0)),
            pl.BlockSpec((l_ctx, gw), lambda i: (cb, 0)),
            pl.BlockSpec((l_ctx, gw), lambda i: (cb, 0)),
            pl.BlockSpec(memory_space=pl.ANY),
        ],
        out_specs=pl.BlockSpec((TM, gw), lambda i: (ct, 0)),
        out_shape=jax.ShapeDtypeStruct((rows, gw), BF16),
        input_output_aliases={3: 0},
        compiler_params=_params(("parallel",)),
        name="attn_d_ctx",
    )(dq, dk, dv, yd)


def _outproj_kernel(ya_ref, yb_ref, yc_ref, yd_ref, w_ref, x_ref, gt_ref, lng_ref, lnb_ref, o_ref):
    y_in = jnp.concatenate([ya_ref[...], yb_ref[...], yc_ref[...], yd_ref[...]], axis=1)
    y = jnp.dot(y_in, w_ref[...], preferred_element_type=F32)
    t = DEEPNORM_ALPHA * x_ref[...] + gt_ref[0] * y
    o_ref[...] = _ln(t, lng_ref[...], lnb_ref[...])


def _outproj(ya, yb, yc, yd, w_out, layer, x, mods_l, gate_k, lng, lnb, n_tiles, nlt):
    d = D_MODEL
    gw = GROUP_WIDTH
    ytile = pl.BlockSpec((TM, gw), lambda i: (i, 0))
    vec = pl.BlockSpec((1, d), lambda i: (0, 0))
    return pl.pallas_call(
        _outproj_kernel,
        grid=(n_tiles,),
        in_specs=[
            ytile, ytile, ytile, ytile,
            pl.BlockSpec((None, d, d), lambda i: (layer, 0, 0), pipeline_mode=pl.Buffered(1)),
            pl.BlockSpec((TM, d), lambda i: (i, 0)),
            pl.BlockSpec((1, 1, d), lambda i: (jnp.where(i >= nlt, N_MOD, 0) + gate_k, 0, 0)),
            vec, vec,
        ],
        out_specs=pl.BlockSpec((TM, d), lambda i: (i, 0)),
        out_shape=jax.ShapeDtypeStruct((n_tiles * TM, d), F32),
        compiler_params=_params(("parallel",)),
        name="outproj",
    )(ya, yb, yc, yd, w_out, x, mods_l, lng, lnb)


def _rope_tables(n, rows):
    t = jnp.arange(n, dtype=jnp.int32)
    row = (t // GRID_W).astype(F32)
    col = (t % GRID_W).astype(F32)
    n_freq = HEAD_DIM // 4
    inv = ROPE_THETA ** (-jnp.arange(n_freq, dtype=F32) / n_freq)
    ang = jnp.concatenate([row[:, None] * inv, col[:, None] * inv], axis=-1)
    cos, sin = jnp.cos(ang), jnp.sin(ang)
    cos_t = jnp.tile(jnp.concatenate([cos, cos], axis=-1), (1, LANES // HEAD_DIM))
    sin_t = jnp.tile(jnp.concatenate([-sin, sin], axis=-1), (1, LANES // HEAD_DIM))
    pad = rows - n
    cos_t = jnp.concatenate([cos_t, jnp.ones((pad, LANES), F32)], axis=0)
    sin_t = jnp.concatenate([sin_t, jnp.zeros((pad, LANES), F32)], axis=0)
    return cos_t, sin_t


def _na_bias(rpb):
    w = GRID_W
    cq = jnp.arange(w)
    cs = jnp.clip(cq - NA_COLS // 2, 0, w - NA_COLS)
    col_ok = (cq[None, :] >= cs[:, None]) & (cq[None, :] < cs[:, None] + NA_COLS)
    edge = w - NA_COLS
    ext = jnp.pad(rpb.astype(F32), ((0, 0), (0, 0), (edge, edge)), mode="edge")
    toep = jnp.stack([ext[:, :, w - 1 - q:2 * w - 1 - q] for q in range(w)], axis=2)
    toep = jnp.where(col_ok[None, None], toep, NEG_INF)
    per_shift = []
    for shift in range(NA_ROWS):
        sl = toep[:, NA_ROWS - 1 - shift:2 * NA_ROWS - 1 - shift]
        per_shift.append(jnp.swapaxes(sl, 1, 2).reshape(rpb.shape[0], w, NA_ROWS * w))
    return jnp.stack(per_shift, axis=0) * LOG2E


def kernel(x, c, ctx, c_ctx, w_mod, b_mod, ln_g, ln_b, ffn1_w_in, ffn1_w_out, ffn2_w_in, ffn2_w_out, mix_w_in, mix_w_out, a_lambda, b_norm_g, b_norm_b, b_spatial_w, b_spatial_b, c_sink, d_rpb):
    depth = w_mod.shape[0]
    n = x.shape[1]
    l_ctx = ctx.shape[1]
    d = D_MODEL
    assert x.shape[0] == 1 and n % TM == 0 and n // GRID_W >= NA_ROWS and n % TM_FFN == 0
    assert l_ctx % CHUNK == 0 and l_ctx <= TM and n % l_ctx == 0
    nlt = n // TM
    rows = n + TM

    cc = jnp.concatenate([c, c_ctx[None], jnp.zeros((SUBLANES - 2, d), F32)], axis=0)
    mods = _mod_vectors(cc, w_mod, b_mod)[:, :2].reshape(depth, 2 * N_MOD, 1, d)
    cos_t, sin_t = _rope_tables(n, rows)
    w1_in, w1_out, w2_in, w2_out, wm_in, wm_out = (
        w.astype(BF16) for w in (ffn1_w_in, ffn1_w_out, ffn2_w_in, ffn2_w_out, mix_w_in, mix_w_out))

    for l in range(depth):
        last = l == depth - 1
        lam_init = 0.8 - 0.6 * math.exp(-0.3 * l)
        mods_l = mods[l]
        lng = [ln_g[l, k][None] for k in range(3)]
        lnb = [ln_b[l, k][None] for k in range(3)]

        if l == 0:
            xs = _ffn(x[0], ctx[0], 0, l_ctx, mods_l, (0, 1, 2), lng[0], lnb[0], w1_in, w1_out, l, n)
        else:
            xs = _ffn(xs, xs, n, l_ctx, mods_l, (0, 1, 2), lng[0], lnb[0], w1_in, w1_out, l, n)
        aq, ak, av, bu, bv, cq, ckv, dq, dk, dv = _inproj(xs, mods_l, 3, 4, wm_in, l, cos_t, sin_t, nlt)
        ya = _attn_a(a_lambda[l], aq, ak, av, n, l_ctx, lam_init, not last)
        yb = _mixb(bu, bv, b_norm_g[l][None], b_norm_b[l][None], b_spatial_w[l], b_spatial_b[l].T)
        yc = _attn_c(c_sink[l], cq, ckv, n, l_ctx, not last)
        yd = _attn_d(dq, dk, dv, _na_bias(d_rpb[l]), n, l_ctx, not last)
        n_tiles = nlt if last else nlt + 1
        xs = _outproj(ya, yb, yc, yd, wm_out, l, xs, mods_l, 5, lng[1], lnb[1], n_tiles, nlt)
        xs = _ffn(xs, None if last else xs, n, l_ctx, mods_l, (6, 7, 8), lng[2], lnb[2], w2_in, w2_out, l, n)
    return xs[None]
```

```python
import functools
import math

import jax
import jax.numpy as jnp
from jax import lax
from jax.experimental import pallas as pl
from jax.experimental.pallas import tpu as pltpu

F32 = jnp.float32
BF16 = jnp.bfloat16

D_MODEL = 2048
N_GROUPS = 4
GROUP_WIDTH = D_MODEL // N_GROUPS
HEAD_DIM = 64
GRID_W = 64
CHUNK = 128
B_GROUPS = GROUP_WIDTH // CHUNK
A_HEADS = GROUP_WIDTH // (2 * HEAD_DIM)
C_HEADS = GROUP_WIDTH // HEAD_DIM
C_KV_HEADS = C_HEADS // 4
D_HEADS = GROUP_WIDTH // HEAD_DIM
WINDOW = 128
QBLOCK = 128
NA_ROWS = 8
NA_COLS = 16
D_FF = 256 * math.ceil(8 * D_MODEL / 3 / 256)
N_MOD = 9
ROPE_THETA = 10000.0
LN_EPS = 1e-6
NEG_INF = -1e30
MODEL_DEPTH = 4
DEEPNORM_ALPHA = (2 * MODEL_DEPTH) ** 0.25
QK_SCALE = HEAD_DIM ** -0.5
LOG2E = math.log2(math.e)

LANES = 128
SUBLANES = 8
BF16_SUBLANES = 16
A_VROWS = 2 * HEAD_DIM + BF16_SUBLANES
A_UNROLL = 16
D_ROWS = 8
TN_MOD = 2048
TM = 512
TM_FFN = 1024
TF = 512
SEG = 512
N_SEG = 10
VMEM_LIMIT = 60 * 1024 * 1024


def _params(sem):
    return pltpu.CompilerParams(dimension_semantics=sem, vmem_limit_bytes=VMEM_LIMIT)


def _ln(t, g, b):
    mu = jnp.mean(t, axis=-1, keepdims=True)
    tc = t - mu
    var = jnp.mean(tc * tc, axis=-1, keepdims=True)
    return tc * lax.rsqrt(var + LN_EPS) * g + b


def _dot_nt(a, b):
    return lax.dot_general(a, b, (((1,), (1,)), ((), ())), preferred_element_type=F32)


def _half_masks(shape):
    lane = lax.broadcasted_iota(jnp.int32, shape, 1)
    return lane < HEAD_DIM, lane >= HEAD_DIM


def _mod_kernel(cc_ref, w_ref, b_ref, o_ref):
    a = cc_ref[...]
    s = (a * jax.nn.sigmoid(a)).astype(BF16)
    o_ref[0] = jnp.dot(s, w_ref[0].astype(BF16), preferred_element_type=F32) + b_ref[0]


def _mod_vectors(cc, w_mod, b_mod):
    depth, d, nm = w_mod.shape
    tn = TN_MOD
    return pl.pallas_call(
        _mod_kernel,
        grid=(depth, nm // tn),
        in_specs=[
            pl.BlockSpec((SUBLANES, d), lambda l, j: (0, 0)),
            pl.BlockSpec((1, d, tn), lambda l, j: (l, 0, j)),
            pl.BlockSpec((1, 1, tn), lambda l, j: (l, 0, j)),
        ],
        out_specs=pl.BlockSpec((1, SUBLANES, tn), lambda l, j: (l, 0, j)),
        out_shape=jax.ShapeDtypeStruct((depth, SUBLANES, nm), F32),
        compiler_params=_params(("parallel", "parallel")),
        name="mod_vectors",
    )(cc, w_mod, b_mod.reshape(depth, 1, nm))


def _ffn_kernel(x_ref, sh_ref, sc_ref, gt_ref, lng_ref, lnb_ref, wg_ref, wu_ref, wo_ref, o_ref):
    j = pl.program_id(1)
    last_j = pl.num_programs(1) - 1
    half = x_ref.shape[0] // 2
    halves = [slice(r * half, (r + 1) * half) for r in range(2)]

    @pl.when(j == 0)
    def _():
        o_ref[...] = jnp.zeros_like(o_ref)

    def accumulate(rows):
        xin = (x_ref[rows] * (1.0 + sc_ref[0]) + sh_ref[0]).astype(BF16)
        g = jnp.dot(xin, wg_ref[...], preferred_element_type=F32)
        u = jnp.dot(xin, wu_ref[...], preferred_element_type=F32)
        hh = ((g * jax.nn.sigmoid(g)) * u).astype(BF16)
        return o_ref[rows] + jnp.dot(hh, wo_ref[...], preferred_element_type=F32)

    @pl.when(j < last_j)
    def _():
        for rows in halves:
            o_ref[rows] = accumulate(rows)

    @pl.when(j == last_j)
    def _():
        acc = [accumulate(rows) for rows in halves]
        for rows, a in zip(halves, acc):
            t = DEEPNORM_ALPHA * x_ref[rows] + (0.5 * gt_ref[0]) * a
            o_ref[rows] = _ln(t, lng_ref[...], lnb_ref[...])


def _ffn_kernel_aliased(*refs):
    _ffn_kernel(*refs[:-2], refs[-1])


def _ffn(x, x_ctx, ctx_row0, l_ctx, mods_l, ks, lng, lnb, w_in, w_out, layer, n):
    d = D_MODEL
    nj = D_FF // TF
    assert n % TM_FFN == 0 and n % TM == 0
    with_ctx = x_ctx is not None
    assert not with_ctx or (TM % l_ctx == 0 and n % l_ctx == 0 and ctx_row0 % l_ctx == 0)
    out_rows = n + TM if with_ctx else n

    def call(src, tm, tile0, tm_out, out_tile0, n_tiles, mod0, alias):
        mod_spec = lambda k: pl.BlockSpec((1, 1, d), lambda i, j: (mod0 + k, 0, 0))
        vec_spec = pl.BlockSpec((1, d), lambda i, j: (0, 0))
        in_specs = [
            pl.BlockSpec((tm, d), lambda i, j: (tile0 + i, 0)),
            mod_spec(ks[0]), mod_spec(ks[1]), mod_spec(ks[2]),
            vec_spec, vec_spec,
            pl.BlockSpec((None, d, TF), lambda i, j: (layer, 0, j)),
            pl.BlockSpec((None, d, TF), lambda i, j: (layer, 0, nj + j)),
            pl.BlockSpec((None, TF, d), lambda i, j: (layer, j, 0)),
        ]
        args = [src, mods_l, mods_l, mods_l, lng, lnb, w_in, w_in, w_out]
        kernel_fn = _ffn_kernel
        aliases = {}
        if alias is not None:
            in_specs.append(pl.BlockSpec(memory_space=pl.ANY))
            args.append(alias)
            aliases = {len(args) - 1: 0}
            kernel_fn = _ffn_kernel_aliased
        return pl.pallas_call(
            kernel_fn,
            grid=(n_tiles, nj),
            in_specs=in_specs,
            out_specs=pl.BlockSpec((tm_out, d), lambda i, j: (out_tile0 + i, 0)),
            out_shape=jax.ShapeDtypeStruct((out_rows, d), F32),
            input_output_aliases=aliases,
            compiler_params=_params(("parallel", "arbitrary")),
            name="ffn" if alias is None else "ffn_ctx",
        )(*args)

    y = call(x, TM_FFN, 0, TM_FFN, 0, n // TM_FFN, 0, None)
    if with_ctx:
        y = call(x_ctx, l_ctx, ctx_row0 // l_ctx, TM, n // TM, 1, N_MOD, y)
    return y


def _rope(t, cos_ref, sin_ref):
    w = t.shape[1]
    lane = lax.broadcasted_iota(jnp.int32, t.shape, 1)
    first = (lane % HEAD_DIM) < (HEAD_DIM // 2)
    rot = jnp.where(first, pltpu.roll(t, w - HEAD_DIM // 2, 1), pltpu.roll(t, HEAD_DIM // 2, 1))
    reps = w // LANES
    cos = jnp.concatenate([cos_ref[...]] * reps, axis=1) if reps > 1 else cos_ref[...]
    sin = jnp.concatenate([sin_ref[...]] * reps, axis=1) if reps > 1 else sin_ref[...]
    return t * cos + rot * sin


_SEG_AQ, _SEG_AK, _SEG_AV, _SEG_BU, _SEG_BV, _SEG_CQ, _SEG_CKV, _SEG_DQ, _SEG_DK, _SEG_DV = range(N_SEG)
_SEG_WIDTHS = tuple(2 * C_KV_HEADS * HEAD_DIM if k == _SEG_CKV else GROUP_WIDTH for k in range(N_SEG))
_SEG_OFFSETS = tuple(sum(_SEG_WIDTHS[:k]) for k in range(N_SEG))
D_PROJ = sum(_SEG_WIDTHS)


def _dup_halves(x):
    swapped = pltpu.roll(x, HEAD_DIM, 1)
    first, _ = _half_masks(x.shape)
    return jnp.where(first, x, swapped), jnp.where(first, swapped, x)


def _inproj_kernel(x_ref, sh_ref, sc_ref, w_ref, cos_ref, sin_ref, *out_refs):
    h = (x_ref[...] * (1.0 + sc_ref[0]) + sh_ref[0]).astype(BF16)

    def project(k):
        return jnp.dot(h, w_ref[:, _SEG_OFFSETS[k]:_SEG_OFFSETS[k] + _SEG_WIDTHS[k]], preferred_element_type=F32)

    def finish(k, t):
        o_ref = out_refs[k]
        if k in (_SEG_AQ, _SEG_AK, _SEG_CQ):
            t = _rope(t, cos_ref, sin_ref)
        if k in (_SEG_AQ, _SEG_CQ, _SEG_DQ):
            t = t * (QK_SCALE * LOG2E)
        if k == _SEG_AQ:
            o_ref[0] = t.T.astype(o_ref.dtype)
        elif k == _SEG_AV:
            tt = t.T.astype(o_ref.dtype)
            extra = A_VROWS - 2 * HEAD_DIM
            ones_row = (lax.broadcasted_iota(jnp.int32, (extra, tt.shape[1]), 0) == 0).astype(o_ref.dtype)
            for hd in range(A_HEADS):
                o_ref[0, hd * A_VROWS:hd * A_VROWS + 2 * HEAD_DIM] = tt[hd * 2 * HEAD_DIM:(hd + 1) * 2 * HEAD_DIM]
                o_ref[0, hd * A_VROWS + 2 * HEAD_DIM:(hd + 1) * A_VROWS] = ones_row
        elif k == _SEG_CKV:
            kk = _dup_halves(_rope(t[:, :LANES], cos_ref, sin_ref))
            vv = _dup_halves(t[:, LANES:])
            o_ref[...] = jnp.concatenate([kk[0], kk[1], vv[0], vv[1]], axis=1).astype(o_ref.dtype)
        else:
            o_ref[...] = t.astype(o_ref.dtype)

    t = project(0)
    for k in range(N_SEG):
        t_next = project(k + 1) if k + 1 < N_SEG else None
        finish(k, t)
        t = t_next


def _inproj(x, mods_l, shift_k, scale_k, w_in, layer, cos_t, sin_t, nlt):
    rows = x.shape[0]

    def mod_spec(k):
        return pl.BlockSpec((1, 1, D_MODEL), lambda i: (jnp.where(i >= nlt, N_MOD, 0) + k, 0, 0))

    n_tiles = rows // TM
    d = D_MODEL
    assert C_KV_HEADS == 2 and w_in.shape[1:] == (d, D_PROJ)
    widths = [SEG] * N_SEG
    dtypes = [BF16] * N_SEG
    dtypes[_SEG_BU] = F32
    dtypes[_SEG_BV] = F32
    out_specs = [pl.BlockSpec((TM, w), lambda i: (i, 0)) for w in widths]
    out_shape = [jax.ShapeDtypeStruct((rows, w), dt) for w, dt in zip(widths, dtypes)]
    for k in (_SEG_AQ, _SEG_AV):
        slab = SEG if k == _SEG_AQ else A_HEADS * A_VROWS
        out_specs[k] = pl.BlockSpec((1, slab, TM), lambda i: (i, 0, 0))
        out_shape[k] = jax.ShapeDtypeStruct((n_tiles, slab, TM), BF16)
    return pl.pallas_call(
        _inproj_kernel,
        grid=(n_tiles,),
        in_specs=[
            pl.BlockSpec((TM, d), lambda i: (i, 0)),
            mod_spec(shift_k), mod_spec(scale_k),
            pl.BlockSpec((None, d, D_PROJ), lambda i: (layer, 0, 0), pipeline_mode=pl.Buffered(1)),
            pl.BlockSpec((TM, LANES), lambda i: (i, 0)),
            pl.BlockSpec((TM, LANES), lambda i: (i, 0)),
        ],
        out_specs=out_specs,
        out_shape=out_shape,
        compiler_params=_params(("parallel",)),
        name="inproj",
    )(x, mods_l, mods_l, w_in, cos_t, sin_t)


def _attn_a_kernel(lam_ref, qt_ref, k_ref, vt_ref, *rest, n_chunks, n_lat_tiles, tail_k_start, tail_chunk, tail_len,
                   lam_init):
    o_ref, m_scr, acc_scr, s_scr, mc_scr = rest
    qt = qt_ref[0]
    comp = lax.broadcasted_iota(jnp.int32, qt.shape, 0) < HEAD_DIM
    zero = jnp.zeros_like(qt)
    qts = (jnp.where(comp, qt, zero), jnp.where(comp, zero, qt))
    m_scr[...] = jnp.full(m_scr.shape, NEG_INF, F32)
    acc_scr[...] = jnp.zeros_like(acc_scr)

    def scores(k, slot, size):
        for c in range(2):
            s = jnp.dot(k, qts[c], preferred_element_type=F32)
            s_scr[slot, c, :size] = s
            mc_scr[slot, c] = jnp.max(s, axis=0, keepdims=True)

    def consume(slot, vt, size):
        for c in range(2):
            m_old = m_scr[c]
            m_new = jnp.maximum(m_old, mc_scr[slot, c])
            alpha = jnp.exp2(m_old - m_new)
            p = jnp.exp2(s_scr[slot, c, :size] - m_new).astype(BF16)
            acc_scr[c] = alpha * acc_scr[c] + jnp.dot(vt, p, preferred_element_type=F32)
            m_scr[c] = m_new

    def k_chunk(i):
        i = jnp.minimum(i, n_chunks - 1)
        return k_ref[pl.ds(pl.multiple_of(i * TM, TM), TM), :]

    scores(k_ref[pl.ds(tail_k_start, tail_len), :], 1, tail_len)
    if n_chunks:
        unroll = math.gcd(n_chunks, A_UNROLL)
        assert unroll % 2 == 0
        scores(k_chunk(0), 0, TM)
    consume(1, vt_ref[tail_chunk][:, :tail_len], tail_len)
    if n_chunks:
        def body(t, carry):
            j = unroll * t
            for u in range(unroll):
                scores(k_chunk(j + u + 1), (u + 1) % 2, TM)
                consume(u % 2, vt_ref[j + u], TM)
            return carry
        trips = jnp.where(pl.program_id(1) < n_lat_tiles, n_chunks // unroll, 0)
        lax.fori_loop(0, trips, body, 0)

    lv = lam_ref[...]
    lam = (jnp.exp(jnp.sum(lv[0:1] * lv[1:2], axis=1, keepdims=True))
           - jnp.exp(jnp.sum(lv[2:3] * lv[3:4], axis=1, keepdims=True)) + lam_init)
    nv = 2 * HEAD_DIM
    ot = (acc_scr[0, :nv] / acc_scr[0, nv:nv + 1]
          - lam * (acc_scr[1, :nv] / acc_scr[1, nv:nv + 1]))
    ms = jnp.mean(ot * ot, axis=0, keepdims=True)
    ot = (ot * lax.rsqrt(ms + LN_EPS)) * (1.0 - lam_init)
    o_ref[...] = ot.T.astype(BF16)


def _attn_a(lam_vecs, aqt, ak, avt, n, l_ctx, lam_init):
    rows = ak.shape[0]
    nt = rows // TM
    nlt = n // TM
    scratch = [pltpu.VMEM((2, 1, TM), F32), pltpu.VMEM((2, A_VROWS, TM), F32),
               pltpu.VMEM((2, 2, TM, TM), F32), pltpu.VMEM((2, 2, 1, TM), F32)]
    lam_spec = pl.BlockSpec((4, HEAD_DIM), lambda h, i: (0, 0))
    return pl.pallas_call(
        functools.partial(_attn_a_kernel, n_chunks=nlt, n_lat_tiles=nlt, tail_k_start=n, tail_chunk=nlt,
                          tail_len=l_ctx, lam_init=lam_init),
        grid=(A_HEADS, nlt + 1),
        in_specs=[
            lam_spec,
            pl.BlockSpec((1, LANES, TM), lambda h, i: (i, h, 0)),
            pl.BlockSpec((rows, LANES), lambda h, i: (0, h)),
            pl.BlockSpec((nt, A_VROWS, TM), lambda h, i: (0, h, 0)),
        ],
        out_specs=pl.BlockSpec((TM, LANES), lambda h, i: (i, h)),
        out_shape=jax.ShapeDtypeStruct((rows, GROUP_WIDTH), BF16),
        scratch_shapes=scratch,
        compiler_params=_params(("parallel", "parallel")),
        name="attn_a",
    )(lam_vecs, aqt, ak, avt)


def _mixb_kernel(u_ref, v_ref, g_ref, b_ref, ws_ref, bs_ref, o_ref):
    u = jax.nn.gelu(u_ref[...])
    v = _ln(jax.nn.gelu(v_ref[...]), g_ref[...], b_ref[...]).astype(BF16)
    for c in range(u.shape[0] // CHUNK):
        rs = slice(c * CHUNK, (c + 1) * CHUNK)
        for g in range(B_GROUPS):
            cs = slice(g * LANES, (g + 1) * LANES)
            mixed = jnp.dot(ws_ref[g].astype(BF16), v[rs, cs], preferred_element_type=F32) + bs_ref[:, g:g + 1]
            o_ref[rs, cs] = (u[rs, cs] * mixed).astype(BF16)


def _mixb(bu, bv, gn_g, gn_b, w_s, b_s_t):
    rows = bu.shape[0]
    gw = GROUP_WIDTH
    tile = pl.BlockSpec((TM, gw), lambda i: (i, 0))
    return pl.pallas_call(
        _mixb_kernel,
        grid=(rows // TM,),
        in_specs=[
            tile, tile,
            pl.BlockSpec((1, gw), lambda i: (0, 0)),
            pl.BlockSpec((1, gw), lambda i: (0, 0)),
            pl.BlockSpec((B_GROUPS, CHUNK, CHUNK), lambda i: (0, 0, 0)),
            pl.BlockSpec((CHUNK, B_GROUPS), lambda i: (0, 0)),
        ],
        out_specs=tile,
        out_shape=jax.ShapeDtypeStruct((rows, gw), BF16),
        compiler_params=_params(("parallel",)),
        name="mix_b",
    )(bu, bv, gn_g, gn_b, w_s, b_s_t)


def _stack_pair(qb):
    first, second = _half_masks(qb.shape)
    zero = jnp.zeros_like(qb)
    return jnp.concatenate([jnp.where(first, qb, zero), jnp.where(second, qb, zero)], axis=0)


def _paired_heads_attention(score_fn, values, n_pairs, sinks=None):
    all_scores = [score_fn(c) for c in range(n_pairs)]
    all_probs = []
    for c, scores in enumerate(all_scores):
        m = jnp.max(scores[0], axis=1, keepdims=True)
        for s in scores[1:]:
            m = jnp.maximum(m, jnp.max(s, axis=1, keepdims=True))
        if sinks is not None:
            m = jnp.maximum(m, sinks[c])
        ps = [jnp.exp2(s - m) for s in scores]
        den = jnp.sum(ps[0], axis=1, keepdims=True)
        for p in ps[1:]:
            den = den + jnp.sum(p, axis=1, keepdims=True)
        if sinks is not None:
            den = den + jnp.exp2(sinks[c] - m)
        all_probs.append(([p.astype(BF16) for p in ps], den))
    outs = []
    for c in range(n_pairs):
        ps, den = all_probs[c]
        num = None
        for p, v in zip(ps, values[c]):
            pv = jnp.dot(p, v, preferred_element_type=F32)
            num = pv if num is None else num + pv
        o = num / den
        r = o.shape[0] // 2
        first, _ = _half_masks((r, LANES))
        outs.append(jnp.where(first, o[:r], o[r:]))
    return outs


def _attn_c_kernel(sink_ref, q_ref, k_ref, v_ref, *rest, n, l_ctx, local, aliased):
    o_ref = rest[-1]
    i = pl.program_id(0)
    n_sub = q_ref.shape[0] // QBLOCK
    wlen = 3 * QBLOCK
    ctx_start = n if local else 0
    n_pairs = C_HEADS // 2
    grp_pairs = n_pairs // C_KV_HEADS
    kv_block = lambda c: slice((c // grp_pairs) * LANES, (c // grp_pairs + 1) * LANES)
    top = lax.broadcasted_iota(jnp.int32, (2 * QBLOCK, 1), 0) < QBLOCK
    pair_sinks = [jnp.where(top, sink_ref[2 * c] * LOG2E, sink_ref[2 * c + 1] * LOG2E) for c in range(n_pairs)]
    sinks = [pair_sinks[c] for _ in range(n_sub) for c in range(n_pairs)]
    q_rows = [slice(b * QBLOCK, (b + 1) * QBLOCK) for b in range(n_sub)]
    if local:
        starts, valids = [], []
        for b in range(n_sub):
            blk = i * n_sub + b
            start = pl.multiple_of(jnp.clip((blk - 1) * QBLOCK, 0, n - wlen), QBLOCK)
            qpos = blk * QBLOCK + lax.broadcasted_iota(jnp.int32, (QBLOCK, wlen), 0)
            kpos = start + lax.broadcasted_iota(jnp.int32, (QBLOCK, wlen), 1)
            starts.append(start)
            valid = jnp.logical_and(jnp.abs(kpos - qpos) <= WINDOW, i < n // q_ref.shape[0])
            valids.append(jnp.concatenate([valid, valid], axis=0))

    def score_fn(idx):
        b, c = divmod(idx, n_pairs)
        q2 = _stack_pair(q_ref[q_rows[b], c * LANES:(c + 1) * LANES])
        scores = [_dot_nt(q2, k_ref[pl.ds(ctx_start, l_ctx), kv_block(c)])]
        if local:
            scores.append(jnp.where(valids[b], _dot_nt(q2, k_ref[pl.ds(starts[b], wlen), kv_block(c)]), NEG_INF))
        return scores

    values = []
    for b in range(n_sub):
        for c in range(n_pairs):
            vals = [v_ref[pl.ds(ctx_start, l_ctx), kv_block(c)]]
            if local:
                vals.append(v_ref[pl.ds(starts[b], wlen), kv_block(c)])
            values.append(vals)
    outs = _paired_heads_attention(score_fn, values, n_sub * n_pairs, sinks)
    for idx, out in enumerate(outs):
        b, c = divmod(idx, n_pairs)
        o_ref[q_rows[b], c * LANES:(c + 1) * LANES] = out.astype(BF16)


def _attn_c(sink, cq, ckv, n, l_ctx):
    rows = cq.shape[0]
    gw = GROUP_WIDTH
    kvw = C_KV_HEADS * LANES
    smem = pl.BlockSpec(memory_space=pltpu.SMEM)
    return pl.pallas_call(
        functools.partial(_attn_c_kernel, n=n, l_ctx=l_ctx, local=True, aliased=False),
        grid=(n // TM + 1,),
        in_specs=[
            smem,
            pl.BlockSpec((TM, gw), lambda i: (i, 0)),
            pl.BlockSpec((rows, kvw), lambda i: (0, 0)),
            pl.BlockSpec((rows, kvw), lambda i: (0, 1)),
        ],
        out_specs=pl.BlockSpec((TM, gw), lambda i: (i, 0)),
        out_shape=jax.ShapeDtypeStruct((rows, gw), BF16),
        compiler_params=_params(("parallel",)),
        name="attn_c",
    )(sink, cq, ckv, ckv)


def _attn_d_kernel(q_ref, kc_ref, vc_ref, *rest, local, aliased, n_rows):
    o_ref = rest[-1]
    n_pairs = D_HEADS // 2
    lane_block = lambda c: slice(c * LANES, (c + 1) * LANES)
    nk = NA_ROWS * GRID_W
    if local:
        k_ref, v_ref, b_ref = rest[0], rest[1], rest[2]
        g = pl.program_id(0)
        win0 = jnp.clip(g * D_ROWS - NA_ROWS // 2, 0, n_rows - (D_ROWS + NA_ROWS - 1))
        q_rows, offs, shifts = [], [], []
        for i in range(D_ROWS):
            r = g * D_ROWS + i
            rs = jnp.clip(r - NA_ROWS // 2, 0, n_rows - NA_ROWS)
            q_rows.append(slice(i * GRID_W, (i + 1) * GRID_W))
            offs.append(pl.multiple_of((rs - win0) * GRID_W, GRID_W))
            shifts.append(jnp.where(g < n_rows // D_ROWS, r - rs, NA_ROWS))
    else:
        q_rows = [slice(0, q_ref.shape[0])]

    def score_fn(idx):
        i, c = divmod(idx, n_pairs)
        q2 = _stack_pair(q_ref[q_rows[i], lane_block(c)])
        scores = [_dot_nt(q2, kc_ref[:, lane_block(c)])]
        if local:
            bias = jnp.concatenate([b_ref[shifts[i], 2 * c], b_ref[shifts[i], 2 * c + 1]], axis=0)
            scores.append(_dot_nt(q2, k_ref[pl.ds(offs[i], nk), lane_block(c)]) + bias)
        return scores

    values = []
    for i in range(len(q_rows)):
        for c in range(n_pairs):
            vals = [vc_ref[:, lane_block(c)]]
            if local:
                vals.append(v_ref[pl.ds(offs[i], nk), lane_block(c)])
            values.append(vals)
    outs = _paired_heads_attention(score_fn, values, len(q_rows) * n_pairs)
    for idx, out in enumerate(outs):
        i, c = divmod(idx, n_pairs)
        o_ref[q_rows[i], lane_block(c)] = out.astype(BF16)


def _attn_d(dq, dk, dv, bias, n, l_ctx):
    rows = dq.shape[0]
    gw = GROUP_WIDTH
    n_rows = n // GRID_W
    nk = NA_ROWS * GRID_W
    cb = n // l_ctx
    win = D_ROWS + NA_ROWS - 1
    assert n_rows % D_ROWS == 0 and n_rows >= win

    def window_start(g):
        return jnp.clip(g * D_ROWS - NA_ROWS // 2, 0, n_rows - win) * GRID_W

    once = dict(pipeline_mode=pl.Buffered(1))
    return pl.pallas_call(
        functools.partial(_attn_d_kernel, local=True, aliased=False, n_rows=n_rows),
        grid=(n_rows // D_ROWS + 1,),
        in_specs=[
            pl.BlockSpec((D_ROWS * GRID_W, gw), lambda g: (g, 0)),
            pl.BlockSpec((l_ctx, gw), lambda g: (cb, 0), **once),
            pl.BlockSpec((l_ctx, gw), lambda g: (cb, 0), **once),
            pl.BlockSpec((pl.Element(win * GRID_W), pl.Element(gw)), lambda g: (window_start(g), 0)),
            pl.BlockSpec((pl.Element(win * GRID_W), pl.Element(gw)), lambda g: (window_start(g), 0)),
            pl.BlockSpec((NA_ROWS + 1, D_HEADS, GRID_W, nk), lambda g: (0, 0, 0, 0), **once),
        ],
        out_specs=pl.BlockSpec((D_ROWS * GRID_W, gw), lambda g: (g, 0)),
        out_shape=jax.ShapeDtypeStruct((rows, gw), BF16),
        compiler_params=_params(("parallel",)),
        name="attn_d",
    )(dq, dk, dv, dk, dv, bias)


def _outproj_kernel(ya_ref, yb_ref, yc_ref, yd_ref, w_ref, x_ref, gt_ref, lng_ref, lnb_ref, o_ref):
    y_in = jnp.concatenate([ya_ref[...], yb_ref[...], yc_ref[...], yd_ref[...]], axis=1)
    y = jnp.dot(y_in, w_ref[...], preferred_element_type=F32)
    t = DEEPNORM_ALPHA * x_ref[...] + gt_ref[0] * y
    o_ref[...] = _ln(t, lng_ref[...], lnb_ref[...])


def _outproj(ya, yb, yc, yd, w_out, layer, x, mods_l, gate_k, lng, lnb, n_tiles, nlt):
    d = D_MODEL
    gw = GROUP_WIDTH
    ytile = pl.BlockSpec((TM, gw), lambda i: (i, 0))
    vec = pl.BlockSpec((1, d), lambda i: (0, 0))
    return pl.pallas_call(
        _outproj_kernel,
        grid=(n_tiles,),
        in_specs=[
            ytile, ytile, ytile, ytile,
            pl.BlockSpec((None, d, d), lambda i: (layer, 0, 0), pipeline_mode=pl.Buffered(1)),
            pl.BlockSpec((TM, d), lambda i: (i, 0)),
            pl.BlockSpec((1, 1, d), lambda i: (jnp.where(i >= nlt, N_MOD, 0) + gate_k, 0, 0)),
            vec, vec,
        ],
        out_specs=pl.BlockSpec((TM, d), lambda i: (i, 0)),
        out_shape=jax.ShapeDtypeStruct((n_tiles * TM, d), F32),
        compiler_params=_params(("parallel",)),
        name="outproj",
    )(ya, yb, yc, yd, w_out, x, mods_l, lng, lnb)


def _rope_tables(n, rows):
    t = jnp.arange(n, dtype=jnp.int32)
    row = (t // GRID_W).astype(F32)
    col = (t % GRID_W).astype(F32)
    n_freq = HEAD_DIM // 4
    inv = ROPE_THETA ** (-jnp.arange(n_freq, dtype=F32) / n_freq)
    ang = jnp.concatenate([row[:, None] * inv, col[:, None] * inv], axis=-1)
    cos, sin = jnp.cos(ang), jnp.sin(ang)
    cos_t = jnp.tile(jnp.concatenate([cos, cos], axis=-1), (1, LANES // HEAD_DIM))
    sin_t = jnp.tile(jnp.concatenate([-sin, sin], axis=-1), (1, LANES // HEAD_DIM))
    pad = rows - n
    cos_t = jnp.concatenate([cos_t, jnp.ones((pad, LANES), F32)], axis=0)
    sin_t = jnp.concatenate([sin_t, jnp.zeros((pad, LANES), F32)], axis=0)
    return cos_t, sin_t


def _na_bias(rpb):
    w = GRID_W
    cq = jnp.arange(w)
    cs = jnp.clip(cq - NA_COLS // 2, 0, w - NA_COLS)
    col_ok = (cq[None, :] >= cs[:, None]) & (cq[None, :] < cs[:, None] + NA_COLS)
    edge = w - NA_COLS
    ext = jnp.pad(rpb.astype(F32), ((0, 0), (0, 0), (edge, edge)), mode="edge")
    toep = jnp.stack([ext[:, :, w - 1 - q:2 * w - 1 - q] for q in range(w)], axis=2)
    toep = jnp.where(col_ok[None, None], toep, NEG_INF)
    per_shift = []
    for shift in range(NA_ROWS):
        sl = toep[:, NA_ROWS - 1 - shift:2 * NA_ROWS - 1 - shift]
        per_shift.append(jnp.swapaxes(sl, 1, 2).reshape(rpb.shape[0], w, NA_ROWS * w))
    bias = jnp.stack(per_shift, axis=0) * LOG2E
    return jnp.concatenate([bias, jnp.full((1,) + bias.shape[1:], NEG_INF, F32)], axis=0)


def kernel(x, c, ctx, c_ctx, w_mod, b_mod, ln_g, ln_b, ffn1_w_in, ffn1_w_out, ffn2_w_in, ffn2_w_out, mix_w_in, mix_w_out, a_lambda, b_norm_g, b_norm_b, b_spatial_w, b_spatial_b, c_sink, d_rpb):
    depth = w_mod.shape[0]
    n = x.shape[1]
    l_ctx = ctx.shape[1]
    d = D_MODEL
    assert x.shape[0] == 1 and n % TM == 0 and n // GRID_W >= NA_ROWS and n % TM_FFN == 0
    assert l_ctx % CHUNK == 0 and l_ctx <= TM and n % l_ctx == 0
    nlt = n // TM
    rows = n + TM

    cc = jnp.concatenate([c, c_ctx[None], jnp.zeros((SUBLANES - 2, d), F32)], axis=0)
    mods = _mod_vectors(cc, w_mod, b_mod)[:, :2].reshape(depth, 2 * N_MOD, 1, d)
    cos_t, sin_t = _rope_tables(n, rows)
    w1_in, w1_out, w2_in, w2_out, wm_in, wm_out = (
        w.astype(BF16) for w in (ffn1_w_in, ffn1_w_out, ffn2_w_in, ffn2_w_out, mix_w_in, mix_w_out))

    for l in range(depth):
        last = l == depth - 1
        lam_init = 0.8 - 0.6 * math.exp(-0.3 * l)
        mods_l = mods[l]
        lng = [ln_g[l, k][None] for k in range(3)]
        lnb = [ln_b[l, k][None] for k in range(3)]

        if l == 0:
            xs = _ffn(x[0], ctx[0], 0, l_ctx, mods_l, (0, 1, 2), lng[0], lnb[0], w1_in, w1_out, l, n)
        else:
            xs = _ffn(xs, xs, n, l_ctx, mods_l, (0, 1, 2), lng[0], lnb[0], w1_in, w1_out, l, n)
        aq, ak, av, bu, bv, cq, ckv, dq, dk, dv = _inproj(xs, mods_l, 3, 4, wm_in, l, cos_t, sin_t, nlt)
        ya = _attn_a(a_lambda[l], aq, ak, av, n, l_ctx, lam_init)
        yb = _mixb(bu, bv, b_norm_g[l][None], b_norm_b[l][None], b_spatial_w[l], b_spatial_b[l].T)
        yc = _attn_c(c_sink[l], cq, ckv, n, l_ctx)
        yd = _attn_d(dq, dk, dv, _na_bias(d_rpb[l]), n, l_ctx)
        n_tiles = nlt if last else nlt + 1
        xs = _outproj(ya, yb, yc, yd, wm_out, l, xs, mods_l, 5, lng[1], lnb[1], n_tiles, nlt)
        xs = _ffn(xs, None if last else xs, n, l_ctx, mods_l, (6, 7, 8), lng[2], lnb[2], w2_in, w2_out, l, n)
    return xs[None]
```

```python
import functools
import math

import jax
import jax.numpy as jnp
from jax import lax
from jax.experimental import pallas as pl
from jax.experimental.pallas import tpu as pltpu

F32 = jnp.float32
BF16 = jnp.bfloat16

D_MODEL = 2048
N_GROUPS = 4
GROUP_WIDTH = D_MODEL // N_GROUPS
HEAD_DIM = 64
GRID_W = 64
CHUNK = 128
B_GROUPS = GROUP_WIDTH // CHUNK
A_HEADS = GROUP_WIDTH // (2 * HEAD_DIM)
C_HEADS = GROUP_WIDTH // HEAD_DIM
C_KV_HEADS = C_HEADS // 4
D_HEADS = GROUP_WIDTH // HEAD_DIM
WINDOW = 128
QBLOCK = 128
NA_ROWS = 8
NA_COLS = 16
D_FF = 256 * math.ceil(8 * D_MODEL / 3 / 256)
N_MOD = 9
ROPE_THETA = 10000.0
LN_EPS = 1e-6
NEG_INF = -1e30
MODEL_DEPTH = 4
DEEPNORM_ALPHA = (2 * MODEL_DEPTH) ** 0.25
QK_SCALE = HEAD_DIM ** -0.5
LOG2E = math.log2(math.e)

LANES = 128
SUBLANES = 8
BF16_SUBLANES = 16
A_VROWS = 2 * HEAD_DIM + BF16_SUBLANES
A_UNROLL = 16
D_ROWS = 8
TN_MOD = 2048
TM = 512
TM_FFN = 1024
TF = 512
SEG = 512
N_SEG = 10
VMEM_LIMIT = 60 * 1024 * 1024


def _params(sem):
    return pltpu.CompilerParams(dimension_semantics=sem, vmem_limit_bytes=VMEM_LIMIT)


def _ln(t, g, b):
    mu = jnp.mean(t, axis=-1, keepdims=True)
    tc = t - mu
    var = jnp.mean(tc * tc, axis=-1, keepdims=True)
    return tc * lax.rsqrt(var + LN_EPS) * g + b


def _dot_nt(a, b):
    return lax.dot_general(a, b, (((1,), (1,)), ((), ())), preferred_element_type=F32)


def _half_masks(shape):
    lane = lax.broadcasted_iota(jnp.int32, shape, 1)
    return lane < HEAD_DIM, lane >= HEAD_DIM


def _mod_kernel(cc_ref, w_ref, b_ref, o_ref):
    a = cc_ref[...]
    s = (a * jax.nn.sigmoid(a)).astype(BF16)
    o_ref[0] = jnp.dot(s, w_ref[0].astype(BF16), preferred_element_type=F32) + b_ref[0]


def _mod_vectors(cc, w_mod, b_mod):
    depth, d, nm = w_mod.shape
    tn = TN_MOD
    return pl.pallas_call(
        _mod_kernel,
        grid=(depth, nm // tn),
        in_specs=[
            pl.BlockSpec((SUBLANES, d), lambda l, j: (0, 0)),
            pl.BlockSpec((1, d, tn), lambda l, j: (l, 0, j)),
            pl.BlockSpec((1, 1, tn), lambda l, j: (l, 0, j)),
        ],
        out_specs=pl.BlockSpec((1, SUBLANES, tn), lambda l, j: (l, 0, j)),
        out_shape=jax.ShapeDtypeStruct((depth, SUBLANES, nm), F32),
        compiler_params=_params(("parallel", "parallel")),
        name="mod_vectors",
    )(cc, w_mod, b_mod.reshape(depth, 1, nm))


def _ffn_kernel(x_ref, sh_ref, sc_ref, gt_ref, lng_ref, lnb_ref, wg_ref, wu_ref, wo_ref, o_ref):
    j = pl.program_id(1)
    last_j = pl.num_programs(1) - 1
    half = x_ref.shape[0] // 2
    halves = [slice(r * half, (r + 1) * half) for r in range(2)]

    @pl.when(j == 0)
    def _():
        o_ref[...] = jnp.zeros_like(o_ref)

    def accumulate(rows):
        xin = (x_ref[rows] * (1.0 + sc_ref[0]) + sh_ref[0]).astype(BF16)
        g = jnp.dot(xin, wg_ref[...], preferred_element_type=F32)
        u = jnp.dot(xin, wu_ref[...], preferred_element_type=F32)
        hh = ((g * jax.nn.sigmoid(g)) * u).astype(BF16)
        return o_ref[rows] + jnp.dot(hh, wo_ref[...], preferred_element_type=F32)

    @pl.when(j < last_j)
    def _():
        for rows in halves:
            o_ref[rows] = accumulate(rows)

    @pl.when(j == last_j)
    def _():
        acc = [accumulate(rows) for rows in halves]
        for rows, a in zip(halves, acc):
            t = DEEPNORM_ALPHA * x_ref[rows] + (0.5 * gt_ref[0]) * a
            o_ref[rows] = _ln(t, lng_ref[...], lnb_ref[...])


def _ffn_kernel_aliased(*refs):
    _ffn_kernel(*refs[:-2], refs[-1])


def _ffn(x, x_ctx, ctx_row0, l_ctx, mods_l, ks, lng, lnb, w_in, w_out, layer, n):
    d = D_MODEL
    nj = D_FF // TF
    assert n % TM_FFN == 0 and n % TM == 0
    with_ctx = x_ctx is not None
    assert not with_ctx or (TM % l_ctx == 0 and n % l_ctx == 0 and ctx_row0 % l_ctx == 0)
    out_rows = n + TM if with_ctx else n

    def call(src, tm, tile0, tm_out, out_tile0, n_tiles, mod0, alias):
        mod_spec = lambda k: pl.BlockSpec((1, 1, d), lambda i, j: (mod0 + k, 0, 0))
        vec_spec = pl.BlockSpec((1, d), lambda i, j: (0, 0))
        in_specs = [
            pl.BlockSpec((tm, d), lambda i, j: (tile0 + i, 0)),
            mod_spec(ks[0]), mod_spec(ks[1]), mod_spec(ks[2]),
            vec_spec, vec_spec,
            pl.BlockSpec((None, d, TF), lambda i, j: (layer, 0, j)),
            pl.BlockSpec((None, d, TF), lambda i, j: (layer, 0, nj + j)),
            pl.BlockSpec((None, TF, d), lambda i, j: (layer, j, 0)),
        ]
        args = [src, mods_l, mods_l, mods_l, lng, lnb, w_in, w_in, w_out]
        kernel_fn = _ffn_kernel
        aliases = {}
        if alias is not None:
            in_specs.append(pl.BlockSpec(memory_space=pl.ANY))
            args.append(alias)
            aliases = {len(args) - 1: 0}
            kernel_fn = _ffn_kernel_aliased
        return pl.pallas_call(
            kernel_fn,
            grid=(n_tiles, nj),
            in_specs=in_specs,
            out_specs=pl.BlockSpec((tm_out, d), lambda i, j: (out_tile0 + i, 0)),
            out_shape=jax.ShapeDtypeStruct((out_rows, d), F32),
            input_output_aliases=aliases,
            compiler_params=_params(("parallel", "arbitrary")),
            name="ffn" if alias is None else "ffn_ctx",
        )(*args)

    y = call(x, TM_FFN, 0, TM_FFN, 0, n // TM_FFN, 0, None)
    if with_ctx:
        y = call(x_ctx, l_ctx, ctx_row0 // l_ctx, TM, n // TM, 1, N_MOD, y)
    return y


def _rope(t, cos_ref, sin_ref):
    w = t.shape[1]
    lane = lax.broadcasted_iota(jnp.int32, t.shape, 1)
    first = (lane % HEAD_DIM) < (HEAD_DIM // 2)
    rot = jnp.where(first, pltpu.roll(t, w - HEAD_DIM // 2, 1), pltpu.roll(t, HEAD_DIM // 2, 1))
    reps = w // LANES
    cos = jnp.concatenate([cos_ref[...]] * reps, axis=1) if reps > 1 else cos_ref[...]
    sin = jnp.concatenate([sin_ref[...]] * reps, axis=1) if reps > 1 else sin_ref[...]
    return t * cos + rot * sin


_SEG_AQ, _SEG_AK, _SEG_AV, _SEG_BU, _SEG_BV, _SEG_CQ, _SEG_CKV, _SEG_DQ, _SEG_DK, _SEG_DV = range(N_SEG)
_SEG_WIDTHS = tuple(2 * C_KV_HEADS * HEAD_DIM if k == _SEG_CKV else GROUP_WIDTH for k in range(N_SEG))
_SEG_OFFSETS = tuple(sum(_SEG_WIDTHS[:k]) for k in range(N_SEG))
D_PROJ = sum(_SEG_WIDTHS)


def _dup_halves(x):
    swapped = pltpu.roll(x, HEAD_DIM, 1)
    first, _ = _half_masks(x.shape)
    return jnp.where(first, x, swapped), jnp.where(first, swapped, x)


def _inproj_kernel(x_ref, sh_ref, sc_ref, w_ref, cos_ref, sin_ref, *out_refs):
    h = (x_ref[...] * (1.0 + sc_ref[0]) + sh_ref[0]).astype(BF16)

    def project(k):
        return jnp.dot(h, w_ref[:, _SEG_OFFSETS[k]:_SEG_OFFSETS[k] + _SEG_WIDTHS[k]], preferred_element_type=F32)

    def finish(k, t):
        o_ref = out_refs[k]
        if k in (_SEG_AQ, _SEG_AK, _SEG_CQ):
            t = _rope(t, cos_ref, sin_ref)
        if k in (_SEG_AQ, _SEG_CQ, _SEG_DQ):
            t = t * (QK_SCALE * LOG2E)
        if k == _SEG_AQ:
            o_ref[0] = t.T.astype(o_ref.dtype)
        elif k == _SEG_AV:
            tt = t.T.astype(o_ref.dtype)
            extra = A_VROWS - 2 * HEAD_DIM
            ones_row = (lax.broadcasted_iota(jnp.int32, (extra, tt.shape[1]), 0) == 0).astype(o_ref.dtype)
            for hd in range(A_HEADS):
                o_ref[0, hd * A_VROWS:hd * A_VROWS + 2 * HEAD_DIM] = tt[hd * 2 * HEAD_DIM:(hd + 1) * 2 * HEAD_DIM]
                o_ref[0, hd * A_VROWS + 2 * HEAD_DIM:(hd + 1) * A_VROWS] = ones_row
        elif k == _SEG_CKV:
            kk = _dup_halves(_rope(t[:, :LANES], cos_ref, sin_ref))
            vv = _dup_halves(t[:, LANES:])
            o_ref[...] = jnp.concatenate([kk[0], kk[1], vv[0], vv[1]], axis=1).astype(o_ref.dtype)
        else:
            o_ref[...] = t.astype(o_ref.dtype)

    t = project(0)
    for k in range(N_SEG):
        t_next = project(k + 1) if k + 1 < N_SEG else None
        finish(k, t)
        t = t_next


def _inproj(x, mods_l, shift_k, scale_k, w_in, layer, cos_t, sin_t, nlt):
    rows = x.shape[0]

    def mod_spec(k):
        return pl.BlockSpec((1, 1, D_MODEL), lambda i: (jnp.where(i >= nlt, N_MOD, 0) + k, 0, 0))

    n_tiles = rows // TM
    d = D_MODEL
    assert C_KV_HEADS == 2 and w_in.shape[1:] == (d, D_PROJ)
    widths = [SEG] * N_SEG
    dtypes = [BF16] * N_SEG
    dtypes[_SEG_BU] = F32
    dtypes[_SEG_BV] = F32
    out_specs = [pl.BlockSpec((TM, w), lambda i: (i, 0)) for w in widths]
    out_shape = [jax.ShapeDtypeStruct((rows, w), dt) for w, dt in zip(widths, dtypes)]
    for k in (_SEG_AQ, _SEG_AV):
        slab = SEG if k == _SEG_AQ else A_HEADS * A_VROWS
        out_specs[k] = pl.BlockSpec((1, slab, TM), lambda i: (i, 0, 0))
        out_shape[k] = jax.ShapeDtypeStruct((n_tiles, slab, TM), BF16)
    return pl.pallas_call(
        _inproj_kernel,
        grid=(n_tiles,),
        in_specs=[
            pl.BlockSpec((TM, d), lambda i: (i, 0)),
            mod_spec(shift_k), mod_spec(scale_k),
            pl.BlockSpec((None, d, D_PROJ), lambda i: (layer, 0, 0), pipeline_mode=pl.Buffered(1)),
            pl.BlockSpec((TM, LANES), lambda i: (i, 0)),
            pl.BlockSpec((TM, LANES), lambda i: (i, 0)),
        ],
        out_specs=out_specs,
        out_shape=out_shape,
        compiler_params=_params(("parallel",)),
        name="inproj",
    )(x, mods_l, mods_l, w_in, cos_t, sin_t)


def _attn_a_kernel(lam_ref, qt_ref, k_ref, vt_ref, *rest, n_chunks, tail_k_start, tail_chunk, tail_len, lam_init,
                   aliased):
    if aliased:
        rest = rest[1:]
    o_ref, m_scr, acc_scr, s_scr, mc_scr = rest
    qt = qt_ref[0]
    comp = lax.broadcasted_iota(jnp.int32, qt.shape, 0) < HEAD_DIM
    zero = jnp.zeros_like(qt)
    qts = (jnp.where(comp, qt, zero), jnp.where(comp, zero, qt))
    m_scr[...] = jnp.full(m_scr.shape, NEG_INF, F32)
    acc_scr[...] = jnp.zeros_like(acc_scr)

    def scores(k, slot, size):
        for c in range(2):
            s = jnp.dot(k, qts[c], preferred_element_type=F32)
            s_scr[slot, c, :size] = s
            mc_scr[slot, c] = jnp.max(s, axis=0, keepdims=True)

    def consume(slot, vt, size):
        for c in range(2):
            m_old = m_scr[c]
            m_new = jnp.maximum(m_old, mc_scr[slot, c])
            alpha = jnp.exp2(m_old - m_new)
            p = jnp.exp2(s_scr[slot, c, :size] - m_new).astype(BF16)
            acc_scr[c] = alpha * acc_scr[c] + jnp.dot(vt, p, preferred_element_type=F32)
            m_scr[c] = m_new

    def k_chunk(i):
        i = jnp.minimum(i, n_chunks - 1)
        return k_ref[pl.ds(pl.multiple_of(i * TM, TM), TM), :]

    scores(k_ref[pl.ds(tail_k_start, tail_len), :], 1, tail_len)
    if n_chunks:
        unroll = math.gcd(n_chunks, A_UNROLL)
        assert unroll % 2 == 0
        scores(k_chunk(0), 0, TM)
    consume(1, vt_ref[tail_chunk][:, :tail_len], tail_len)
    if n_chunks:
        def body(t, carry):
            j = unroll * t
            for u in range(unroll):
                scores(k_chunk(j + u + 1), (u + 1) % 2, TM)
                consume(u % 2, vt_ref[j + u], TM)
            return carry
        lax.fori_loop(0, n_chunks // unroll, body, 0)

    lv = lam_ref[...]
    lam = (jnp.exp(jnp.sum(lv[0:1] * lv[1:2], axis=1, keepdims=True))
           - jnp.exp(jnp.sum(lv[2:3] * lv[3:4], axis=1, keepdims=True)) + lam_init)
    nv = 2 * HEAD_DIM
    ot = (acc_scr[0, :nv] / acc_scr[0, nv:nv + 1]
          - lam * (acc_scr[1, :nv] / acc_scr[1, nv:nv + 1]))
    ms = jnp.mean(ot * ot, axis=0, keepdims=True)
    ot = (ot * lax.rsqrt(ms + LN_EPS)) * (1.0 - lam_init)
    o_ref[...] = ot.T.astype(BF16)


def _attn_a(lam_vecs, aqt, ak, avt, n, l_ctx, lam_init, need_ctx):
    rows = ak.shape[0]
    nt = rows // TM
    nlt = n // TM
    scratch = [pltpu.VMEM((2, 1, TM), F32), pltpu.VMEM((2, A_VROWS, TM), F32),
               pltpu.VMEM((2, 2, TM, TM), F32), pltpu.VMEM((2, 2, 1, TM), F32)]
    lam_spec = pl.BlockSpec((4, HEAD_DIM), lambda h, i: (0, 0))
    ya = pl.pallas_call(
        functools.partial(_attn_a_kernel, n_chunks=nlt, tail_k_start=n, tail_chunk=nlt, tail_len=l_ctx,
                          lam_init=lam_init, aliased=False),
        grid=(A_HEADS, nlt),
        in_specs=[
            lam_spec,
            pl.BlockSpec((1, LANES, TM), lambda h, i: (i, h, 0)),
            pl.BlockSpec((rows, LANES), lambda h, i: (0, h)),
            pl.BlockSpec((nt, A_VROWS, TM), lambda h, i: (0, h, 0)),
        ],
        out_specs=pl.BlockSpec((TM, LANES), lambda h, i: (i, h)),
        out_shape=jax.ShapeDtypeStruct((rows, GROUP_WIDTH), BF16),
        scratch_shapes=scratch,
        compiler_params=_params(("parallel", "parallel")),
        name="attn_a",
    )(lam_vecs, aqt, ak, avt)
    if not need_ctx:
        return ya
    cb = n // l_ctx
    return pl.pallas_call(
        functools.partial(_attn_a_kernel, n_chunks=0, tail_k_start=0, tail_chunk=0, tail_len=l_ctx,
                          lam_init=lam_init, aliased=True),
        grid=(A_HEADS, 1),
        in_specs=[
            lam_spec,
            pl.BlockSpec((1, LANES, TM), lambda h, i: (nlt, h, 0)),
            pl.BlockSpec((l_ctx, LANES), lambda h, i: (cb, h)),
            pl.BlockSpec((1, A_VROWS, TM), lambda h, i: (nlt, h, 0)),
            pl.BlockSpec(memory_space=pl.ANY),
        ],
        out_specs=pl.BlockSpec((TM, LANES), lambda h, i: (nlt, h)),
        out_shape=jax.ShapeDtypeStruct((rows, GROUP_WIDTH), BF16),
        scratch_shapes=scratch,
        input_output_aliases={4: 0},
        compiler_params=_params(("parallel", "parallel")),
        name="attn_a_ctx",
    )(lam_vecs, aqt, ak, avt, ya)


def _mixb_kernel(u_ref, v_ref, g_ref, b_ref, ws_ref, bs_ref, o_ref):
    u = jax.nn.gelu(u_ref[...])
    v = _ln(jax.nn.gelu(v_ref[...]), g_ref[...], b_ref[...]).astype(BF16)
    for c in range(u.shape[0] // CHUNK):
        rs = slice(c * CHUNK, (c + 1) * CHUNK)
        for g in range(B_GROUPS):
            cs = slice(g * LANES, (g + 1) * LANES)
            mixed = jnp.dot(ws_ref[g].astype(BF16), v[rs, cs], preferred_element_type=F32) + bs_ref[:, g:g + 1]
            o_ref[rs, cs] = (u[rs, cs] * mixed).astype(BF16)


def _mixb(bu, bv, gn_g, gn_b, w_s, b_s_t):
    rows = bu.shape[0]
    gw = GROUP_WIDTH
    tile = pl.BlockSpec((TM, gw), lambda i: (i, 0))
    return pl.pallas_call(
        _mixb_kernel,
        grid=(rows // TM,),
        in_specs=[
            tile, tile,
            pl.BlockSpec((1, gw), lambda i: (0, 0)),
            pl.BlockSpec((1, gw), lambda i: (0, 0)),
            pl.BlockSpec((B_GROUPS, CHUNK, CHUNK), lambda i: (0, 0, 0)),
            pl.BlockSpec((CHUNK, B_GROUPS), lambda i: (0, 0)),
        ],
        out_specs=tile,
        out_shape=jax.ShapeDtypeStruct((rows, gw), BF16),
        compiler_params=_params(("parallel",)),
        name="mix_b",
    )(bu, bv, gn_g, gn_b, w_s, b_s_t)


def _stack_pair(qb):
    first, second = _half_masks(qb.shape)
    zero = jnp.zeros_like(qb)
    return jnp.concatenate([jnp.where(first, qb, zero), jnp.where(second, qb, zero)], axis=0)


def _paired_heads_attention(score_fn, values, n_pairs, sinks=None):
    all_scores = [score_fn(c) for c in range(n_pairs)]
    all_probs = []
    for c, scores in enumerate(all_scores):
        m = jnp.max(scores[0], axis=1, keepdims=True)
        for s in scores[1:]:
            m = jnp.maximum(m, jnp.max(s, axis=1, keepdims=True))
        if sinks is not None:
            m = jnp.maximum(m, sinks[c])
        ps = [jnp.exp2(s - m) for s in scores]
        den = jnp.sum(ps[0], axis=1, keepdims=True)
        for p in ps[1:]:
            den = den + jnp.sum(p, axis=1, keepdims=True)
        if sinks is not None:
            den = den + jnp.exp2(sinks[c] - m)
        all_probs.append(([p.astype(BF16) for p in ps], den))
    outs = []
    for c in range(n_pairs):
        ps, den = all_probs[c]
        num = None
        for p, v in zip(ps, values[c]):
            pv = jnp.dot(p, v, preferred_element_type=F32)
            num = pv if num is None else num + pv
        o = num / den
        r = o.shape[0] // 2
        first, _ = _half_masks((r, LANES))
        outs.append(jnp.where(first, o[:r], o[r:]))
    return outs


def _attn_c_kernel(sink_ref, q_ref, k_ref, v_ref, *rest, n, l_ctx, local, aliased):
    o_ref = rest[-1]
    i = pl.program_id(0)
    n_sub = q_ref.shape[0] // QBLOCK
    wlen = 3 * QBLOCK
    ctx_start = n if local else 0
    n_pairs = C_HEADS // 2
    grp_pairs = n_pairs // C_KV_HEADS
    kv_block = lambda c: slice((c // grp_pairs) * LANES, (c // grp_pairs + 1) * LANES)
    top = lax.broadcasted_iota(jnp.int32, (2 * QBLOCK, 1), 0) < QBLOCK
    pair_sinks = [jnp.where(top, sink_ref[2 * c] * LOG2E, sink_ref[2 * c + 1] * LOG2E) for c in range(n_pairs)]
    sinks = [pair_sinks[c] for _ in range(n_sub) for c in range(n_pairs)]
    q_rows = [slice(b * QBLOCK, (b + 1) * QBLOCK) for b in range(n_sub)]
    if local:
        starts, valids = [], []
        for b in range(n_sub):
            blk = i * n_sub + b
            start = pl.multiple_of(jnp.clip((blk - 1) * QBLOCK, 0, n - wlen), QBLOCK)
            qpos = blk * QBLOCK + lax.broadcasted_iota(jnp.int32, (QBLOCK, wlen), 0)
            kpos = start + lax.broadcasted_iota(jnp.int32, (QBLOCK, wlen), 1)
            starts.append(start)
            valid = jnp.abs(kpos - qpos) <= WINDOW
            valids.append(jnp.concatenate([valid, valid], axis=0))

    def score_fn(idx):
        b, c = divmod(idx, n_pairs)
        q2 = _stack_pair(q_ref[q_rows[b], c * LANES:(c + 1) * LANES])
        scores = [_dot_nt(q2, k_ref[pl.ds(ctx_start, l_ctx), kv_block(c)])]
        if local:
            scores.append(jnp.where(valids[b], _dot_nt(q2, k_ref[pl.ds(starts[b], wlen), kv_block(c)]), NEG_INF))
        return scores

    values = []
    for b in range(n_sub):
        for c in range(n_pairs):
            vals = [v_ref[pl.ds(ctx_start, l_ctx), kv_block(c)]]
            if local:
                vals.append(v_ref[pl.ds(starts[b], wlen), kv_block(c)])
            values.append(vals)
    outs = _paired_heads_attention(score_fn, values, n_sub * n_pairs, sinks)
    for idx, out in enumerate(outs):
        b, c = divmod(idx, n_pairs)
        o_ref[q_rows[b], c * LANES:(c + 1) * LANES] = out.astype(BF16)


def _attn_c(sink, cq, ckv, n, l_ctx, need_ctx):
    rows = cq.shape[0]
    gw = GROUP_WIDTH
    kvw = C_KV_HEADS * LANES
    smem = pl.BlockSpec(memory_space=pltpu.SMEM)
    yc = pl.pallas_call(
        functools.partial(_attn_c_kernel, n=n, l_ctx=l_ctx, local=True, aliased=False),
        grid=(n // TM,),
        in_specs=[
            smem,
            pl.BlockSpec((TM, gw), lambda i: (i, 0)),
            pl.BlockSpec((rows, kvw), lambda i: (0, 0)),
            pl.BlockSpec((rows, kvw), lambda i: (0, 1)),
        ],
        out_specs=pl.BlockSpec((TM, gw), lambda i: (i, 0)),
        out_shape=jax.ShapeDtypeStruct((rows, gw), BF16),
        compiler_params=_params(("parallel",)),
        name="attn_c",
    )(sink, cq, ckv, ckv)
    if not need_ctx:
        return yc
    ct = n // TM
    cb = n // l_ctx
    return pl.pallas_call(
        functools.partial(_attn_c_kernel, n=n, l_ctx=l_ctx, local=False, aliased=True),
        grid=(1,),
        in_specs=[
            smem,
            pl.BlockSpec((TM, gw), lambda i: (ct, 0)),
            pl.BlockSpec((l_ctx, kvw), lambda i: (cb, 0)),
            pl.BlockSpec((l_ctx, kvw), lambda i: (cb, 1)),
            pl.BlockSpec(memory_space=pl.ANY),
        ],
        out_specs=pl.BlockSpec((TM, gw), lambda i: (ct, 0)),
        out_shape=jax.ShapeDtypeStruct((rows, gw), BF16),
        input_output_aliases={4: 0},
        compiler_params=_params(("parallel",)),
        name="attn_c_ctx",
    )(sink, cq, ckv, ckv, yc)


def _attn_d_kernel(q_ref, kc_ref, vc_ref, *rest, local, aliased, n_rows):
    o_ref = rest[-1]
    n_pairs = D_HEADS // 2
    lane_block = lambda c: slice(c * LANES, (c + 1) * LANES)
    nk = NA_ROWS * GRID_W
    if local:
        k_ref, v_ref, b_ref = rest[0], rest[1], rest[2]
        g = pl.program_id(0)
        win0 = jnp.clip(g * D_ROWS - NA_ROWS // 2, 0, n_rows - (D_ROWS + NA_ROWS - 1))
        q_rows, offs, shifts = [], [], []
        for i in range(D_ROWS):
            r = g * D_ROWS + i
            rs = jnp.clip(r - NA_ROWS // 2, 0, n_rows - NA_ROWS)
            q_rows.append(slice(i * GRID_W, (i + 1) * GRID_W))
            offs.append(pl.multiple_of((rs - win0) * GRID_W, GRID_W))
            shifts.append(r - rs)
    else:
        q_rows = [slice(0, q_ref.shape[0])]

    def score_fn(idx):
        i, c = divmod(idx, n_pairs)
        q2 = _stack_pair(q_ref[q_rows[i], lane_block(c)])
        scores = [_dot_nt(q2, kc_ref[:, lane_block(c)])]
        if local:
            bias = jnp.concatenate([b_ref[shifts[i], 2 * c], b_ref[shifts[i], 2 * c + 1]], axis=0)
            scores.append(_dot_nt(q2, k_ref[pl.ds(offs[i], nk), lane_block(c)]) + bias)
        return scores

    values = []
    for i in range(len(q_rows)):
        for c in range(n_pairs):
            vals = [vc_ref[:, lane_block(c)]]
            if local:
                vals.append(v_ref[pl.ds(offs[i], nk), lane_block(c)])
            values.append(vals)
    outs = _paired_heads_attention(score_fn, values, len(q_rows) * n_pairs)
    for idx, out in enumerate(outs):
        i, c = divmod(idx, n_pairs)
        o_ref[q_rows[i], lane_block(c)] = out.astype(BF16)


def _attn_d(dq, dk, dv, bias, n, l_ctx, need_ctx):
    rows = dq.shape[0]
    gw = GROUP_WIDTH
    n_rows = n // GRID_W
    nk = NA_ROWS * GRID_W
    cb = n // l_ctx
    win = D_ROWS + NA_ROWS - 1
    assert n_rows % D_ROWS == 0 and n_rows >= win

    def window_start(g):
        return jnp.clip(g * D_ROWS - NA_ROWS // 2, 0, n_rows - win) * GRID_W

    once = dict(pipeline_mode=pl.Buffered(1))
    yd = pl.pallas_call(
        functools.partial(_attn_d_kernel, local=True, aliased=False, n_rows=n_rows),
        grid=(n_rows // D_ROWS,),
        in_specs=[
            pl.BlockSpec((D_ROWS * GRID_W, gw), lambda g: (g, 0)),
            pl.BlockSpec((l_ctx, gw), lambda g: (cb, 0), **once),
            pl.BlockSpec((l_ctx, gw), lambda g: (cb, 0), **once),
            pl.BlockSpec((pl.Element(win * GRID_W), pl.Element(gw)), lambda g: (window_start(g), 0)),
            pl.BlockSpec((pl.Element(win * GRID_W), pl.Element(gw)), lambda g: (window_start(g), 0)),
            pl.BlockSpec((NA_ROWS, D_HEADS, GRID_W, nk), lambda g: (0, 0, 0, 0), **once),
        ],
        out_specs=pl.BlockSpec((D_ROWS * GRID_W, gw), lambda g: (g, 0)),
        out_shape=jax.ShapeDtypeStruct((rows, gw), BF16),
        compiler_params=_params(("parallel",)),
        name="attn_d",
    )(dq, dk, dv, dk, dv, bias)
    if not need_ctx:
        return yd
    ct = n // TM
    return pl.pallas_call(
        functools.partial(_attn_d_kernel, local=False, aliased=True, n_rows=n_rows),
        grid=(1,),
        in_specs=[
            pl.BlockSpec((TM, gw), lambda i: (ct, 0)),
            pl.BlockSpec((l_ctx, gw), lambda i: (cb, 0)),
            pl.BlockSpec((l_ctx, gw), lambda i: (cb, 0)),
            pl.BlockSpec(memory_space=pl.ANY),
        ],
        out_specs=pl.BlockSpec((TM, gw), lambda i: (ct, 0)),
        out_shape=jax.ShapeDtypeStruct((rows, gw), BF16),
        input_output_aliases={3: 0},
        compiler_params=_params(("parallel",)),
        name="attn_d_ctx",
    )(dq, dk, dv, yd)


def _outproj_kernel(ya_ref, yb_ref, yc_ref, yd_ref, w_ref, x_ref, gt_ref, lng_ref, lnb_ref, o_ref):
    half = x_ref.shape[0] // 2
    halves = [slice(r * half, (r + 1) * half) for r in range(2)]
    ys = []
    for rows in halves:
        y_in = jnp.concatenate([ya_ref[rows], yb_ref[rows], yc_ref[rows], yd_ref[rows]], axis=1)
        ys.append(jnp.dot(y_in, w_ref[...], preferred_element_type=F32))
    for rows, y in zip(halves, ys):
        t = DEEPNORM_ALPHA * x_ref[rows] + gt_ref[0] * y
        o_ref[rows] = _ln(t, lng_ref[...], lnb_ref[...])


def _outproj(ya, yb, yc, yd, w_out, layer, x, mods_l, gate_k, lng, lnb, n_tiles, nlt):
    d = D_MODEL
    gw = GROUP_WIDTH
    ytile = pl.BlockSpec((TM, gw), lambda i: (i, 0))
    vec = pl.BlockSpec((1, d), lambda i: (0, 0))
    return pl.pallas_call(
        _outproj_kernel,
        grid=(n_tiles,),
        in_specs=[
            ytile, ytile, ytile, ytile,
            pl.BlockSpec((None, d, d), lambda i: (layer, 0, 0), pipeline_mode=pl.Buffered(1)),
            pl.BlockSpec((TM, d), lambda i: (i, 0)),
            pl.BlockSpec((1, 1, d), lambda i: (jnp.where(i >= nlt, N_MOD, 0) + gate_k, 0, 0)),
            vec, vec,
        ],
        out_specs=pl.BlockSpec((TM, d), lambda i: (i, 0)),
        out_shape=jax.ShapeDtypeStruct((n_tiles * TM, d), F32),
        compiler_params=_params(("parallel",)),
        name="outproj",
    )(ya, yb, yc, yd, w_out, x, mods_l, lng, lnb)


def _rope_tables(n, rows):
    t = jnp.arange(n, dtype=jnp.int32)
    row = (t // GRID_W).astype(F32)
    col = (t % GRID_W).astype(F32)
    n_freq = HEAD_DIM // 4
    inv = ROPE_THETA ** (-jnp.arange(n_freq, dtype=F32) / n_freq)
    ang = jnp.concatenate([row[:, None] * inv, col[:, None] * inv], axis=-1)
    cos, sin = jnp.cos(ang), jnp.sin(ang)
    cos_t = jnp.tile(jnp.concatenate([cos, cos], axis=-1), (1, LANES // HEAD_DIM))
    sin_t = jnp.tile(jnp.concatenate([-sin, sin], axis=-1), (1, LANES // HEAD_DIM))
    pad = rows - n
    cos_t = jnp.concatenate([cos_t, jnp.ones((pad, LANES), F32)], axis=0)
    sin_t = jnp.concatenate([sin_t, jnp.zeros((pad, LANES), F32)], axis=0)
    return cos_t, sin_t


def _na_bias(rpb):
    w = GRID_W
    cq = jnp.arange(w)
    cs = jnp.clip(cq - NA_COLS // 2, 0, w - NA_COLS)
    col_ok = (cq[None, :] >= cs[:, None]) & (cq[None, :] < cs[:, None] + NA_COLS)
    edge = w - NA_COLS
    ext = jnp.pad(rpb.astype(F32), ((0, 0), (0, 0), (edge, edge)), mode="edge")
    toep = jnp.stack([ext[:, :, w - 1 - q:2 * w - 1 - q] for q in range(w)], axis=2)
    toep = jnp.where(col_ok[None, None], toep, NEG_INF)
    per_shift = []
    for shift in range(NA_ROWS):
        sl = toep[:, NA_ROWS - 1 - shift:2 * NA_ROWS - 1 - shift]
        per_shift.append(jnp.swapaxes(sl, 1, 2).reshape(rpb.shape[0], w, NA_ROWS * w))
    return jnp.stack(per_shift, axis=0) * LOG2E


def kernel(x, c, ctx, c_ctx, w_mod, b_mod, ln_g, ln_b, ffn1_w_in, ffn1_w_out, ffn2_w_in, ffn2_w_out, mix_w_in, mix_w_out, a_lambda, b_norm_g, b_norm_b, b_spatial_w, b_spatial_b, c_sink, d_rpb):
    depth = w_mod.shape[0]
    n = x.shape[1]
    l_ctx = ctx.shape[1]
    d = D_MODEL
    assert x.shape[0] == 1 and n % TM == 0 and n // GRID_W >= NA_ROWS and n % TM_FFN == 0
    assert l_ctx % CHUNK == 0 and l_ctx <= TM and n % l_ctx == 0
    nlt = n // TM
    rows = n + TM

    cc = jnp.concatenate([c, c_ctx[None], jnp.zeros((SUBLANES - 2, d), F32)], axis=0)
    mods = _mod_vectors(cc, w_mod, b_mod)[:, :2].reshape(depth, 2 * N_MOD, 1, d)
    cos_t, sin_t = _rope_tables(n, rows)
    w1_in, w1_out, w2_in, w2_out, wm_in, wm_out = (
        w.astype(BF16) for w in (ffn1_w_in, ffn1_w_out, ffn2_w_in, ffn2_w_out, mix_w_in, mix_w_out))

    for l in range(depth):
        last = l == depth - 1
        lam_init = 0.8 - 0.6 * math.exp(-0.3 * l)
        mods_l = mods[l]
        lng = [ln_g[l, k][None] for k in range(3)]
        lnb = [ln_b[l, k][None] for k in range(3)]

        if l == 0:
            xs = _ffn(x[0], ctx[0], 0, l_ctx, mods_l, (0, 1, 2), lng[0], lnb[0], w1_in, w1_out, l, n)
        else:
            xs = _ffn(xs, xs, n, l_ctx, mods_l, (0, 1, 2), lng[0], lnb[0], w1_in, w1_out, l, n)
        aq, ak, av, bu, bv, cq, ckv, dq, dk, dv = _inproj(xs, mods_l, 3, 4, wm_in, l, cos_t, sin_t, nlt)
        ya = _attn_a(a_lambda[l], aq, ak, av, n, l_ctx, lam_init, not last)
        yb = _mixb(bu, bv, b_norm_g[l][None], b_norm_b[l][None], b_spatial_w[l], b_spatial_b[l].T)
        yc = _attn_c(c_sink[l], cq, ckv, n, l_ctx, not last)
        yd = _attn_d(dq, dk, dv, _na_bias(d_rpb[l]), n, l_ctx, not last)
        n_tiles = nlt if last else nlt + 1
        xs = _outproj(ya, yb, yc, yd, wm_out, l, xs, mods_l, 5, lng[1], lnb[1], n_tiles, nlt)
        xs = _ffn(xs, None if last else xs, n, l_ctx, mods_l, (6, 7, 8), lng[2], lnb[2], w2_in, w2_out, l, n)
    return xs[None]
```

```python
import functools
import math

import jax
import jax.numpy as jnp
from jax import lax
from jax.experimental import pallas as pl
from jax.experimental.pallas import tpu as pltpu

F32 = jnp.float32
BF16 = jnp.bfloat16

D_MODEL = 2048
N_GROUPS = 4
GROUP_WIDTH = D_MODEL // N_GROUPS
HEAD_DIM = 64
GRID_W = 64
CHUNK = 128
B_GROUPS = GROUP_WIDTH // CHUNK
A_HEADS = GROUP_WIDTH // (2 * HEAD_DIM)
C_HEADS = GROUP_WIDTH // HEAD_DIM
C_KV_HEADS = C_HEADS // 4
D_HEADS = GROUP_WIDTH // HEAD_DIM
WINDOW = 128
QBLOCK = 128
NA_ROWS = 8
NA_COLS = 16
D_FF = 256 * math.ceil(8 * D_MODEL / 3 / 256)
N_MOD = 9
ROPE_THETA = 10000.0
LN_EPS = 1e-6
NEG_INF = -1e30
MODEL_DEPTH = 4
DEEPNORM_ALPHA = (2 * MODEL_DEPTH) ** 0.25
QK_SCALE = HEAD_DIM ** -0.5
LOG2E = math.log2(math.e)

LANES = 128
SUBLANES = 8
BF16_SUBLANES = 16
A_VROWS = 2 * HEAD_DIM + BF16_SUBLANES
A_UNROLL = 16
D_ROWS = 8
TN_MOD = 2048
TM = 512
TM_FFN = 1024
TF = 512
SEG = 512
N_SEG = 10
VMEM_LIMIT = 60 * 1024 * 1024


def _params(sem):
    return pltpu.CompilerParams(dimension_semantics=sem, vmem_limit_bytes=VMEM_LIMIT)


def _ln(t, g, b):
    mu = jnp.mean(t, axis=-1, keepdims=True)
    tc = t - mu
    var = jnp.mean(tc * tc, axis=-1, keepdims=True)
    return tc * lax.rsqrt(var + LN_EPS) * g + b


def _dot_nt(a, b):
    return lax.dot_general(a, b, (((1,), (1,)), ((), ())), preferred_element_type=F32)


def _half_masks(shape):
    lane = lax.broadcasted_iota(jnp.int32, shape, 1)
    return lane < HEAD_DIM, lane >= HEAD_DIM


def _mod_kernel(cc_ref, w_ref, b_ref, o_ref):
    a = cc_ref[...]
    s = (a * jax.nn.sigmoid(a)).astype(BF16)
    o_ref[0] = jnp.dot(s, w_ref[0].astype(BF16), preferred_element_type=F32) + b_ref[0]


def _mod_vectors(cc, w_mod, b_mod):
    depth, d, nm = w_mod.shape
    tn = TN_MOD
    return pl.pallas_call(
        _mod_kernel,
        grid=(depth, nm // tn),
        in_specs=[
            pl.BlockSpec((SUBLANES, d), lambda l, j: (0, 0)),
            pl.BlockSpec((1, d, tn), lambda l, j: (l, 0, j)),
            pl.BlockSpec((1, 1, tn), lambda l, j: (l, 0, j)),
        ],
        out_specs=pl.BlockSpec((1, SUBLANES, tn), lambda l, j: (l, 0, j)),
        out_shape=jax.ShapeDtypeStruct((depth, SUBLANES, nm), F32),
        compiler_params=_params(("parallel", "parallel")),
        name="mod_vectors",
    )(cc, w_mod, b_mod.reshape(depth, 1, nm))


def _ffn_kernel(x_ref, sh_ref, sc_ref, gt_ref, lng_ref, lnb_ref, wg_ref, wu_ref, wo_ref, o_ref):
    j = pl.program_id(1)
    last_j = pl.num_programs(1) - 1
    half = x_ref.shape[0] // 2
    halves = [slice(r * half, (r + 1) * half) for r in range(2)]

    def contribution(rows):
        xin = (x_ref[rows] * (1.0 + sc_ref[0]) + sh_ref[0]).astype(BF16)
        g = jnp.dot(xin, wg_ref[...], preferred_element_type=F32)
        u = jnp.dot(xin, wu_ref[...], preferred_element_type=F32)
        hh = ((g * jax.nn.sigmoid(g)) * u).astype(BF16)
        return jnp.dot(hh, wo_ref[...], preferred_element_type=F32)

    def accumulate(rows):
        return o_ref[rows] + contribution(rows)

    @pl.when(j == 0)
    def _():
        for rows in halves:
            o_ref[rows] = contribution(rows)
        if o_ref.shape[0] > x_ref.shape[0]:
            o_ref[x_ref.shape[0]:] = jnp.zeros((o_ref.shape[0] - x_ref.shape[0], o_ref.shape[1]), F32)

    @pl.when(jnp.logical_and(j > 0, j < last_j))
    def _():
        for rows in halves:
            o_ref[rows] = accumulate(rows)

    @pl.when(j == last_j)
    def _():
        acc = [accumulate(rows) for rows in halves]
        for rows, a in zip(halves, acc):
            t = DEEPNORM_ALPHA * x_ref[rows] + (0.5 * gt_ref[0]) * a
            o_ref[rows] = _ln(t, lng_ref[...], lnb_ref[...])


def _ffn_kernel_aliased(*refs):
    _ffn_kernel(*refs[:-2], refs[-1])


def _ffn(x, x_ctx, ctx_row0, l_ctx, mods_l, ks, lng, lnb, w_in, w_out, layer, n):
    d = D_MODEL
    nj = D_FF // TF
    assert n % TM_FFN == 0 and n % TM == 0
    with_ctx = x_ctx is not None
    assert not with_ctx or (TM % l_ctx == 0 and n % l_ctx == 0 and ctx_row0 % l_ctx == 0)
    out_rows = n + TM if with_ctx else n

    def call(src, tm, tile0, tm_out, out_tile0, n_tiles, mod0, alias):
        mod_spec = lambda k: pl.BlockSpec((1, 1, d), lambda i, j: (mod0 + k, 0, 0))
        vec_spec = pl.BlockSpec((1, d), lambda i, j: (0, 0))
        in_specs = [
            pl.BlockSpec((tm, d), lambda i, j: (tile0 + i, 0)),
            mod_spec(ks[0]), mod_spec(ks[1]), mod_spec(ks[2]),
            vec_spec, vec_spec,
            pl.BlockSpec((None, d, TF), lambda i, j: (layer, 0, j)),
            pl.BlockSpec((None, d, TF), lambda i, j: (layer, 0, nj + j)),
            pl.BlockSpec((None, TF, d), lambda i, j: (layer, j, 0)),
        ]
        args = [src, mods_l, mods_l, mods_l, lng, lnb, w_in, w_in, w_out]
        kernel_fn = _ffn_kernel
        aliases = {}
        if alias is not None:
            in_specs.append(pl.BlockSpec(memory_space=pl.ANY))
            args.append(alias)
            aliases = {len(args) - 1: 0}
            kernel_fn = _ffn_kernel_aliased
        return pl.pallas_call(
            kernel_fn,
            grid=(n_tiles, nj),
            in_specs=in_specs,
            out_specs=pl.BlockSpec((tm_out, d), lambda i, j: (out_tile0 + i, 0)),
            out_shape=jax.ShapeDtypeStruct((out_rows, d), F32),
            input_output_aliases=aliases,
            compiler_params=_params(("parallel", "arbitrary")),
            name="ffn" if alias is None else "ffn_ctx",
        )(*args)

    y = call(x, TM_FFN, 0, TM_FFN, 0, n // TM_FFN, 0, None)
    if with_ctx:
        y = call(x_ctx, l_ctx, ctx_row0 // l_ctx, TM, n // TM, 1, N_MOD, y)
    return y


def _rope(t, cos_ref, sin_ref):
    w = t.shape[1]
    lane = lax.broadcasted_iota(jnp.int32, t.shape, 1)
    first = (lane % HEAD_DIM) < (HEAD_DIM // 2)
    rot = jnp.where(first, pltpu.roll(t, w - HEAD_DIM // 2, 1), pltpu.roll(t, HEAD_DIM // 2, 1))
    reps = w // LANES
    cos = jnp.concatenate([cos_ref[...]] * reps, axis=1) if reps > 1 else cos_ref[...]
    sin = jnp.concatenate([sin_ref[...]] * reps, axis=1) if reps > 1 else sin_ref[...]
    return t * cos + rot * sin


_SEG_AQ, _SEG_AK, _SEG_AV, _SEG_BU, _SEG_BV, _SEG_CQ, _SEG_CKV, _SEG_DQ, _SEG_DK, _SEG_DV = range(N_SEG)
_SEG_WIDTHS = tuple(2 * C_KV_HEADS * HEAD_DIM if k == _SEG_CKV else GROUP_WIDTH for k in range(N_SEG))
_SEG_OFFSETS = tuple(sum(_SEG_WIDTHS[:k]) for k in range(N_SEG))
D_PROJ = sum(_SEG_WIDTHS)


def _dup_halves(x):
    swapped = pltpu.roll(x, HEAD_DIM, 1)
    first, _ = _half_masks(x.shape)
    return jnp.where(first, x, swapped), jnp.where(first, swapped, x)


def _inproj_kernel(x_ref, sh_ref, sc_ref, w_ref, cos_ref, sin_ref, *out_refs):
    h = (x_ref[...] * (1.0 + sc_ref[0]) + sh_ref[0]).astype(BF16)

    def project(k):
        return jnp.dot(h, w_ref[:, _SEG_OFFSETS[k]:_SEG_OFFSETS[k] + _SEG_WIDTHS[k]], preferred_element_type=F32)

    def finish(k, t):
        o_ref = out_refs[k]
        if k in (_SEG_AQ, _SEG_AK, _SEG_CQ):
            t = _rope(t, cos_ref, sin_ref)
        if k in (_SEG_AQ, _SEG_CQ, _SEG_DQ):
            t = t * (QK_SCALE * LOG2E)
        if k == _SEG_AQ:
            o_ref[0] = t.T.astype(o_ref.dtype)
        elif k == _SEG_AV:
            tt = t.T.astype(o_ref.dtype)
            extra = A_VROWS - 2 * HEAD_DIM
            ones_row = (lax.broadcasted_iota(jnp.int32, (extra, tt.shape[1]), 0) == 0).astype(o_ref.dtype)
            for hd in range(A_HEADS):
                o_ref[0, hd * A_VROWS:hd * A_VROWS + 2 * HEAD_DIM] = tt[hd * 2 * HEAD_DIM:(hd + 1) * 2 * HEAD_DIM]
                o_ref[0, hd * A_VROWS + 2 * HEAD_DIM:(hd + 1) * A_VROWS] = ones_row
        elif k == _SEG_CKV:
            kk = _dup_halves(_rope(t[:, :LANES], cos_ref, sin_ref))
            vv = _dup_halves(t[:, LANES:])
            o_ref[...] = jnp.concatenate([kk[0], kk[1], vv[0], vv[1]], axis=1).astype(o_ref.dtype)
        else:
            o_ref[...] = t.astype(o_ref.dtype)

    t = project(0)
    for k in range(N_SEG):
        t_next = project(k + 1) if k + 1 < N_SEG else None
        finish(k, t)
        t = t_next


def _inproj(x, mods_l, shift_k, scale_k, w_in, layer, cos_t, sin_t, nlt):
    rows = x.shape[0]

    def mod_spec(k):
        return pl.BlockSpec((1, 1, D_MODEL), lambda i: (jnp.where(i >= nlt, N_MOD, 0) + k, 0, 0))

    n_tiles = rows // TM
    d = D_MODEL
    assert C_KV_HEADS == 2 and w_in.shape[1:] == (d, D_PROJ)
    widths = [SEG] * N_SEG
    dtypes = [BF16] * N_SEG
    dtypes[_SEG_BU] = F32
    dtypes[_SEG_BV] = F32
    out_specs = [pl.BlockSpec((TM, w), lambda i: (i, 0)) for w in widths]
    out_shape = [jax.ShapeDtypeStruct((rows, w), dt) for w, dt in zip(widths, dtypes)]
    for k in (_SEG_AQ, _SEG_AV):
        slab = SEG if k == _SEG_AQ else A_HEADS * A_VROWS
        out_specs[k] = pl.BlockSpec((1, slab, TM), lambda i: (i, 0, 0))
        out_shape[k] = jax.ShapeDtypeStruct((n_tiles, slab, TM), BF16)
    return pl.pallas_call(
        _inproj_kernel,
        grid=(n_tiles,),
        in_specs=[
            pl.BlockSpec((TM, d), lambda i: (i, 0)),
            mod_spec(shift_k), mod_spec(scale_k),
            pl.BlockSpec((None, d, D_PROJ), lambda i: (layer, 0, 0), pipeline_mode=pl.Buffered(1)),
            pl.BlockSpec((TM, LANES), lambda i: (i, 0)),
            pl.BlockSpec((TM, LANES), lambda i: (i, 0)),
        ],
        out_specs=out_specs,
        out_shape=out_shape,
        compiler_params=_params(("parallel",)),
        name="inproj",
    )(x, mods_l, mods_l, w_in, cos_t, sin_t)


def _attn_a_kernel(lam_ref, qt_ref, k_ref, vt_ref, *rest, n_chunks, tail_k_start, tail_chunk, tail_len, lam_init,
                   aliased):
    if aliased:
        rest = rest[1:]
    o_ref, m_scr, acc_scr, s_scr, mc_scr = rest
    qt = qt_ref[0]
    comp = lax.broadcasted_iota(jnp.int32, qt.shape, 0) < HEAD_DIM
    zero = jnp.zeros_like(qt)
    qts = (jnp.where(comp, qt, zero), jnp.where(comp, zero, qt))
    m_scr[...] = jnp.full(m_scr.shape, NEG_INF, F32)
    acc_scr[...] = jnp.zeros_like(acc_scr)

    def scores(k, slot, size):
        for c in range(2):
            s = jnp.dot(k, qts[c], preferred_element_type=F32)
            s_scr[slot, c, :size] = s
            mc_scr[slot, c] = jnp.max(s, axis=0, keepdims=True)

    def consume(slot, vt, size):
        for c in range(2):
            m_old = m_scr[c]
            m_new = jnp.maximum(m_old, mc_scr[slot, c])
            alpha = jnp.exp2(m_old - m_new)
            p = jnp.exp2(s_scr[slot, c, :size] - m_new).astype(BF16)
            acc_scr[c] = alpha * acc_scr[c] + jnp.dot(vt, p, preferred_element_type=F32)
            m_scr[c] = m_new

    def k_chunk(i):
        i = jnp.minimum(i, n_chunks - 1)
        return k_ref[pl.ds(pl.multiple_of(i * TM, TM), TM), :]

    scores(k_ref[pl.ds(tail_k_start, tail_len), :], 1, tail_len)
    if n_chunks:
        unroll = math.gcd(n_chunks, A_UNROLL)
        assert unroll % 2 == 0
        scores(k_chunk(0), 0, TM)
    consume(1, vt_ref[tail_chunk][:, :tail_len], tail_len)
    if n_chunks:
        def body(t, carry):
            j = unroll * t
            for u in range(unroll):
                scores(k_chunk(j + u + 1), (u + 1) % 2, TM)
                consume(u % 2, vt_ref[j + u], TM)
            return carry
        lax.fori_loop(0, n_chunks // unroll, body, 0)

    lv = lam_ref[...]
    lam = (jnp.exp(jnp.sum(lv[0:1] * lv[1:2], axis=1, keepdims=True))
           - jnp.exp(jnp.sum(lv[2:3] * lv[3:4], axis=1, keepdims=True)) + lam_init)
    nv = 2 * HEAD_DIM
    ot = (acc_scr[0, :nv] / acc_scr[0, nv:nv + 1]
          - lam * (acc_scr[1, :nv] / acc_scr[1, nv:nv + 1]))
    ms = jnp.mean(ot * ot, axis=0, keepdims=True)
    ot = (ot * lax.rsqrt(ms + LN_EPS)) * (1.0 - lam_init)
    o_ref[...] = ot.T.astype(BF16)


def _attn_a(lam_vecs, aqt, ak, avt, n, l_ctx, lam_init, need_ctx):
    rows = ak.shape[0]
    nt = rows // TM
    nlt = n // TM
    scratch = [pltpu.VMEM((2, 1, TM), F32), pltpu.VMEM((2, A_VROWS, TM), F32),
               pltpu.VMEM((2, 2, TM, TM), F32), pltpu.VMEM((2, 2, 1, TM), F32)]
    lam_spec = pl.BlockSpec((4, HEAD_DIM), lambda h, i: (0, 0))
    ya = pl.pallas_call(
        functools.partial(_attn_a_kernel, n_chunks=nlt, tail_k_start=n, tail_chunk=nlt, tail_len=l_ctx,
                          lam_init=lam_init, aliased=False),
        grid=(A_HEADS, nlt),
        in_specs=[
            lam_spec,
            pl.BlockSpec((1, LANES, TM), lambda h, i: (i, h, 0)),
            pl.BlockSpec((rows, LANES), lambda h, i: (0, h)),
            pl.BlockSpec((nt, A_VROWS, TM), lambda h, i: (0, h, 0)),
        ],
        out_specs=pl.BlockSpec((TM, LANES), lambda h, i: (i, h)),
        out_shape=jax.ShapeDtypeStruct((rows, GROUP_WIDTH), BF16),
        scratch_shapes=scratch,
        compiler_params=_params(("parallel", "parallel")),
        name="attn_a",
    )(lam_vecs, aqt, ak, avt)
    if not need_ctx:
        return ya
    cb = n // l_ctx
    return pl.pallas_call(
        functools.partial(_attn_a_kernel, n_chunks=0, tail_k_start=0, tail_chunk=0, tail_len=l_ctx,
                          lam_init=lam_init, aliased=True),
        grid=(A_HEADS, 1),
        in_specs=[
            lam_spec,
            pl.BlockSpec((1, LANES, TM), lambda h, i: (nlt, h, 0)),
            pl.BlockSpec((l_ctx, LANES), lambda h, i: (cb, h)),
            pl.BlockSpec((1, A_VROWS, TM), lambda h, i: (nlt, h, 0)),
            pl.BlockSpec(memory_space=pl.ANY),
        ],
        out_specs=pl.BlockSpec((TM, LANES), lambda h, i: (nlt, h)),
        out_shape=jax.ShapeDtypeStruct((rows, GROUP_WIDTH), BF16),
        scratch_shapes=scratch,
        input_output_aliases={4: 0},
        compiler_params=_params(("parallel", "parallel")),
        name="attn_a_ctx",
    )(lam_vecs, aqt, ak, avt, ya)


def _mixb_kernel(u_ref, v_ref, g_ref, b_ref, ws_ref, bs_ref, o_ref):
    u = jax.nn.gelu(u_ref[...])
    v = _ln(jax.nn.gelu(v_ref[...]), g_ref[...], b_ref[...]).astype(BF16)
    for c in range(u.shape[0] // CHUNK):
        rs = slice(c * CHUNK, (c + 1) * CHUNK)
        for g in range(B_GROUPS):
            cs = slice(g * LANES, (g + 1) * LANES)
            mixed = jnp.dot(ws_ref[g].astype(BF16), v[rs, cs], preferred_element_type=F32) + bs_ref[:, g:g + 1]
            o_ref[rs, cs] = (u[rs, cs] * mixed).astype(BF16)


def _mixb(bu, bv, gn_g, gn_b, w_s, b_s_t):
    rows = bu.shape[0]
    gw = GROUP_WIDTH
    tile = pl.BlockSpec((TM, gw), lambda i: (i, 0))
    return pl.pallas_call(
        _mixb_kernel,
        grid=(rows // TM,),
        in_specs=[
            tile, tile,
            pl.BlockSpec((1, gw), lambda i: (0, 0)),
            pl.BlockSpec((1, gw), lambda i: (0, 0)),
            pl.BlockSpec((B_GROUPS, CHUNK, CHUNK), lambda i: (0, 0, 0)),
            pl.BlockSpec((CHUNK, B_GROUPS), lambda i: (0, 0)),
        ],
        out_specs=tile,
        out_shape=jax.ShapeDtypeStruct((rows, gw), BF16),
        compiler_params=_params(("parallel",)),
        name="mix_b",
    )(bu, bv, gn_g, gn_b, w_s, b_s_t)


def _stack_pair(qb):
    first, second = _half_masks(qb.shape)
    zero = jnp.zeros_like(qb)
    return jnp.concatenate([jnp.where(first, qb, zero), jnp.where(second, qb, zero)], axis=0)


def _paired_heads_attention(score_fn, values, n_pairs, sinks=None):
    all_scores = [score_fn(c) for c in range(n_pairs)]
    all_probs = []
    for c, scores in enumerate(all_scores):
        m = jnp.max(scores[0], axis=1, keepdims=True)
        for s in scores[1:]:
            m = jnp.maximum(m, jnp.max(s, axis=1, keepdims=True))
        if sinks is not None:
            m = jnp.maximum(m, sinks[c])
        ps = [jnp.exp2(s - m) for s in scores]
        den = jnp.sum(ps[0], axis=1, keepdims=True)
        for p in ps[1:]:
            den = den + jnp.sum(p, axis=1, keepdims=True)
        if sinks is not None:
            den = den + jnp.exp2(sinks[c] - m)
        all_probs.append(([p.astype(BF16) for p in ps], den))
    outs = []
    for c in range(n_pairs):
        ps, den = all_probs[c]
        num = None
        for p, v in zip(ps, values[c]):
            pv = jnp.dot(p, v, preferred_element_type=F32)
            num = pv if num is None else num + pv
        o = num / den
        r = o.shape[0] // 2
        first, _ = _half_masks((r, LANES))
        outs.append(jnp.where(first, o[:r], o[r:]))
    return outs


def _attn_c_kernel(sink_ref, q_ref, k_ref, v_ref, *rest, n, l_ctx, local, aliased):
    o_ref = rest[-1]
    i = pl.program_id(0)
    n_sub = q_ref.shape[0] // QBLOCK
    wlen = 3 * QBLOCK
    ctx_start = n if local else 0
    n_pairs = C_HEADS // 2
    grp_pairs = n_pairs // C_KV_HEADS
    kv_block = lambda c: slice((c // grp_pairs) * LANES, (c // grp_pairs + 1) * LANES)
    top = lax.broadcasted_iota(jnp.int32, (2 * QBLOCK, 1), 0) < QBLOCK
    pair_sinks = [jnp.where(top, sink_ref[2 * c] * LOG2E, sink_ref[2 * c + 1] * LOG2E) for c in range(n_pairs)]
    sinks = [pair_sinks[c] for _ in range(n_sub) for c in range(n_pairs)]
    q_rows = [slice(b * QBLOCK, (b + 1) * QBLOCK) for b in range(n_sub)]
    if local:
        starts, valids = [], []
        for b in range(n_sub):
            blk = i * n_sub + b
            start = pl.multiple_of(jnp.clip((blk - 1) * QBLOCK, 0, n - wlen), QBLOCK)
            qpos = blk * QBLOCK + lax.broadcasted_iota(jnp.int32, (QBLOCK, wlen), 0)
            kpos = start + lax.broadcasted_iota(jnp.int32, (QBLOCK, wlen), 1)
            starts.append(start)
            valid = jnp.abs(kpos - qpos) <= WINDOW
            valids.append(jnp.concatenate([valid, valid], axis=0))

    def score_fn(idx):
        b, c = divmod(idx, n_pairs)
        q2 = _stack_pair(q_ref[q_rows[b], c * LANES:(c + 1) * LANES])
        scores = [_dot_nt(q2, k_ref[pl.ds(ctx_start, l_ctx), kv_block(c)])]
        if local:
            scores.append(jnp.where(valids[b], _dot_nt(q2, k_ref[pl.ds(starts[b], wlen), kv_block(c)]), NEG_INF))
        return scores

    values = []
    for b in range(n_sub):
        for c in range(n_pairs):
            vals = [v_ref[pl.ds(ctx_start, l_ctx), kv_block(c)]]
            if local:
                vals.append(v_ref[pl.ds(starts[b], wlen), kv_block(c)])
            values.append(vals)
    outs = _paired_heads_attention(score_fn, values, n_sub * n_pairs, sinks)
    for idx, out in enumerate(outs):
        b, c = divmod(idx, n_pairs)
        o_ref[q_rows[b], c * LANES:(c + 1) * LANES] = out.astype(BF16)


def _attn_c(sink, cq, ckv, n, l_ctx, need_ctx):
    rows = cq.shape[0]
    gw = GROUP_WIDTH
    kvw = C_KV_HEADS * LANES
    smem = pl.BlockSpec(memory_space=pltpu.SMEM)
    yc = pl.pallas_call(
        functools.partial(_attn_c_kernel, n=n, l_ctx=l_ctx, local=True, aliased=False),
        grid=(n // TM,),
        in_specs=[
            smem,
            pl.BlockSpec((TM, gw), lambda i: (i, 0)),
            pl.BlockSpec((rows, kvw), lambda i: (0, 0)),
            pl.BlockSpec((rows, kvw), lambda i: (0, 1)),
        ],
        out_specs=pl.BlockSpec((TM, gw), lambda i: (i, 0)),
        out_shape=jax.ShapeDtypeStruct((rows, gw), BF16),
        compiler_params=_params(("parallel",)),
        name="attn_c",
    )(sink, cq, ckv, ckv)
    if not need_ctx:
        return yc
    ct = n // TM
    cb = n // l_ctx
    return pl.pallas_call(
        functools.partial(_attn_c_kernel, n=n, l_ctx=l_ctx, local=False, aliased=True),
        grid=(1,),
        in_specs=[
            smem,
            pl.BlockSpec((TM, gw), lambda i: (ct, 0)),
            pl.BlockSpec((l_ctx, kvw), lambda i: (cb, 0)),
            pl.BlockSpec((l_ctx, kvw), lambda i: (cb, 1)),
            pl.BlockSpec(memory_space=pl.ANY),
        ],
        out_specs=pl.BlockSpec((TM, gw), lambda i: (ct, 0)),
        out_shape=jax.ShapeDtypeStruct((rows, gw), BF16),
        input_output_aliases={4: 0},
        compiler_params=_params(("parallel",)),
        name="attn_c_ctx",
    )(sink, cq, ckv, ckv, yc)


def _attn_d_kernel(q_ref, kc_ref, vc_ref, *rest, local, aliased, n_rows):
    o_ref = rest[-1]
    n_pairs = D_HEADS // 2
    lane_block = lambda c: slice(c * LANES, (c + 1) * LANES)
    nk = NA_ROWS * GRID_W
    if local:
        k_ref, v_ref, b_ref = rest[0], rest[1], rest[2]
        g = pl.program_id(0)
        win0 = jnp.clip(g * D_ROWS - NA_ROWS // 2, 0, n_rows - (D_ROWS + NA_ROWS - 1))
        q_rows, offs, shifts = [], [], []
        for i in range(D_ROWS):
            r = g * D_ROWS + i
            rs = jnp.clip(r - NA_ROWS // 2, 0, n_rows - NA_ROWS)
            q_rows.append(slice(i * GRID_W, (i + 1) * GRID_W))
            offs.append(pl.multiple_of((rs - win0) * GRID_W, GRID_W))
            shifts.append(r - rs)
    else:
        q_rows = [slice(0, q_ref.shape[0])]

    def score_fn(idx):
        i, c = divmod(idx, n_pairs)
        q2 = _stack_pair(q_ref[q_rows[i], lane_block(c)])
        scores = [_dot_nt(q2, kc_ref[:, lane_block(c)])]
        if local:
            bias = jnp.concatenate([b_ref[shifts[i], 2 * c], b_ref[shifts[i], 2 * c + 1]], axis=0)
            scores.append(_dot_nt(q2, k_ref[pl.ds(offs[i], nk), lane_block(c)]) + bias)
        return scores

    values = []
    for i in range(len(q_rows)):
        for c in range(n_pairs):
            vals = [vc_ref[:, lane_block(c)]]
            if local:
                vals.append(v_ref[pl.ds(offs[i], nk), lane_block(c)])
            values.append(vals)
    outs = _paired_heads_attention(score_fn, values, len(q_rows) * n_pairs)
    for idx, out in enumerate(outs):
        i, c = divmod(idx, n_pairs)
        o_ref[q_rows[i], lane_block(c)] = out.astype(BF16)


def _attn_d(dq, dk, dv, bias, n, l_ctx, need_ctx):
    rows = dq.shape[0]
    gw = GROUP_WIDTH
    n_rows = n // GRID_W
    nk = NA_ROWS * GRID_W
    cb = n // l_ctx
    win = D_ROWS + NA_ROWS - 1
    assert n_rows % D_ROWS == 0 and n_rows >= win

    def window_start(g):
        return jnp.clip(g * D_ROWS - NA_ROWS // 2, 0, n_rows - win) * GRID_W

    once = dict(pipeline_mode=pl.Buffered(1))
    yd = pl.pallas_call(
        functools.partial(_attn_d_kernel, local=True, aliased=False, n_rows=n_rows),
        grid=(n_rows // D_ROWS,),
        in_specs=[
            pl.BlockSpec((D_ROWS * GRID_W, gw), lambda g: (g, 0)),
            pl.BlockSpec((l_ctx, gw), lambda g: (cb, 0), **once),
            pl.BlockSpec((l_ctx, gw), lambda g: (cb, 0), **once),
            pl.BlockSpec((pl.Element(win * GRID_W), pl.Element(gw)), lambda g: (window_start(g), 0)),
            pl.BlockSpec((pl.Element(win * GRID_W), pl.Element(gw)), lambda g: (window_start(g), 0)),
            pl.BlockSpec((NA_ROWS, D_HEADS, GRID_W, nk), lambda g: (0, 0, 0, 0), **once),
        ],
        out_specs=pl.BlockSpec((D_ROWS * GRID_W, gw), lambda g: (g, 0)),
        out_shape=jax.ShapeDtypeStruct((rows, gw), BF16),
        compiler_params=_params(("parallel",)),
        name="attn_d",
    )(dq, dk, dv, dk, dv, bias)
    if not need_ctx:
        return yd
    ct = n // TM
    return pl.pallas_call(
        functools.partial(_attn_d_kernel, local=False, aliased=True, n_rows=n_rows),
        grid=(1,),
        in_specs=[
            pl.BlockSpec((TM, gw), lambda i: (ct, 0)),
            pl.BlockSpec((l_ctx, gw), lambda i: (cb, 0)),
            pl.BlockSpec((l_ctx, gw), lambda i: (cb, 0)),
            pl.BlockSpec(memory_space=pl.ANY),
        ],
        out_specs=pl.BlockSpec((TM, gw), lambda i: (ct, 0)),
        out_shape=jax.ShapeDtypeStruct((rows, gw), BF16),
        input_output_aliases={3: 0},
        compiler_params=_params(("parallel",)),
        name="attn_d_ctx",
    )(dq, dk, dv, yd)


def _outproj_kernel(ya_ref, yb_ref, yc_ref, yd_ref, w_ref, x_ref, gt_ref, lng_ref, lnb_ref, o_ref):
    half = x_ref.shape[0] // 2
    halves = [slice(r * half, (r + 1) * half) for r in range(2)]
    ys = []
    for rows in halves:
        y_in = jnp.concatenate([ya_ref[rows], yb_ref[rows], yc_ref[rows], yd_ref[rows]], axis=1)
        ys.append(jnp.dot(y_in, w_ref[...], preferred_element_type=F32))
    for rows, y in zip(halves, ys):
        t = DEEPNORM_ALPHA * x_ref[rows] + gt_ref[0] * y
        o_ref[rows] = _ln(t, lng_ref[...], lnb_ref[...])


def _outproj(ya, yb, yc, yd, w_out, layer, x, mods_l, gate_k, lng, lnb, n_tiles, nlt):
    d = D_MODEL
    gw = GROUP_WIDTH
    ytile = pl.BlockSpec((TM, gw), lambda i: (i, 0))
    vec = pl.BlockSpec((1, d), lambda i: (0, 0))
    return pl.pallas_call(
        _outproj_kernel,
        grid=(n_tiles,),
        in_specs=[
            ytile, ytile, ytile, ytile,
            pl.BlockSpec((None, d, d), lambda i: (layer, 0, 0), pipeline_mode=pl.Buffered(1)),
            pl.BlockSpec((TM, d), lambda i: (i, 0)),
            pl.BlockSpec((1, 1, d), lambda i: (jnp.where(i >= nlt, N_MOD, 0) + gate_k, 0, 0)),
            vec, vec,
        ],
        out_specs=pl.BlockSpec((TM, d), lambda i: (i, 0)),
        out_shape=jax.ShapeDtypeStruct((n_tiles * TM, d), F32),
        compiler_params=_params(("parallel",)),
        name="outproj",
    )(ya, yb, yc, yd, w_out, x, mods_l, lng, lnb)


def _rope_tables(n, rows):
    t = jnp.arange(n, dtype=jnp.int32)
    row = (t // GRID_W).astype(F32)
    col = (t % GRID_W).astype(F32)
    n_freq = HEAD_DIM // 4
    inv = ROPE_THETA ** (-jnp.arange(n_freq, dtype=F32) / n_freq)
    ang = jnp.concatenate([row[:, None] * inv, col[:, None] * inv], axis=-1)
    cos, sin = jnp.cos(ang), jnp.sin(ang)
    cos_t = jnp.tile(jnp.concatenate([cos, cos], axis=-1), (1, LANES // HEAD_DIM))
    sin_t = jnp.tile(jnp.concatenate([-sin, sin], axis=-1), (1, LANES // HEAD_DIM))
    pad = rows - n
    cos_t = jnp.concatenate([cos_t, jnp.ones((pad, LANES), F32)], axis=0)
    sin_t = jnp.concatenate([sin_t, jnp.zeros((pad, LANES), F32)], axis=0)
    return cos_t, sin_t


def _na_bias(rpb):
    w = GRID_W
    cq = jnp.arange(w)
    cs = jnp.clip(cq - NA_COLS // 2, 0, w - NA_COLS)
    col_ok = (cq[None, :] >= cs[:, None]) & (cq[None, :] < cs[:, None] + NA_COLS)
    edge = w - NA_COLS
    ext = jnp.pad(rpb.astype(F32), ((0, 0), (0, 0), (edge, edge)), mode="edge")
    toep = jnp.stack([ext[:, :, w - 1 - q:2 * w - 1 - q] for q in range(w)], axis=2)
    toep = jnp.where(col_ok[None, None], toep, NEG_INF)
    per_shift = []
    for shift in range(NA_ROWS):
        sl = toep[:, NA_ROWS - 1 - shift:2 * NA_ROWS - 1 - shift]
        per_shift.append(jnp.swapaxes(sl, 1, 2).reshape(rpb.shape[0], w, NA_ROWS * w))
    return jnp.stack(per_shift, axis=0) * LOG2E


def kernel(x, c, ctx, c_ctx, w_mod, b_mod, ln_g, ln_b, ffn1_w_in, ffn1_w_out, ffn2_w_in, ffn2_w_out, mix_w_in, mix_w_out, a_lambda, b_norm_g, b_norm_b, b_spatial_w, b_spatial_b, c_sink, d_rpb):
    depth = w_mod.shape[0]
    n = x.shape[1]
    l_ctx = ctx.shape[1]
    d = D_MODEL
    assert x.shape[0] == 1 and n % TM == 0 and n // GRID_W >= NA_ROWS and n % TM_FFN == 0
    assert l_ctx % CHUNK == 0 and l_ctx <= TM and n % l_ctx == 0
    nlt = n // TM
    rows = n + TM

    cc = jnp.concatenate([c, c_ctx[None], jnp.zeros((SUBLANES - 2, d), F32)], axis=0)
    mods = _mod_vectors(cc, w_mod, b_mod)[:, :2].reshape(depth, 2 * N_MOD, 1, d)
    cos_t, sin_t = _rope_tables(n, rows)
    w1_in, w1_out, w2_in, w2_out, wm_in, wm_out = (
        w.astype(BF16) for w in (ffn1_w_in, ffn1_w_out, ffn2_w_in, ffn2_w_out, mix_w_in, mix_w_out))

    for l in range(depth):
        last = l == depth - 1
        lam_init = 0.8 - 0.6 * math.exp(-0.3 * l)
        mods_l = mods[l]
        lng = [ln_g[l, k][None] for k in range(3)]
        lnb = [ln_b[l, k][None] for k in range(3)]

        if l == 0:
            xs = _ffn(x[0], ctx[0], 0, l_ctx, mods_l, (0, 1, 2), lng[0], lnb[0], w1_in, w1_out, l, n)
        else:
            xs = _ffn(xs, xs, n, l_ctx, mods_l, (0, 1, 2), lng[0], lnb[0], w1_in, w1_out, l, n)
        aq, ak, av, bu, bv, cq, ckv, dq, dk, dv = _inproj(xs, mods_l, 3, 4, wm_in, l, cos_t, sin_t, nlt)
        ya = _attn_a(a_lambda[l], aq, ak, av, n, l_ctx, lam_init, not last)
        yb = _mixb(bu, bv, b_norm_g[l][None], b_norm_b[l][None], b_spatial_w[l], b_spatial_b[l].T)
        yc = _attn_c(c_sink[l], cq, ckv, n, l_ctx, not last)
        yd = _attn_d(dq, dk, dv, _na_bias(d_rpb[l]), n, l_ctx, not last)
        n_tiles = nlt if last else nlt + 1
        xs = _outproj(ya, yb, yc, yd, wm_out, l, xs, mods_l, 5, lng[1], lnb[1], n_tiles, nlt)
        xs = _ffn(xs, None if last else xs, n, l_ctx, mods_l, (6, 7, 8), lng[2], lnb[2], w2_in, w2_out, l, n)
    return xs[None]
```
